```python
import jax, jax.numpy as jnp
from jax import lax
import numpy as np

D_MODEL = 1024
BATCH = 8
SEQ = 4096
DEPTH = 4

CTX_LEN = 256
GRID_W = 64
N_MOD = 6
ATTN_WIDTH = 512
ATTN_HEADS = 8
ATTN_HEAD_DIM = ATTN_WIDTH // ATTN_HEADS
ATTN_KV_HEADS = 2
ATTN_GROUP = ATTN_HEADS // ATTN_KV_HEADS
KV_WIDTH = ATTN_KV_HEADS * ATTN_HEAD_DIM
Q_BLOCK = 128
ROPE_THETA = 10000.0
ROPE_AXIS_DIM = ATTN_HEAD_DIM // 2
MLSTM_WIDTH = D_MODEL - ATTN_WIDTH
MLSTM_HEADS = 4
MLSTM_HEAD_DIM = MLSTM_WIDTH // MLSTM_HEADS
MLSTM_CHUNK = 128
CONV_WIDTH = 3
N_GATES = 4 * MLSTM_HEADS
FFN_DIM = 4 * D_MODEL
NORM_EPS = 1e-6
IN_COLS = ATTN_WIDTH + 2 * KV_WIDTH + 4 * MLSTM_WIDTH + N_GATES
SPLIT_IDX = (
    ATTN_WIDTH,
    ATTN_WIDTH + KV_WIDTH,
    ATTN_WIDTH + 2 * KV_WIDTH,
    ATTN_WIDTH + 2 * KV_WIDTH + 2 * MLSTM_WIDTH,
    ATTN_WIDTH + 2 * KV_WIDTH + 3 * MLSTM_WIDTH,
    ATTN_WIDTH + 2 * KV_WIDTH + 4 * MLSTM_WIDTH,
)

kernel_name = "hybrid_mlstm_gqa_dit_block"


def _rmsnorm(x, gain):
    xf = x.astype(jnp.float32)
    y = xf * lax.rsqrt(jnp.mean(xf * xf, axis=-1, keepdims=True) + NORM_EPS)
    return (y * gain.astype(jnp.float32)).astype(x.dtype)


def _modulate(x, gain, shift, scale):
    return _rmsnorm(x, gain) * (1 + scale) + shift


def _axial_rope_tables(n_tokens):
    rows = n_tokens // GRID_W
    row_idx = jnp.repeat(jnp.arange(rows, dtype=jnp.float32), GRID_W)
    col_idx = jnp.tile(jnp.arange(GRID_W, dtype=jnp.float32), rows)
    inv_freq = jnp.power(ROPE_THETA, -jnp.arange(0, ROPE_AXIS_DIM, 2, dtype=jnp.float32) / ROPE_AXIS_DIM)
    ang = jnp.concatenate([row_idx[:, None] * inv_freq, col_idx[:, None] * inv_freq], axis=-1)
    return jnp.cos(ang), jnp.sin(ang)


def _apply_rope(x, cos, sin):
    B, T, H, hd = x.shape
    xf = x.astype(jnp.float32).reshape(B, T, H, hd // 2, 2)
    x0, x1 = xf[..., 0], xf[..., 1]
    c = cos[None, :, None, :]
    s = sin[None, :, None, :]
    return jnp.stack([x0 * c - x1 * s, x0 * s + x1 * c], axis=-1).reshape(B, T, H, hd).astype(x.dtype)


def _centred_dwconv(x, w):
    T = x.shape[1]
    pad = (CONV_WIDTH - 1) // 2
    xp = jnp.pad(x, ((0, 0), (pad, pad), (0, 0)))
    return sum(xp[:, j:j + T] * w[j] for j in range(CONV_WIDTH))


def _project_stream(h, w_in, b_gates, conv_w, q_gain, k_gain, rope):
    B, T, _ = h.shape
    p = h @ w_in
    a_q, a_k, a_v, m_qk, m_v, m_o, gates = jnp.split(p, SPLIT_IDX, axis=-1)
    a_q = _rmsnorm(a_q.reshape(B, T, ATTN_HEADS, ATTN_HEAD_DIM), q_gain)
    a_k = _rmsnorm(a_k.reshape(B, T, ATTN_KV_HEADS, ATTN_HEAD_DIM), k_gain)
    a_v = a_v.reshape(B, T, ATTN_KV_HEADS, ATTN_HEAD_DIM)
    if rope is not None:
        cos, sin = rope
        a_q = _apply_rope(a_q, cos, sin)
        a_k = _apply_rope(a_k, cos, sin)
    m_q, m_k = jnp.split(jax.nn.silu(_centred_dwconv(m_qk, conv_w)), 2, axis=-1)

    def heads(t):
        return t.reshape(B, T, MLSTM_HEADS, MLSTM_HEAD_DIM).transpose(0, 2, 1, 3).astype(jnp.float32)

    m_q, m_k, m_v = heads(m_q), heads(m_k) * (MLSTM_HEAD_DIM ** -0.5), heads(m_v)
    g = (gates.astype(jnp.float32) + b_gates.astype(jnp.float32)).transpose(0, 2, 1)
    i_f, f_f, i_b, f_b = jnp.split(g, 4, axis=1)
    fwd = (m_q, m_k, m_v, i_f, jax.nn.log_sigmoid(f_f))
    bwd = (jnp.flip(m_q, 2), jnp.flip(m_k, 2), jnp.flip(m_v, 2),
           jnp.flip(i_b, -1), jax.nn.log_sigmoid(jnp.flip(f_b, -1)))
    return (a_q, a_k, a_v), fwd, bwd, jax.nn.sigmoid(m_o)


def _blocked_attention(q, k, v):
    B, T, HQ, hd = q.shape
    nb = T // Q_BLOCK
    qb = q.reshape(B, nb, Q_BLOCK, ATTN_KV_HEADS, ATTN_GROUP, hd).transpose(1, 0, 2, 3, 4, 5)
    scale = hd ** -0.5

    def block(qblk):
        s = jnp.einsum('bqkgd,bskd->bkgqs', qblk, k, preferred_element_type=jnp.float32) * scale
        p = jax.nn.softmax(s, axis=-1).astype(v.dtype)
        return jnp.einsum('bkgqs,bskd->bqkgd', p, v)

    o = lax.map(block, qb)
    return o.transpose(1, 0, 2, 3, 4, 5).reshape(B, T, HQ * hd)


def _mlstm_init(batch):
    return (jnp.zeros((batch, MLSTM_HEADS, MLSTM_HEAD_DIM, MLSTM_HEAD_DIM), jnp.float32),
            jnp.zeros((batch, MLSTM_HEADS, MLSTM_HEAD_DIM), jnp.float32),
            jnp.zeros((batch, MLSTM_HEADS), jnp.float32))


def _mlstm_scan(q, k, v, ig, lf, state, emit):
    B, H, T, d = q.shape
    nc = T // MLSTM_CHUNK

    def chunks(a):
        return jnp.moveaxis(a.reshape(B, H, nc, MLSTM_CHUNK, *a.shape[3:]), 2, 0)

    tril = jnp.tril(jnp.ones((MLSTM_CHUNK, MLSTM_CHUNK), dtype=bool))

    def step(carry, xs):
        C, n, m = carry
        qc, kc, vc, igc, lfc = xs
        b = jnp.cumsum(lfc, axis=-1)
        b_last = b[..., -1]
        g = b_last[..., None] - b + igc
        m_new = jnp.maximum(b_last + m, jnp.max(g, axis=-1))
        decay = jnp.exp(b_last + m - m_new)
        w = jnp.exp(g - m_new[..., None])
        C_new = decay[..., None, None] * C + jnp.einsum('bhsv,bhsk->bhvk', vc * w[..., None], kc)
        n_new = decay[..., None] * n + jnp.einsum('bhs,bhsk->bhk', w, kc)
        if not emit:
            return (C_new, n_new, m_new), None
        dmat = jnp.where(tril, b[..., :, None] - b[..., None, :] + igc[..., None, :], -jnp.inf)
        m_inter = b + m[..., None]
        m_t = jnp.maximum(m_inter, jnp.max(dmat, axis=-1))
        w_inter = jnp.exp(m_inter - m_t)
        s = jnp.einsum('bhtk,bhsk->bhts', qc, kc) * jnp.exp(dmat - m_t[..., None])
        num = w_inter[..., None] * jnp.einsum('bhvk,bhtk->bhtv', C, qc) + jnp.einsum('bhts,bhsv->bhtv', s, vc)
        den = w_inter * jnp.einsum('bhk,bhtk->bht', n, qc) + jnp.sum(s, axis=-1)
        h = num / jnp.maximum(jnp.abs(den), jnp.exp(-m_t))[..., None]
        return (C_new, n_new, m_new), h

    state, hs = lax.scan(step, state, tuple(chunks(a) for a in (q, k, v, ig, lf)))
    if not emit:
        return None, state
    return jnp.moveaxis(hs, 0, 2).reshape(B, H, T, d), state


def _merge_mlstm(h_f, h_b, o, gain):
    B, H, T, d = h_f.shape
    h = (h_f + h_b).transpose(0, 2, 1, 3)
    h = _rmsnorm(h, gain.reshape(H, d)).reshape(B, T, H * d)
    return (o * h).astype(o.dtype)


def _hybrid_mixer(h_lat, h_ctx, w_in, b_gates, conv_w, q_gain, k_gain, ml_gain, w_out, rope, emit_ctx):
    attn_l, fwd_l, bwd_l, o_l = _project_stream(h_lat, w_in, b_gates, conv_w, q_gain, k_gain, rope)
    attn_c, fwd_c, bwd_c, o_c = _project_stream(h_ctx, w_in, b_gates, conv_w, q_gain, k_gain, None)
    q_l, k_l, v_l = attn_l
    q_c, k_c, v_c = attn_c
    att_l = _blocked_attention(q_l, jnp.concatenate([k_c, k_l], axis=1), jnp.concatenate([v_c, v_l], axis=1))
    init = _mlstm_init(h_lat.shape[0])
    hc_f, st_f = _mlstm_scan(*fwd_c, init, emit_ctx)
    hc_b, st_b = _mlstm_scan(*bwd_c, init, emit_ctx)
    hl_f, _ = _mlstm_scan(*fwd_l, st_f, True)
    hl_b, _ = _mlstm_scan(*bwd_l, st_b, True)
    mem_l = _merge_mlstm(hl_f, jnp.flip(hl_b, 2), o_l, ml_gain)
    y_lat = jnp.concatenate([att_l, mem_l.astype(att_l.dtype)], axis=-1) @ w_out
    if not emit_ctx:
        return y_lat, None
    att_c = _blocked_attention(q_c, k_c, v_c)
    mem_c = _merge_mlstm(hc_f, jnp.flip(hc_b, 2), o_c, ml_gain)
    y_ctx = jnp.concatenate([att_c, mem_c.astype(att_c.dtype)], axis=-1) @ w_out
    return y_lat, y_ctx


def _sqrelu_mlp(h, w1, w2):
    return jnp.square(jax.nn.relu(h @ w1)) @ w2


def setup_inputs(seed: int = 0) -> dict:
    key = jax.random.key(seed)
    ks = jax.random.split(key, 20)

    def nrm(k, shape, scale):
        return jax.random.normal(k, shape, jnp.float32) * scale

    f_bias = jnp.linspace(3.0, 6.0, MLSTM_HEADS, dtype=jnp.float32)
    b_gates = jnp.concatenate([
        nrm(ks[9], (DEPTH, MLSTM_HEADS), 0.1),
        f_bias + nrm(ks[10], (DEPTH, MLSTM_HEADS), 0.1),
        nrm(ks[11], (DEPTH, MLSTM_HEADS), 0.1),
        f_bias + nrm(ks[12], (DEPTH, MLSTM_HEADS), 0.1),
    ], axis=-1)
    return {
        "x": nrm(ks[0], (BATCH, SEQ, D_MODEL), 1.0),
        "c": nrm(ks[1], (BATCH, D_MODEL), 1.0),
        "ctx": nrm(ks[2], (BATCH, CTX_LEN, D_MODEL), 1.0),
        "c_ctx": nrm(ks[3], (D_MODEL,), 1.0),
        "w_ada": nrm(ks[4], (DEPTH, D_MODEL, N_MOD * D_MODEL), 0.5 * D_MODEL ** -0.5),
        "b_ada": nrm(ks[5], (DEPTH, N_MOD * D_MODEL), 0.02),
        "norm_mix": 1.0 + nrm(ks[6], (DEPTH, D_MODEL), 0.02),
        "norm_mlp": 1.0 + nrm(ks[7], (DEPTH, D_MODEL), 0.02),
        "w_in": nrm(ks[8], (DEPTH, D_MODEL, IN_COLS), D_MODEL ** -0.5),
        "b_gates": b_gates,
        "conv_qk": nrm(ks[13], (DEPTH, CONV_WIDTH, 2 * MLSTM_WIDTH), CONV_WIDTH ** -0.5),
        "q_norm": 1.0 + nrm(ks[14], (DEPTH, ATTN_HEAD_DIM), 0.02),
        "k_norm": 1.0 + nrm(ks[15], (DEPTH, ATTN_HEAD_DIM), 0.02),
        "mlstm_norm": 1.0 + nrm(ks[16], (DEPTH, MLSTM_WIDTH), 0.02),
        "w_out": nrm(ks[17], (DEPTH, D_MODEL, D_MODEL), D_MODEL ** -0.5),
        "w_mlp_in": nrm(ks[18], (DEPTH, D_MODEL, FFN_DIM), D_MODEL ** -0.5),
        "w_mlp_out": nrm(ks[19], (DEPTH, FFN_DIM, D_MODEL), FFN_DIM ** -0.5),
        "norm_final": 1.0 + nrm(jax.random.fold_in(key, 99), (D_MODEL,), 0.02),
    }


def reference(x, c, ctx, c_ctx, w_ada, b_ada, norm_mix, norm_mlp, w_in, b_gates, conv_qk,
              q_norm, k_norm, mlstm_norm, w_out, w_mlp_in, w_mlp_out, norm_final):
    rope = _axial_rope_tables(x.shape[1])
    silu_c = jax.nn.silu(c)
    silu_cc = jax.nn.silu(c_ctx)
    for layer in range(DEPTH):
        emit_ctx = layer < DEPTH - 1
        mod_l = (silu_c @ w_ada[layer] + b_ada[layer])[:, None, :]
        mod_c = silu_cc @ w_ada[layer] + b_ada[layer]
        sh1, sc1, g1, sh2, sc2, g2 = jnp.split(mod_l, N_MOD, axis=-1)
        csh1, csc1, cg1, csh2, csc2, cg2 = jnp.split(mod_c, N_MOD, axis=-1)
        y_l, y_c = _hybrid_mixer(
            _modulate(x, norm_mix[layer], sh1, sc1),
            _modulate(ctx, norm_mix[layer], csh1, csc1),
            w_in[layer], b_gates[layer], conv_qk[layer], q_norm[layer], k_norm[layer],
            mlstm_norm[layer], w_out[layer], rope, emit_ctx)
        x = x + g1 * y_l
        x = x + g2 * _sqrelu_mlp(_modulate(x, norm_mlp[layer], sh2, sc2), w_mlp_in[layer], w_mlp_out[layer])
        if emit_ctx:
            ctx = ctx + cg1 * y_c
            ctx = ctx + cg2 * _sqrelu_mlp(_modulate(ctx, norm_mlp[layer], csh2, csc2), w_mlp_in[layer], w_mlp_out[layer])
    return _rmsnorm(x, norm_final)
```

```python
import functools

import numpy as np
import jax
import jax.numpy as jnp
from jax import lax
from jax.experimental import pallas as pl
from jax.experimental.pallas import tpu as pltpu

D_MODEL = 1024
N_MOD = 6
GRID_W = 64
ATTN_WIDTH = 512
ATTN_HEADS = 8
HEAD_DIM = 64
KV_HEADS = 2
ATTN_GROUP = ATTN_HEADS // KV_HEADS
KV_WIDTH = KV_HEADS * HEAD_DIM
ROPE_THETA = 10000.0
ROPE_AXIS_DIM = HEAD_DIM // 2
ML_WIDTH = 512
ML_HEADS = 4
ML_DIM = 128
CHUNK = 128
CONV_WIDTH = 3
N_GATES = 4 * ML_HEADS
GATE_ROWS = 8
FFN_DIM = 4 * D_MODEL
NORM_EPS = 1e-6

C_MQK = 0
C_Q = C_MQK + 2 * ML_WIDTH
C_MV = C_Q + ATTN_WIDTH
C_MO = C_MV + ML_WIDTH
C_K = C_MO + ML_WIDTH
C_V = C_K + KV_WIDTH
C_G = C_V + KV_WIDTH
W_COLS = C_G + 128

VMEM_LIMIT_BYTES = 56 * 1024 * 1024

BF16 = jnp.bfloat16
F32 = jnp.float32


def _cparams(n_grid):
    return pltpu.CompilerParams(
        dimension_semantics=("arbitrary",) * n_grid,
        vmem_limit_bytes=VMEM_LIMIT_BYTES)


def _const_spec(shape, index_map):
    return pl.BlockSpec(shape, index_map, pipeline_mode=pl.Buffered(1))


def _aligned(i, m):
    return i if isinstance(i, int) else pl.multiple_of(i, m)


def _dot(a, b):
    return jnp.dot(a, b, preferred_element_type=F32)


def _sigmoid(x):
    return 1.0 / (1.0 + jnp.exp(-x))


def _log_sigmoid(x):
    return jnp.minimum(x, 0.0) - jnp.log(1.0 + jnp.exp(-jnp.abs(x)))


def _group_ones(n, group):
    r = lax.broadcasted_iota(jnp.int32, (n, n), 0) // group
    c = lax.broadcasted_iota(jnp.int32, (n, n), 1) // group
    return jnp.where(r == c, 1.0, 0.0).astype(BF16)


def _split_dot(a, ones_mat):
    hi = a.astype(BF16)
    lo = (a - hi.astype(F32)).astype(BF16)
    return _dot(hi, ones_mat) + _dot(lo, ones_mat)


def _ada_kernel(c_ref, w_ref, b_ref, o_ref):
    c = c_ref[...]
    sc = (c * _sigmoid(c)).astype(BF16)
    o_ref[0] = _dot(sc, w_ref[0].astype(BF16)) + b_ref[0]


def _ada_call(cvec, w_ada, b_ada):
    depth, d, n = w_ada.shape
    rows = cvec.shape[0]
    tn = 1536
    return pl.pallas_call(
        _ada_kernel,
        grid=(depth, n // tn),
        in_specs=[
            pl.BlockSpec((rows, d), lambda l, j: (0, 0)),
            pl.BlockSpec((1, d, tn), lambda l, j: (l, 0, j)),
            pl.BlockSpec((1, 1, tn), lambda l, j: (l, 0, j)),
        ],
        out_specs=pl.BlockSpec((1, rows, tn), lambda l, j: (l, 0, j)),
        out_shape=jax.ShapeDtypeStruct((depth, rows, n), F32),
        compiler_params=_cparams(2),
        name="ada_mod",
    )(cvec, w_ada, b_ada.reshape(depth, 1, n))


def _rms_modulate(x, gain, shift, scale):
    ms = jnp.mean(x * x, axis=-1, keepdims=True)
    y = x * lax.rsqrt(ms + NORM_EPS) * gain
    return y * (1.0 + scale) + shift


def _swap32(x):
    lane = lax.broadcasted_iota(jnp.int32, x.shape, 1)
    up = pltpu.roll(x, 96, 1)
    down = pltpu.roll(x, 32, 1)
    return jnp.where((lane % HEAD_DIM) < ROPE_AXIS_DIM, up, down)


def _head_norm_rope(t, gain, cos, sin, use_rope):
    ss = _dot((t * t).astype(BF16), _group_ones(128, HEAD_DIM))
    tn = t * lax.rsqrt(ss * (1.0 / HEAD_DIM) + NORM_EPS) * gain
    if use_rope:
        tn = tn * cos + _swap32(tn) * sin
    return tn


def _inproj_kernel(x_ref, mod_ref, gain_ref, w_ref, qg_ref, kg_ref, bg_ref, cos_ref, sin_ref,
                   q_out, kt_out, v_out, mqk_out, mv_out, mo_out, g_out, *, use_rope):
    x = x_ref[...]
    mod = mod_ref[...]
    h = _rms_modulate(x, gain_ref[...], mod[0:1], mod[1:2]).astype(BF16)
    tm = x.shape[0]

    mqk_out[...] = _dot(h, w_ref[:, C_MQK:C_MQK + 2 * ML_WIDTH])
    mv_out[...] = _dot(h, w_ref[:, C_MV:C_MV + ML_WIDTH]).astype(BF16)
    mo_out[...] = _sigmoid(_dot(h, w_ref[:, C_MO:C_MO + ML_WIDTH])).astype(BF16)
    v_out[...] = _dot(h, w_ref[:, C_V:C_V + KV_WIDTH]).astype(BF16)

    cos = cos_ref[...] if use_rope else None
    sin = sin_ref[...] if use_rope else None

    q = _dot(h, w_ref[:, C_Q:C_Q + ATTN_WIDTH])
    qscale = HEAD_DIM ** -0.5
    for j in range(ATTN_WIDTH // 128):
        blk = _head_norm_rope(q[:, 128 * j:128 * (j + 1)], qg_ref[...], cos, sin, use_rope)
        q_out[:, 128 * j:128 * (j + 1)] = (blk * qscale).astype(BF16)

    k = _dot(h, w_ref[:, C_K:C_K + KV_WIDTH])
    kn = _head_norm_rope(k, kg_ref[...], cos, sin, use_rope)
    kt_out[...] = kn.T.astype(BF16)

    g = _dot(h, w_ref[:, C_G:C_G + 128])
    gt = g.T[0:ML_HEADS * GATE_ROWS, :] + bg_ref[...]
    row = lax.broadcasted_iota(jnp.int32, gt.shape, 0) % GATE_ROWS
    gt = jnp.where((row == 1) | (row == 3), _log_sigmoid(gt), gt)
    for j in range(tm // CHUNK):
        g_out[j] = gt[:, CHUNK * j:CHUNK * (j + 1)]


def _inproj_call(x, mod, mod_row, gain, w_all, layer, qg, kg, bg, cos, sin, *, use_rope, tm):
    bsz, t, d = x.shape
    nt = t // tm
    nc = t // CHUNK
    cpt = tm // CHUNK
    if mod_row is None:
        mod_map = lambda b, i: (b, 0, 0)
    else:
        mod_map = lambda b, i: (mod_row, 0, 0)
    tok = lambda w: pl.BlockSpec((None, tm, w), lambda b, i: (b, i, 0))
    out_shapes = (
        jax.ShapeDtypeStruct((bsz, t, ATTN_WIDTH), BF16),
        jax.ShapeDtypeStruct((bsz, KV_WIDTH, t), BF16),
        jax.ShapeDtypeStruct((bsz, t, KV_WIDTH), BF16),
        jax.ShapeDtypeStruct((bsz, t, 2 * ML_WIDTH), F32),
        jax.ShapeDtypeStruct((bsz, t, ML_WIDTH), BF16),
        jax.ShapeDtypeStruct((bsz, t, ML_WIDTH), BF16),
        jax.ShapeDtypeStruct((bsz, nc, ML_HEADS * GATE_ROWS, CHUNK), F32),
    )
    out_specs = (
        tok(ATTN_WIDTH),
        pl.BlockSpec((None, KV_WIDTH, tm), lambda b, i: (b, 0, i)),
        tok(KV_WIDTH),
        tok(2 * ML_WIDTH),
        tok(ML_WIDTH),
        tok(ML_WIDTH),
        pl.BlockSpec((None, cpt, ML_HEADS * GATE_ROWS, CHUNK), lambda b, i: (b, i, 0, 0)),
    )
    in_specs = [
        tok(d),
        pl.BlockSpec((None, N_MOD, d), mod_map),
        _const_spec((None, 1, d), lambda b, i: (layer, 0, 0)),
        _const_spec((None, d, W_COLS), lambda b, i: (layer, 0, 0)),
        _const_spec((None, 1, 128), lambda b, i: (layer, 0, 0)),
        _const_spec((None, 1, 128), lambda b, i: (layer, 0, 0)),
        _const_spec((None, ML_HEADS * GATE_ROWS, 1), lambda b, i: (layer, 0, 0)),
        pl.BlockSpec((tm, 128), lambda b, i: (i, 0)),
        pl.BlockSpec((tm, 128), lambda b, i: (i, 0)),
    ]
    return pl.pallas_call(
        functools.partial(_inproj_kernel, use_rope=use_rope),
        grid=(bsz, nt),
        in_specs=in_specs,
        out_specs=out_specs,
        out_shape=out_shapes,
        compiler_params=_cparams(2),
        name="inproj_rope" if use_rope else "inproj_ctx",
    )(x, mod, gain, w_all, qg, kg, bg, cos, sin)


def _attn_kernel(*refs, n_seg, key_chunk):
    q_ref = refs[0]
    kt_refs = refs[1:1 + n_seg]
    v_refs = refs[1 + n_seg:1 + 2 * n_seg]
    o_ref = refs[1 + 2 * n_seg]
    s_ref, m_ref, l_ref, acc_ref = refs[2 + 2 * n_seg:]
    tq = q_ref.shape[0]
    rows = ATTN_GROUP * tq

    def chunks(seg):
        s_len = v_refs[seg].shape[0]
        ck = min(key_chunk, s_len)
        return s_len // ck, ck

    def for_chunks(fn):
        base = 0
        for seg in range(n_seg):
            n, ck = chunks(seg)
            if n == 1:
                fn(seg, 0, base, ck)
            else:
                def body(c, carry, seg=seg, base=base, ck=ck):
                    fn(seg, pl.multiple_of(c * ck, ck), base, ck)
                    return carry
                lax.fori_loop(0, n, body, 0)
            base += n * ck

    for kvh in range(KV_HEADS):
        lane0 = kvh * ATTN_GROUP * HEAD_DIM
        qs = jnp.concatenate(
            [q_ref[:, lane0 + HEAD_DIM * g:lane0 + HEAD_DIM * (g + 1)] for g in range(ATTN_GROUP)],
            axis=0)
        m_ref[...] = jnp.full((rows, 128), -jnp.inf, F32)

        def scores(seg, off, base, ck, kvh=kvh, qs=qs):
            kt = kt_refs[seg][HEAD_DIM * kvh:HEAD_DIM * (kvh + 1), pl.ds(off, ck)]
            s = _dot(qs, kt)
            s_ref[:, pl.ds(_aligned(base + off, 128), ck)] = s
            m = m_ref[...]
            for j in range(ck // 128):
                m = jnp.maximum(m, s[:, 128 * j:128 * (j + 1)])
            m_ref[...] = m

        for_chunks(scores)

        m_ref[...] = jnp.broadcast_to(jnp.max(m_ref[...], axis=-1, keepdims=True), (rows, 128))
        l_ref[...] = jnp.zeros((rows, 128), F32)
        acc_ref[...] = jnp.zeros((rows, 128), F32)

        def weighted(seg, off, base, ck):
            m = m_ref[...]
            s = s_ref[:, pl.ds(_aligned(base + off, 128), ck)]
            p = jnp.exp(s - jnp.concatenate([m] * (ck // 128), axis=1))
            l = l_ref[...]
            for j in range(ck // 128):
                l = l + p[:, 128 * j:128 * (j + 1)]
            l_ref[...] = l
            acc_ref[...] += _dot(p.astype(BF16), v_refs[seg][pl.ds(off, ck), :])

        for_chunks(weighted)

        inv = 1.0 / jnp.sum(l_ref[...], axis=-1, keepdims=True)
        o = acc_ref[:, HEAD_DIM * kvh:HEAD_DIM * (kvh + 1)] * inv
        o_ref[:, lane0:lane0 + ATTN_GROUP * HEAD_DIM] = jnp.concatenate(
            [o[tq * g:tq * (g + 1)] for g in range(ATTN_GROUP)], axis=1).astype(BF16)


def _attn_call(q, kts, vs, *, tq, key_chunk=512):
    bsz, t, _ = q.shape
    n_seg = len(kts)
    s_tot = sum(v.shape[1] for v in vs)
    rows = ATTN_GROUP * tq
    in_specs = [pl.BlockSpec((None, tq, ATTN_WIDTH), lambda b, i: (b, i, 0))]
    for kt in kts:
        in_specs.append(pl.BlockSpec((None, KV_WIDTH, kt.shape[2]), lambda b, i: (b, 0, 0)))
    for v in vs:
        in_specs.append(pl.BlockSpec((None, v.shape[1], KV_WIDTH), lambda b, i: (b, 0, 0)))
    return pl.pallas_call(
        functools.partial(_attn_kernel, n_seg=n_seg, key_chunk=key_chunk),
        grid=(bsz, t // tq),
        in_specs=in_specs,
        out_specs=pl.BlockSpec((None, tq, ATTN_WIDTH), lambda b, i: (b, i, 0)),
        out_shape=jax.ShapeDtypeStruct((bsz, t, ATTN_WIDTH), BF16),
        scratch_shapes=[
            pltpu.VMEM((rows, s_tot), F32),
            pltpu.VMEM((rows, 128), F32),
            pltpu.VMEM((rows, 128), F32),
            pltpu.VMEM((rows, 128), F32),
        ],
        compiler_params=_cparams(2),
        name="attention_%dseg" % n_seg,
    )(q, *kts, *vs)


def _mlstm_kernel(*refs, emit_ctx):
    (mq_c, mk_c, mv_c, mo_c, g_c, mq_l, mk_l, mv_l, mo_l, g_l, cw_q, cw_k, gain_ref) = refs[:13]
    if emit_ctx:
        out_c, out_l = refs[13:15]
        scratch = refs[15:]
    else:
        out_c = None
        out_l = refs[13]
        scratch = refs[14:]
    q_s, kt_s, cum_s, hf_s, hb_s, ct_s, m_s = scratch

    segs = [(mq_c, mk_c, mv_c, mo_c, g_c, out_c, emit_ctx),
            (mq_l, mk_l, mv_l, mo_l, g_l, out_l, True)]
    L = CHUNK
    ri = lax.broadcasted_iota(jnp.int32, (L, L), 0)
    ci = lax.broadcasted_iota(jnp.int32, (L, L), 1)
    ones_le = jnp.where(ri <= ci, 1.0, 0.0).astype(BF16)
    ones_ge = jnp.where(ri >= ci, 1.0, 0.0).astype(BF16)
    eye = ri == ci
    mask_f = ci <= ri
    mask_b = ci >= ri
    row_i = lax.broadcasted_iota(jnp.int32, (L, ML_DIM), 0)
    ones_col = jnp.where(lax.broadcasted_iota(jnp.int32, (L, 128), 1) == 0, 1.0, 0.0).astype(BF16)
    kscale = ML_DIM ** -0.5

    chunk_base = 0
    for (mq, mk, mv, mo, g, out, emit) in segs:
        t_seg = mq.shape[0]
        n = t_seg // L

        def conv_silu(src, cw, c, t_seg=t_seg, n=n):
            t0 = pl.multiple_of(c * L, L)
            xc = src[pl.ds(t0, L), :]
            prev = src[pl.ds(jnp.maximum(t0 - 1, 0), 1), :] * jnp.where(c > 0, 1.0, 0.0)
            nxt = src[pl.ds(jnp.minimum(t0 + L, t_seg - 1), 1), :] * jnp.where(c < n - 1, 1.0, 0.0)
            xm = jnp.where(row_i == 0, prev, pltpu.roll(xc, 1, 0))
            xp = jnp.where(row_i == L - 1, nxt, pltpu.roll(xc, L - 1, 0))
            y = xm * cw[0:1, :] + xc * cw[1:2, :] + xp * cw[2:3, :]
            return y * _sigmoid(y)

        def prep(c, carry, mq=mq, mk=mk, base=chunk_base, conv_silu=conv_silu):
            t0g = pl.multiple_of((base + c) * L, L)
            q_s[pl.ds(t0g, L), :] = conv_silu(mq, cw_q, c).astype(BF16)
            kt_s[base + c] = (conv_silu(mk, cw_k, c) * kscale).T.astype(BF16)
            return carry

        lax.fori_loop(0, n, prep, 0)
        g2 = g[...].reshape(n * GATE_ROWS, L)
        cum_s[0, chunk_base:chunk_base + n] = _split_dot(g2, ones_le).reshape(n, GATE_ROWS, L)
        cum_s[1, chunk_base:chunk_base + n] = _split_dot(g2, ones_ge).reshape(n, GATE_ROWS, L)
        chunk_base += n

    ct_s[...] = jnp.zeros(ct_s.shape, F32)
    m_s[...] = jnp.zeros(m_s.shape, F32)

    def direction(d, qc, ktc, vc, ig, brow, total, mask, emit):
        m_prev = m_s[d, 0:1, 0:1]
        ct = ct_s[d]
        v_aug = jnp.concatenate([vc, ones_col], axis=1)
        h = None
        if emit:
            bcol = jnp.sum(jnp.where(eye, brow, 0.0), axis=-1, keepdims=True)
            dmat = jnp.where(mask, bcol - brow + ig, -jnp.inf)
            m_t = jnp.maximum(bcol + m_prev, jnp.max(dmat, axis=-1, keepdims=True))
            s = (_dot(qc, ktc) * jnp.exp(dmat - m_t)).astype(BF16)
            w_inter = jnp.exp(bcol + m_prev - m_t)
            tot = w_inter * _dot(qc, ct.astype(BF16)) + _dot(s, v_aug)
            den = tot[:, ML_DIM:ML_DIM + 1]
            h = tot[:, 0:ML_DIM] / jnp.maximum(jnp.abs(den), jnp.exp(-m_t))
        g_row = total - brow + ig
        m_new = jnp.maximum(total + m_prev, jnp.max(g_row, axis=-1, keepdims=True))
        decay = jnp.exp(total + m_prev - m_new)
        w_row = jnp.exp(g_row - m_new)
        kw = (ktc.astype(F32) * w_row).astype(BF16)
        ct_s[d] = decay * ct + _dot(kw, v_aug)
        m_s[d] = jnp.broadcast_to(m_new, m_s.shape[1:])
        return h

    chunk_base = 0
    row_base = 0
    for (mq, mk, mv, mo, g, out, emit) in segs:
        t_seg = mq.shape[0]
        n = t_seg // L

        def step(i, carry, mv=mv, g=g, n=n, cb=chunk_base, rb=row_base, emit=emit):
            for d, c in ((0, i), (1, n - 1 - i)):
                t0 = pl.multiple_of(c * L, L)
                t0g = pl.multiple_of(rb + c * L, L)
                gr = g[c]
                cum = cum_s[d, cb + c]
                if d == 0:
                    ig, brow = gr[0:1], cum[1:2]
                    total = brow[:, L - 1:L]
                    mask = mask_f
                else:
                    ig, brow = gr[2:3], cum[3:4]
                    total = brow[:, 0:1]
                    mask = mask_b
                h = direction(d, q_s[pl.ds(t0g, L), :], kt_s[cb + c], mv[pl.ds(t0, L), :],
                              ig, brow, total, mask, emit)
                if emit:
                    (hf_s if d == 0 else hb_s)[pl.ds(t0g, L), :] = h
            return carry

        lax.fori_loop(0, n, step, 0)

        if emit:
            def merge(c, carry, mo=mo, out=out, rb=row_base):
                t0 = pl.multiple_of(c * L, L)
                t0g = pl.multiple_of(rb + c * L, L)
                hsum = hf_s[pl.ds(t0g, L), :] + hb_s[pl.ds(t0g, L), :]
                ms = jnp.mean(hsum * hsum, axis=-1, keepdims=True)
                hn = hsum * lax.rsqrt(ms + NORM_EPS) * gain_ref[...]
                out[pl.ds(t0, L), :] = (mo[pl.ds(t0, L), :].astype(F32) * hn).astype(BF16)
                return carry
            lax.fori_loop(0, n, merge, 0)
        chunk_base += n
        row_base += t_seg


def _mlstm_call(ctx_p, lat_p, conv_w, ml_gain, layer, *, emit_ctx):
    bsz = lat_p[0].shape[0]
    in_specs = []
    args = []
    t_tot = 0
    for (mqk, mv, mo, g) in (ctx_p, lat_p):
        t = mqk.shape[1]
        n = t // CHUNK
        t_tot += t
        in_specs += [
            pl.BlockSpec((None, t, ML_DIM), lambda b, h: (b, 0, h)),
            pl.BlockSpec((None, t, ML_DIM), lambda b, h: (b, 0, ML_HEADS + h)),
            pl.BlockSpec((None, t, ML_DIM), lambda b, h: (b, 0, h)),
            pl.BlockSpec((None, t, ML_DIM), lambda b, h: (b, 0, h)),
            pl.BlockSpec((None, n, GATE_ROWS, CHUNK), lambda b, h: (b, 0, h, 0)),
        ]
        args += [mqk, mqk, mv, mo, g]
    in_specs += [
        pl.BlockSpec((None, CONV_WIDTH, ML_DIM), lambda b, h: (layer, 0, h)),
        pl.BlockSpec((None, CONV_WIDTH, ML_DIM), lambda b, h: (layer, 0, ML_HEADS + h)),
        pl.BlockSpec((None, 1, ML_DIM), lambda b, h: (layer, 0, h)),
    ]
    args += [conv_w, conv_w, ml_gain]
    t_c, t_l = ctx_p[0].shape[1], lat_p[0].shape[1]
    out_l_shape = jax.ShapeDtypeStruct((bsz, t_l, ML_WIDTH), BF16)
    out_l_spec = pl.BlockSpec((None, t_l, ML_DIM), lambda b, h: (b, 0, h))
    if emit_ctx:
        out_shape = (jax.ShapeDtypeStruct((bsz, t_c, ML_WIDTH), BF16), out_l_shape)
        out_specs = (pl.BlockSpec((None, t_c, ML_DIM), lambda b, h: (b, 0, h)), out_l_spec)
    else:
        out_shape = out_l_shape
        out_specs = out_l_spec
    n_tot = t_tot // CHUNK
    res = pl.pallas_call(
        functools.partial(_mlstm_kernel, emit_ctx=emit_ctx),
        grid=(bsz, ML_HEADS),
        in_specs=in_specs,
        out_specs=out_specs,
        out_shape=out_shape,
        scratch_shapes=[
            pltpu.VMEM((t_tot, ML_DIM), BF16),
            pltpu.VMEM((n_tot, ML_DIM, CHUNK), BF16),
            pltpu.VMEM((2, n_tot, GATE_ROWS, CHUNK), F32),
            pltpu.VMEM((t_tot, ML_DIM), F32),
            pltpu.VMEM((t_tot, ML_DIM), F32),
            pltpu.VMEM((2, ML_DIM, 2 * ML_DIM), F32),
            pltpu.VMEM((2, 8, 128), F32),
        ],
        compiler_params=_cparams(2),
        name="mlstm_emit" if emit_ctx else "mlstm_last",
    )(*args)
    if emit_ctx:
        return res
    return None, res


def _outmlp_kernel(*refs, final, ffn_chunk):
    x_ref, att_ref, mem_ref, mod_ref, wo_ref, gain_ref, w1_ref, w2_ref = refs[:8]
    if final:
        gf_ref, o_ref = refs[8:]
    else:
        o_ref = refs[8]
    mod = mod_ref[...]
    y = _dot(att_ref[...], wo_ref[0:ATTN_WIDTH, :]) + _dot(mem_ref[...], wo_ref[ATTN_WIDTH:D_MODEL, :])
    x1 = x_ref[...] + mod[2:3] * y
    h = _rms_modulate(x1, gain_ref[...], mod[3:4], mod[4:5]).astype(BF16)
    acc = jnp.zeros(x1.shape, F32)
    for j in range(FFN_DIM // ffn_chunk):
        f = jnp.maximum(_dot(h, w1_ref[:, ffn_chunk * j:ffn_chunk * (j + 1)]), 0.0)
        acc = acc + _dot((f * f).astype(BF16), w2_ref[ffn_chunk * j:ffn_chunk * (j + 1), :])
    x2 = x1 + mod[5:6] * acc
    if final:
        ms = jnp.mean(x2 * x2, axis=-1, keepdims=True)
        x2 = x2 * lax.rsqrt(ms + NORM_EPS) * gf_ref[...]
    o_ref[...] = x2


def _outmlp_call(x, att, mem, mod, mod_row, wo, gain, w1, w2, layer, gf, *, tm, ffn_chunk=1024):
    bsz, t, d = x.shape
    final = gf is not None
    if mod_row is None:
        mod_map = lambda b, i: (b, 0, 0)
    else:
        mod_map = lambda b, i: (mod_row, 0, 0)
    in_specs = [
        pl.BlockSpec((None, tm, d), lambda b, i: (b, i, 0)),
        pl.BlockSpec((None, tm, ATTN_WIDTH), lambda b, i: (b, i, 0)),
        pl.BlockSpec((None, tm, ML_WIDTH), lambda b, i: (b, i, 0)),
        pl.BlockSpec((None, N_MOD, d), mod_map),
        _const_spec((None, d, d), lambda b, i: (layer, 0, 0)),
        _const_spec((None, 1, d), lambda b, i: (layer, 0, 0)),
        _const_spec((None, d, FFN_DIM), lambda b, i: (layer, 0, 0)),
        _const_spec((None, FFN_DIM, d), lambda b, i: (layer, 0, 0)),
    ]
    args = [x, att, mem, mod, wo, gain, w1, w2]
    if final:
        in_specs.append(_const_spec((1, d), lambda b, i: (0, 0)))
        args.append(gf)
    return pl.pallas_call(
        functools.partial(_outmlp_kernel, final=final, ffn_chunk=ffn_chunk),
        grid=(bsz, t // tm),
        in_specs=in_specs,
        out_specs=pl.BlockSpec((None, tm, d), lambda b, i: (b, i, 0)),
        out_shape=jax.ShapeDtypeStruct((bsz, t, d), F32),
        compiler_params=_cparams(2),
        name="outproj_mlp_final" if final else "outproj_mlp",
    )(*args)


def _projection_columns():
    a_q, a_k, a_v = 0, ATTN_WIDTH, ATTN_WIDTH + KV_WIDTH
    m_qk = ATTN_WIDTH + 2 * KV_WIDTH
    m_v = m_qk + 2 * ML_WIDTH
    m_o = m_v + ML_WIDTH
    gates = m_o + ML_WIDTH
    deint = np.concatenate([np.arange(0, HEAD_DIM, 2), np.arange(1, HEAD_DIM, 2)])
    idx = np.full((W_COLS,), -1, np.int64)
    idx[C_MQK:C_MQK + 2 * ML_WIDTH] = m_qk + np.arange(2 * ML_WIDTH)
    idx[C_Q:C_Q + ATTN_WIDTH] = a_q + (HEAD_DIM * np.arange(ATTN_HEADS)[:, None] + deint[None, :]).reshape(-1)
    idx[C_MV:C_MV + ML_WIDTH] = m_v + np.arange(ML_WIDTH)
    idx[C_MO:C_MO + ML_WIDTH] = m_o + np.arange(ML_WIDTH)
    idx[C_K:C_K + KV_WIDTH] = a_k + (HEAD_DIM * np.arange(KV_HEADS)[:, None] + deint[None, :]).reshape(-1)
    idx[C_V:C_V + KV_WIDTH] = a_v + np.arange(KV_WIDTH)
    for h in range(ML_HEADS):
        for kind in range(4):
            idx[C_G + GATE_ROWS * h + kind] = gates + kind * ML_HEADS + h
    return idx, deint


def _rope_tables(t):
    rows = t // GRID_W
    row_idx = jnp.repeat(jnp.arange(rows, dtype=F32), GRID_W)
    col_idx = jnp.tile(jnp.arange(GRID_W, dtype=F32), rows)
    inv_freq = jnp.power(ROPE_THETA, -jnp.arange(0, ROPE_AXIS_DIM, 2, dtype=F32) / ROPE_AXIS_DIM)
    ang = jnp.concatenate([row_idx[:, None] * inv_freq, col_idx[:, None] * inv_freq], axis=-1)
    cos, sin = jnp.cos(ang), jnp.sin(ang)
    cos128 = jnp.tile(cos, (1, 4))
    sin128 = jnp.tile(jnp.concatenate([-sin, sin], axis=-1), (1, 2))
    return cos128, sin128


def _pick_tile(t, pref):
    tm = min(pref, t)
    while t % tm:
        tm //= 2
    return tm


def kernel(x, c, ctx, c_ctx, w_ada, b_ada, norm_mix, norm_mlp, w_in, b_gates, conv_qk,
           q_norm, k_norm, mlstm_norm, w_out, w_mlp_in, w_mlp_out, norm_final):
    bsz, t, d = x.shape
    t_c = ctx.shape[1]
    depth = w_ada.shape[0]
    assert d == D_MODEL and t % CHUNK == 0 and t_c % CHUNK == 0 and t % GRID_W == 0

    rows = -(-(bsz + 1) // 8) * 8
    cvec = jnp.zeros((rows, d), F32).at[:bsz].set(c).at[bsz].set(c_ctx)
    mod = _ada_call(cvec, w_ada, b_ada).reshape(depth, rows, N_MOD, d)

    idx, deint = _projection_columns()
    w_all = jnp.where(idx[None, None, :] >= 0, jnp.take(w_in, np.maximum(idx, 0), axis=2), 0.0).astype(BF16)
    qg = jnp.tile(q_norm[:, deint], (1, 2)).reshape(depth, 1, 128)
    kg = jnp.tile(k_norm[:, deint], (1, 2)).reshape(depth, 1, 128)
    norm_mix = norm_mix.reshape(depth, 1, d)
    norm_mlp = norm_mlp.reshape(depth, 1, d)
    mlstm_norm = mlstm_norm.reshape(depth, 1, ML_WIDTH)
    bg = jnp.zeros((depth, ML_HEADS, GATE_ROWS), F32).at[:, :, 0:4].set(
        b_gates.reshape(depth, 4, ML_HEADS).transpose(0, 2, 1)).reshape(depth, ML_HEADS * GATE_ROWS, 1)
    wo = w_out.astype(BF16)
    w1 = w_mlp_in.astype(BF16)
    w2 = w_mlp_out.astype(BF16)
    cos128, sin128 = _rope_tables(t)
    gf = norm_final.reshape(1, d)

    tm_l = _pick_tile(t, 512)
    tm_c = _pick_tile(t_c, 512)
    tq_l = _pick_tile(t, 128)
    tq_c = _pick_tile(t_c, 128)

    for layer in range(depth):
        emit_ctx = layer < depth - 1
        mod_l = mod[layer]
        q_l, kt_l, v_l, mqk_l, mv_l, mo_l, g_l = _inproj_call(
            x, mod_l, None, norm_mix, w_all, layer, qg, kg, bg, cos128, sin128, use_rope=True, tm=tm_l)
        q_c, kt_c, v_c, mqk_c, mv_c, mo_c, g_c = _inproj_call(
            ctx, mod_l, bsz, norm_mix, w_all, layer, qg, kg, bg, cos128, sin128, use_rope=False, tm=tm_c)
        att_l = _attn_call(q_l, [kt_c, kt_l], [v_c, v_l], tq=tq_l)
        mem_c, mem_l = _mlstm_call((mqk_c, mv_c, mo_c, g_c), (mqk_l, mv_l, mo_l, g_l),
                                   conv_qk, mlstm_norm, layer, emit_ctx=emit_ctx)
        x = _outmlp_call(x, att_l, mem_l, mod_l, None, wo, norm_mlp, w1, w2, layer,
                         None if emit_ctx else gf, tm=tm_l)
        if emit_ctx:
            att_c = _attn_call(q_c, [kt_c], [v_c], tq=tq_c)
            ctx = _outmlp_call(ctx, att_c, mem_c, mod_l, bsz, wo, norm_mlp, w1, w2, layer, None, tm=tm_c)
    return x
```

```python
import functools

import numpy as np
import jax
import jax.numpy as jnp
from jax import lax
from jax.experimental import pallas as pl
from jax.experimental.pallas import tpu as pltpu

D_MODEL = 1024
N_MOD = 6
GRID_W = 64
ATTN_WIDTH = 512
ATTN_HEADS = 8
HEAD_DIM = 64
KV_HEADS = 2
ATTN_GROUP = ATTN_HEADS // KV_HEADS
KV_WIDTH = KV_HEADS * HEAD_DIM
ROPE_THETA = 10000.0
ROPE_AXIS_DIM = HEAD_DIM // 2
ML_WIDTH = 512
ML_HEADS = 4
ML_DIM = 128
CHUNK = 128
CONV_WIDTH = 3
N_GATES = 4 * ML_HEADS
GATE_ROWS = 8
FFN_DIM = 4 * D_MODEL
NORM_EPS = 1e-6
LOG2_E = 1.4426950408889634

C_MQK = 0
C_Q = C_MQK + 2 * ML_WIDTH
C_MV = C_Q + ATTN_WIDTH
C_MO = C_MV + ML_WIDTH
C_K = C_MO + ML_WIDTH
C_V = C_K + KV_WIDTH
C_G = C_V + KV_WIDTH
W_COLS = C_G + 128

VMEM_LIMIT_BYTES = 56 * 1024 * 1024

BF16 = jnp.bfloat16
F32 = jnp.float32


def _cparams(n_grid):
    return pltpu.CompilerParams(
        dimension_semantics=("arbitrary",) * n_grid,
        vmem_limit_bytes=VMEM_LIMIT_BYTES)


def _const_spec(shape, index_map):
    return pl.BlockSpec(shape, index_map, pipeline_mode=pl.Buffered(1))


def _aligned(i, m):
    return i if isinstance(i, int) else pl.multiple_of(i, m)


def _dot(a, b):
    return jnp.dot(a, b, preferred_element_type=F32)


def _sigmoid(x):
    return 1.0 / (1.0 + jnp.exp(-x))


def _log_sigmoid(x):
    return jnp.minimum(x, 0.0) - jnp.log(1.0 + jnp.exp(-jnp.abs(x)))


def _group_ones(n, group):
    r = lax.broadcasted_iota(jnp.int32, (n, n), 0) // group
    c = lax.broadcasted_iota(jnp.int32, (n, n), 1) // group
    return jnp.where(r == c, 1.0, 0.0).astype(BF16)


def _split_dot(a, ones_mat):
    hi = a.astype(BF16)
    lo = (a - hi.astype(F32)).astype(BF16)
    return _dot(hi, ones_mat) + _dot(lo, ones_mat)


def _ada_kernel(c_ref, w_ref, b_ref, o_ref):
    c = c_ref[...]
    sc = (c * _sigmoid(c)).astype(BF16)
    o_ref[0] = _dot(sc, w_ref[0].astype(BF16)) + b_ref[0]


def _ada_call(cvec, w_ada, b_ada):
    depth, d, n = w_ada.shape
    rows = cvec.shape[0]
    tn = 1536
    return pl.pallas_call(
        _ada_kernel,
        grid=(depth, n // tn),
        in_specs=[
            pl.BlockSpec((rows, d), lambda l, j: (0, 0)),
            pl.BlockSpec((1, d, tn), lambda l, j: (l, 0, j)),
            pl.BlockSpec((1, 1, tn), lambda l, j: (l, 0, j)),
        ],
        out_specs=pl.BlockSpec((1, rows, tn), lambda l, j: (l, 0, j)),
        out_shape=jax.ShapeDtypeStruct((depth, rows, n), F32),
        compiler_params=_cparams(2),
        name="ada_mod",
    )(cvec, w_ada, b_ada.reshape(depth, 1, n))


def _rms_modulate(x, gain, shift, scale):
    ms = jnp.mean(x * x, axis=-1, keepdims=True)
    y = x * lax.rsqrt(ms + NORM_EPS) * gain
    return y * (1.0 + scale) + shift


def _swap32(x):
    lane = lax.broadcasted_iota(jnp.int32, x.shape, 1)
    up = pltpu.roll(x, 96, 1)
    down = pltpu.roll(x, 32, 1)
    return jnp.where((lane % HEAD_DIM) < ROPE_AXIS_DIM, up, down)


def _head_norm_rope(t, gain, cos, sin, use_rope):
    ss = _dot((t * t).astype(BF16), _group_ones(128, HEAD_DIM))
    tn = t * lax.rsqrt(ss * (1.0 / HEAD_DIM) + NORM_EPS) * gain
    if use_rope:
        tn = tn * cos + _swap32(tn) * sin
    return tn


def _inproj_kernel(x_ref, mod_ref, gain_ref, w_ref, qg_ref, kg_ref, bg_ref, cos_ref, sin_ref,
                   qt_out, k_out, vt_out, mqk_out, mv_out, mo_out, g_out, *, use_rope):
    x = x_ref[...]
    mod = mod_ref[...]
    h = _rms_modulate(x, gain_ref[...], mod[0:1], mod[1:2]).astype(BF16)
    tm = x.shape[0]

    mqk_out[...] = _dot(h, w_ref[:, C_MQK:C_MQK + 2 * ML_WIDTH])
    mv_out[...] = _dot(h, w_ref[:, C_MV:C_MV + ML_WIDTH]).astype(BF16)
    mo_out[...] = _sigmoid(_dot(h, w_ref[:, C_MO:C_MO + ML_WIDTH])).astype(BF16)

    vt = _dot(h, w_ref[:, C_V:C_V + KV_WIDTH]).T
    pad_row = lax.broadcasted_iota(jnp.int32, (HEAD_DIM, tm), 0)
    pad = jnp.where(pad_row == 0, 1.0, 0.0)
    for kvh in range(KV_HEADS):
        vt_out[128 * kvh:128 * kvh + HEAD_DIM, :] = vt[HEAD_DIM * kvh:HEAD_DIM * (kvh + 1)].astype(BF16)
        vt_out[128 * kvh + HEAD_DIM:128 * (kvh + 1), :] = pad.astype(BF16)

    cos = cos_ref[...] if use_rope else None
    sin = sin_ref[...] if use_rope else None

    q = _dot(h, w_ref[:, C_Q:C_Q + ATTN_WIDTH])
    qscale = HEAD_DIM ** -0.5 * LOG2_E
    for j in range(ATTN_WIDTH // 128):
        blk = _head_norm_rope(q[:, 128 * j:128 * (j + 1)], qg_ref[...], cos, sin, use_rope)
        qt_out[128 * j:128 * (j + 1), :] = (blk * qscale).T.astype(BF16)

    k = _dot(h, w_ref[:, C_K:C_K + KV_WIDTH])
    k_out[...] = _head_norm_rope(k, kg_ref[...], cos, sin, use_rope).astype(BF16)

    g = _dot(h, w_ref[:, C_G:C_G + 128])
    gt = g.T[0:ML_HEADS * GATE_ROWS, :] + bg_ref[...]
    row = lax.broadcasted_iota(jnp.int32, gt.shape, 0) % GATE_ROWS
    gt = jnp.where((row == 1) | (row == 3), _log_sigmoid(gt), gt)
    for j in range(tm // CHUNK):
        g_out[j] = gt[:, CHUNK * j:CHUNK * (j + 1)]


def _inproj_call(x, mod, mod_row, gain, w_all, layer, qg, kg, bg, cos, sin, *, use_rope, tm):
    bsz, t, d = x.shape
    nt = t // tm
    nc = t // CHUNK
    cpt = tm // CHUNK
    if mod_row is None:
        mod_map = lambda b, i: (b, 0, 0)
    else:
        mod_map = lambda b, i: (mod_row, 0, 0)
    tok = lambda w: pl.BlockSpec((None, tm, w), lambda b, i: (b, i, 0))
    out_shapes = (
        jax.ShapeDtypeStruct((bsz, ATTN_WIDTH, t), BF16),
        jax.ShapeDtypeStruct((bsz, t, KV_WIDTH), BF16),
        jax.ShapeDtypeStruct((bsz, KV_HEADS * 128, t), BF16),
        jax.ShapeDtypeStruct((bsz, t, 2 * ML_WIDTH), F32),
        jax.ShapeDtypeStruct((bsz, t, ML_WIDTH), BF16),
        jax.ShapeDtypeStruct((bsz, t, ML_WIDTH), BF16),
        jax.ShapeDtypeStruct((bsz, nc, ML_HEADS * GATE_ROWS, CHUNK), F32),
    )
    out_specs = (
        pl.BlockSpec((None, ATTN_WIDTH, tm), lambda b, i: (b, 0, i)),
        tok(KV_WIDTH),
        pl.BlockSpec((None, KV_HEADS * 128, tm), lambda b, i: (b, 0, i)),
        tok(2 * ML_WIDTH),
        tok(ML_WIDTH),
        tok(ML_WIDTH),
        pl.BlockSpec((None, cpt, ML_HEADS * GATE_ROWS, CHUNK), lambda b, i: (b, i, 0, 0)),
    )
    in_specs = [
        tok(d),
        pl.BlockSpec((None, N_MOD, d), mod_map),
        _const_spec((None, 1, d), lambda b, i: (layer, 0, 0)),
        _const_spec((None, d, W_COLS), lambda b, i: (layer, 0, 0)),
        _const_spec((None, 1, 128), lambda b, i: (layer, 0, 0)),
        _const_spec((None, 1, 128), lambda b, i: (layer, 0, 0)),
        _const_spec((None, ML_HEADS * GATE_ROWS, 1), lambda b, i: (layer, 0, 0)),
        pl.BlockSpec((tm, 128), lambda b, i: (i, 0)),
        pl.BlockSpec((tm, 128), lambda b, i: (i, 0)),
    ]
    return pl.pallas_call(
        functools.partial(_inproj_kernel, use_rope=use_rope),
        grid=(bsz, nt),
        in_specs=in_specs,
        out_specs=out_specs,
        out_shape=out_shapes,
        compiler_params=_cparams(2),
        name="inproj_rope" if use_rope else "inproj_ctx",
    )(x, mod, gain, w_all, qg, kg, bg, cos, sin)


def _attn_kernel(*refs, n_seg, key_chunk):
    qt_ref = refs[0]
    k_refs = refs[1:1 + n_seg]
    vt_refs = refs[1 + n_seg:1 + 2 * n_seg]
    o_ref = refs[1 + 2 * n_seg]
    st_ref, acc_ref = refs[2 + 2 * n_seg:]
    tq = qt_ref.shape[1]
    cols = ATTN_GROUP * tq

    def for_chunks(fn, carry):
        base = 0
        for seg in range(n_seg):
            s_len = k_refs[seg].shape[0]
            ck = min(key_chunk, s_len)
            n = s_len // ck
            if n == 1:
                carry = fn(seg, 0, base, ck, carry)
            else:
                def body(c, carry, seg=seg, base=base, ck=ck):
                    return fn(seg, pl.multiple_of(c * ck, ck), base, ck, carry)
                carry = lax.fori_loop(0, n, body, carry)
            base += n * ck
        return carry

    zeros = jnp.zeros((HEAD_DIM, tq), BF16)
    for kvh in range(KV_HEADS):
        pieces = []
        for g in range(ATTN_GROUP):
            row0 = HEAD_DIM * (ATTN_GROUP * kvh + g)
            qg = qt_ref[row0:row0 + HEAD_DIM, :]
            pieces.append(jnp.concatenate([qg, zeros] if kvh == 0 else [zeros, qg], axis=0))
        qz = jnp.concatenate(pieces, axis=1)

        def scores(seg, off, base, ck, mx, qz=qz):
            st = _dot(k_refs[seg][pl.ds(off, ck), :], qz)
            st_ref[pl.ds(_aligned(base + off, 128), ck), :] = st
            return jnp.maximum(mx, jnp.max(st.reshape(ck // 8, 8, cols), axis=0))

        mx = for_chunks(scores, jnp.full((8, cols), -jnp.inf, F32))
        m = jnp.max(mx, axis=0, keepdims=True)
        acc_ref[...] = jnp.zeros((128, cols), F32)

        def weighted(seg, off, base, ck, carry, kvh=kvh, m=m):
            p = jnp.exp2(st_ref[pl.ds(_aligned(base + off, 128), ck), :] - m).astype(BF16)
            acc_ref[...] += _dot(vt_refs[seg][128 * kvh:128 * (kvh + 1), pl.ds(off, ck)], p)
            return carry

        for_chunks(weighted, 0)

        acc = acc_ref[...]
        ot = acc[0:HEAD_DIM] * (1.0 / acc[HEAD_DIM:HEAD_DIM + 1])
        for pair in range(ATTN_GROUP // 2):
            two = jnp.concatenate([ot[:, tq * (2 * pair):tq * (2 * pair + 1)],
                                   ot[:, tq * (2 * pair + 1):tq * (2 * pair + 2)]], axis=0)
            lane0 = HEAD_DIM * (ATTN_GROUP * kvh + 2 * pair)
            o_ref[:, lane0:lane0 + 2 * HEAD_DIM] = two.T.astype(BF16)


def _attn_call(qt, ks, vts, *, tq, key_chunk=2048):
    bsz, _, t = qt.shape
    n_seg = len(ks)
    s_tot = sum(k.shape[1] for k in ks)
    cols = ATTN_GROUP * tq
    in_specs = [pl.BlockSpec((None, ATTN_WIDTH, tq), lambda b, i: (b, 0, i))]
    for k in ks:
        in_specs.append(pl.BlockSpec((None, k.shape[1], KV_WIDTH), lambda b, i: (b, 0, 0)))
    for vt in vts:
        in_specs.append(pl.BlockSpec((None, KV_HEADS * 128, vt.shape[2]), lambda b, i: (b, 0, 0)))
    return pl.pallas_call(
        functools.partial(_attn_kernel, n_seg=n_seg, key_chunk=key_chunk),
        grid=(bsz, t // tq),
        in_specs=in_specs,
        out_specs=pl.BlockSpec((None, tq, ATTN_WIDTH), lambda b, i: (b, i, 0)),
        out_shape=jax.ShapeDtypeStruct((bsz, t, ATTN_WIDTH), BF16),
        scratch_shapes=[
            pltpu.VMEM((s_tot, cols), F32),
            pltpu.VMEM((128, cols), F32),
        ],
        compiler_params=_cparams(2),
        name="attention_%dseg" % n_seg,
    )(qt, *ks, *vts)


def _mlstm_kernel(*refs, emit_ctx):
    (mq_c, mk_c, mv_c, mo_c, g_c, mq_l, mk_l, mv_l, mo_l, g_l, cw_q, cw_k, gain_ref) = refs[:13]
    if emit_ctx:
        out_c, out_l = refs[13:15]
        scratch = refs[15:]
    else:
        out_c = None
        out_l = refs[13]
        scratch = refs[14:]
    q_s, kt_s, cum_s, hf_s, hb_s, ct_s, m_s = scratch

    segs = [(mq_c, mk_c, mv_c, mo_c, g_c, out_c, emit_ctx),
            (mq_l, mk_l, mv_l, mo_l, g_l, out_l, True)]
    L = CHUNK
    ri = lax.broadcasted_iota(jnp.int32, (L, L), 0)
    ci = lax.broadcasted_iota(jnp.int32, (L, L), 1)
    ones_le = jnp.where(ri <= ci, 1.0, 0.0).astype(BF16)
    ones_ge = jnp.where(ri >= ci, 1.0, 0.0).astype(BF16)
    eye = ri == ci
    mask_f = ci <= ri
    mask_b = ci >= ri
    row_i = lax.broadcasted_iota(jnp.int32, (L, ML_DIM), 0)
    ones_col = jnp.where(lax.broadcasted_iota(jnp.int32, (L, 128), 1) == 0, 1.0, 0.0).astype(BF16)
    kscale = ML_DIM ** -0.5

    chunk_base = 0
    for (mq, mk, mv, mo, g, out, emit) in segs:
        t_seg = mq.shape[0]
        n = t_seg // L

        def conv_silu(src, cw, c, t_seg=t_seg, n=n):
            t0 = pl.multiple_of(c * L, L)
            xc = src[pl.ds(t0, L), :]
            prev = src[pl.ds(jnp.maximum(t0 - 1, 0), 1), :] * jnp.where(c > 0, 1.0, 0.0)
            nxt = src[pl.ds(jnp.minimum(t0 + L, t_seg - 1), 1), :] * jnp.where(c < n - 1, 1.0, 0.0)
            xm = jnp.where(row_i == 0, prev, pltpu.roll(xc, 1, 0))
            xp = jnp.where(row_i == L - 1, nxt, pltpu.roll(xc, L - 1, 0))
            y = xm * cw[0:1, :] + xc * cw[1:2, :] + xp * cw[2:3, :]
            return y * _sigmoid(y)

        def prep(c, carry, mq=mq, mk=mk, base=chunk_base, conv_silu=conv_silu):
            t0g = pl.multiple_of((base + c) * L, L)
            q_s[pl.ds(t0g, L), :] = conv_silu(mq, cw_q, c).astype(BF16)
            kt_s[base + c] = (conv_silu(mk, cw_k, c) * kscale).T.astype(BF16)
            return carry

        lax.fori_loop(0, n, prep, 0)
        g2 = g[...].reshape(n * GATE_ROWS, L)
        cum_s[0, chunk_base:chunk_base + n] = _split_dot(g2, ones_le).reshape(n, GATE_ROWS, L)
        cum_s[1, chunk_base:chunk_base + n] = _split_dot(g2, ones_ge).reshape(n, GATE_ROWS, L)
        chunk_base += n

    ct_s[...] = jnp.zeros(ct_s.shape, F32)
    m_s[...] = jnp.zeros(m_s.shape, F32)

    def direction(d, qc, ktc, vc, ig, brow, total, mask, emit):
        m_prev = m_s[d, 0:1, 0:1]
        ct = ct_s[d]
        v_aug = jnp.concatenate([vc, ones_col], axis=1)
        h = None
        if emit:
            bcol = jnp.sum(jnp.where(eye, brow, 0.0), axis=-1, keepdims=True)
            dmat = jnp.where(mask, bcol - brow + ig, -jnp.inf)
            m_t = jnp.maximum(bcol + m_prev, jnp.max(dmat, axis=-1, keepdims=True))
            s = (_dot(qc, ktc) * jnp.exp(dmat - m_t)).astype(BF16)
            w_inter = jnp.exp(bcol + m_prev - m_t)
            tot = w_inter * _dot(qc, ct.astype(BF16)) + _dot(s, v_aug)
            den = tot[:, ML_DIM:ML_DIM + 1]
            h = tot[:, 0:ML_DIM] / jnp.maximum(jnp.abs(den), jnp.exp(-m_t))
        g_row = total - brow + ig
        m_new = jnp.maximum(total + m_prev, jnp.max(g_row, axis=-1, keepdims=True))
        decay = jnp.exp(total + m_prev - m_new)
        w_row = jnp.exp(g_row - m_new)
        kw = (ktc.astype(F32) * w_row).astype(BF16)
        ct_s[d] = decay * ct + _dot(kw, v_aug)
        m_s[d] = jnp.broadcast_to(m_new, m_s.shape[1:])
        return h

    chunk_base = 0
    row_base = 0
    for (mq, mk, mv, mo, g, out, emit) in segs:
        t_seg = mq.shape[0]
        n = t_seg // L

        def step(i, carry, mv=mv, g=g, n=n, cb=chunk_base, rb=row_base, emit=emit):
            for d, c in ((0, i), (1, n - 1 - i)):
                t0 = pl.multiple_of(c * L, L)
                t0g = pl.multiple_of(rb + c * L, L)
                gr = g[c]
                cum = cum_s[d, cb + c]
                if d == 0:
                    ig, brow = gr[0:1], cum[1:2]
                    total = brow[:, L - 1:L]
                    mask = mask_f
                else:
                    ig, brow = gr[2:3], cum[3:4]
                    total = brow[:, 0:1]
                    mask = mask_b
                h = direction(d, q_s[pl.ds(t0g, L), :], kt_s[cb + c], mv[pl.ds(t0, L), :],
                              ig, brow, total, mask, emit)
                if emit:
                    (hf_s if d == 0 else hb_s)[pl.ds(t0g, L), :] = h
            return carry

        lax.fori_loop(0, n, step, 0)

        if emit:
            def merge(c, carry, mo=mo, out=out, rb=row_base):
                t0 = pl.multiple_of(c * L, L)
                t0g = pl.multiple_of(rb + c * L, L)
                hsum = hf_s[pl.ds(t0g, L), :] + hb_s[pl.ds(t0g, L), :]
                ms = jnp.mean(hsum * hsum, axis=-1, keepdims=True)
                hn = hsum * lax.rsqrt(ms + NORM_EPS) * gain_ref[...]
                out[pl.ds(t0, L), :] = (mo[pl.ds(t0, L), :].astype(F32) * hn).astype(BF16)
                return carry
            lax.fori_loop(0, n, merge, 0)
        chunk_base += n
        row_base += t_seg


def _mlstm_call(ctx_p, lat_p, conv_w, ml_gain, layer, *, emit_ctx):
    bsz = lat_p[0].shape[0]
    in_specs = []
    args = []
    t_tot = 0
    for (mqk, mv, mo, g) in (ctx_p, lat_p):
        t = mqk.shape[1]
        n = t // CHUNK
        t_tot += t
        in_specs += [
            pl.BlockSpec((None, t, ML_DIM), lambda b, h: (b, 0, h)),
            pl.BlockSpec((None, t, ML_DIM), lambda b, h: (b, 0, ML_HEADS + h)),
            pl.BlockSpec((None, t, ML_DIM), lambda b, h: (b, 0, h)),
            pl.BlockSpec((None, t, ML_DIM), lambda b, h: (b, 0, h)),
            pl.BlockSpec((None, n, GATE_ROWS, CHUNK), lambda b, h: (b, 0, h, 0)),
        ]
        args += [mqk, mqk, mv, mo, g]
    in_specs += [
        pl.BlockSpec((None, CONV_WIDTH, ML_DIM), lambda b, h: (layer, 0, h)),
        pl.BlockSpec((None, CONV_WIDTH, ML_DIM), lambda b, h: (layer, 0, ML_HEADS + h)),
        pl.BlockSpec((None, 1, ML_DIM), lambda b, h: (layer, 0, h)),
    ]
    args += [conv_w, conv_w, ml_gain]
    t_c, t_l = ctx_p[0].shape[1], lat_p[0].shape[1]
    out_l_shape = jax.ShapeDtypeStruct((bsz, t_l, ML_WIDTH), BF16)
    out_l_spec = pl.BlockSpec((None, t_l, ML_DIM), lambda b, h: (b, 0, h))
    if emit_ctx:
        out_shape = (jax.ShapeDtypeStruct((bsz, t_c, ML_WIDTH), BF16), out_l_shape)
        out_specs = (pl.BlockSpec((None, t_c, ML_DIM), lambda b, h: (b, 0, h)), out_l_spec)
    else:
        out_shape = out_l_shape
        out_specs = out_l_spec
    n_tot = t_tot // CHUNK
    res = pl.pallas_call(
        functools.partial(_mlstm_kernel, emit_ctx=emit_ctx),
        grid=(bsz, ML_HEADS),
        in_specs=in_specs,
        out_specs=out_specs,
        out_shape=out_shape,
        scratch_shapes=[
            pltpu.VMEM((t_tot, ML_DIM), BF16),
            pltpu.VMEM((n_tot, ML_DIM, CHUNK), BF16),
            pltpu.VMEM((2, n_tot, GATE_ROWS, CHUNK), F32),
            pltpu.VMEM((t_tot, ML_DIM), F32),
            pltpu.VMEM((t_tot, ML_DIM), F32),
            pltpu.VMEM((2, ML_DIM, 2 * ML_DIM), F32),
            pltpu.VMEM((2, 8, 128), F32),
        ],
        compiler_params=_cparams(2),
        name="mlstm_emit" if emit_ctx else "mlstm_last",
    )(*args)
    if emit_ctx:
        return res
    return None, res


def _outmlp_kernel(*refs, final, ffn_chunk):
    x_ref, att_ref, mem_ref, mod_ref, wo_ref, gain_ref, w1_ref, w2_ref = refs[:8]
    if final:
        gf_ref, o_ref = refs[8:]
    else:
        o_ref = refs[8]
    mod = mod_ref[...]
    y = _dot(att_ref[...], wo_ref[0:ATTN_WIDTH, :]) + _dot(mem_ref[...], wo_ref[ATTN_WIDTH:D_MODEL, :])
    x1 = x_ref[...] + mod[2:3] * y
    h = _rms_modulate(x1, gain_ref[...], mod[3:4], mod[4:5]).astype(BF16)
    acc = jnp.zeros(x1.shape, F32)
    for j in range(FFN_DIM // ffn_chunk):
        f = jnp.maximum(_dot(h, w1_ref[:, ffn_chunk * j:ffn_chunk * (j + 1)]), 0.0)
        acc = acc + _dot((f * f).astype(BF16), w2_ref[ffn_chunk * j:ffn_chunk * (j + 1), :])
    x2 = x1 + mod[5:6] * acc
    if final:
        ms = jnp.mean(x2 * x2, axis=-1, keepdims=True)
        x2 = x2 * lax.rsqrt(ms + NORM_EPS) * gf_ref[...]
    o_ref[...] = x2


def _outmlp_call(x, att, mem, mod, mod_row, wo, gain, w1, w2, layer, gf, *, tm, ffn_chunk=1024):
    bsz, t, d = x.shape
    final = gf is not None
    if mod_row is None:
        mod_map = lambda b, i: (b, 0, 0)
    else:
        mod_map = lambda b, i: (mod_row, 0, 0)
    in_specs = [
        pl.BlockSpec((None, tm, d), lambda b, i: (b, i, 0)),
        pl.BlockSpec((None, tm, ATTN_WIDTH), lambda b, i: (b, i, 0)),
        pl.BlockSpec((None, tm, ML_WIDTH), lambda b, i: (b, i, 0)),
        pl.BlockSpec((None, N_MOD, d), mod_map),
        _const_spec((None, d, d), lambda b, i: (layer, 0, 0)),
        _const_spec((None, 1, d), lambda b, i: (layer, 0, 0)),
        _const_spec((None, d, FFN_DIM), lambda b, i: (layer, 0, 0)),
        _const_spec((None, FFN_DIM, d), lambda b, i: (layer, 0, 0)),
    ]
    args = [x, att, mem, mod, wo, gain, w1, w2]
    if final:
        in_specs.append(_const_spec((1, d), lambda b, i: (0, 0)))
        args.append(gf)
    return pl.pallas_call(
        functools.partial(_outmlp_kernel, final=final, ffn_chunk=ffn_chunk),
        grid=(bsz, t // tm),
        in_specs=in_specs,
        out_specs=pl.BlockSpec((None, tm, d), lambda b, i: (b, i, 0)),
        out_shape=jax.ShapeDtypeStruct((bsz, t, d), F32),
        compiler_params=_cparams(2),
        name="outproj_mlp_final" if final else "outproj_mlp",
    )(*args)


def _projection_columns():
    a_q, a_k, a_v = 0, ATTN_WIDTH, ATTN_WIDTH + KV_WIDTH
    m_qk = ATTN_WIDTH + 2 * KV_WIDTH
    m_v = m_qk + 2 * ML_WIDTH
    m_o = m_v + ML_WIDTH
    gates = m_o + ML_WIDTH
    deint = np.concatenate([np.arange(0, HEAD_DIM, 2), np.arange(1, HEAD_DIM, 2)])
    idx = np.full((W_COLS,), -1, np.int64)
    idx[C_MQK:C_MQK + 2 * ML_WIDTH] = m_qk + np.arange(2 * ML_WIDTH)
    idx[C_Q:C_Q + ATTN_WIDTH] = a_q + (HEAD_DIM * np.arange(ATTN_HEADS)[:, None] + deint[None, :]).reshape(-1)
    idx[C_MV:C_MV + ML_WIDTH] = m_v + np.arange(ML_WIDTH)
    idx[C_MO:C_MO + ML_WIDTH] = m_o + np.arange(ML_WIDTH)
    idx[C_K:C_K + KV_WIDTH] = a_k + (HEAD_DIM * np.arange(KV_HEADS)[:, None] + deint[None, :]).reshape(-1)
    idx[C_V:C_V + KV_WIDTH] = a_v + np.arange(KV_WIDTH)
    for h in range(ML_HEADS):
        for kind in range(4):
            idx[C_G + GATE_ROWS * h + kind] = gates + kind * ML_HEADS + h
    return idx, deint


def _rope_tables(t):
    rows = t // GRID_W
    row_idx = jnp.repeat(jnp.arange(rows, dtype=F32), GRID_W)
    col_idx = jnp.tile(jnp.arange(GRID_W, dtype=F32), rows)
    inv_freq = jnp.power(ROPE_THETA, -jnp.arange(0, ROPE_AXIS_DIM, 2, dtype=F32) / ROPE_AXIS_DIM)
    ang = jnp.concatenate([row_idx[:, None] * inv_freq, col_idx[:, None] * inv_freq], axis=-1)
    cos, sin = jnp.cos(ang), jnp.sin(ang)
    cos128 = jnp.tile(cos, (1, 4))
    sin128 = jnp.tile(jnp.concatenate([-sin, sin], axis=-1), (1, 2))
    return cos128, sin128


def _pick_tile(t, pref):
    tm = min(pref, t)
    while t % tm:
        tm //= 2
    return tm


def kernel(x, c, ctx, c_ctx, w_ada, b_ada, norm_mix, norm_mlp, w_in, b_gates, conv_qk,
           q_norm, k_norm, mlstm_norm, w_out, w_mlp_in, w_mlp_out, norm_final):
    bsz, t, d = x.shape
    t_c = ctx.shape[1]
    depth = w_ada.shape[0]
    assert d == D_MODEL and t % CHUNK == 0 and t_c % CHUNK == 0 and t % GRID_W == 0

    rows = -(-(bsz + 1) // 8) * 8
    cvec = jnp.zeros((rows, d), F32).at[:bsz].set(c).at[bsz].set(c_ctx)
    mod = _ada_call(cvec, w_ada, b_ada).reshape(depth, rows, N_MOD, d)

    idx, deint = _projection_columns()
    w_all = jnp.where(idx[None, None, :] >= 0, jnp.take(w_in, np.maximum(idx, 0), axis=2), 0.0).astype(BF16)
    qg = jnp.tile(q_norm[:, deint], (1, 2)).reshape(depth, 1, 128)
    kg = jnp.tile(k_norm[:, deint], (1, 2)).reshape(depth, 1, 128)
    norm_mix = norm_mix.reshape(depth, 1, d)
    norm_mlp = norm_mlp.reshape(depth, 1, d)
    mlstm_norm = mlstm_norm.reshape(depth, 1, ML_WIDTH)
    bg = jnp.zeros((depth, ML_HEADS, GATE_ROWS), F32).at[:, :, 0:4].set(
        b_gates.reshape(depth, 4, ML_HEADS).transpose(0, 2, 1)).reshape(depth, ML_HEADS * GATE_ROWS, 1)
    wo = w_out.astype(BF16)
    w1 = w_mlp_in.astype(BF16)
    w2 = w_mlp_out.astype(BF16)
    cos128, sin128 = _rope_tables(t)
    gf = norm_final.reshape(1, d)

    tm_l = _pick_tile(t, 512)
    tm_c = _pick_tile(t_c, 512)
    tq_l = _pick_tile(t, 128)
    tq_c = _pick_tile(t_c, 128)

    for layer in range(depth):
        emit_ctx = layer < depth - 1
        mod_l = mod[layer]
        qt_l, k_l, vt_l, mqk_l, mv_l, mo_l, g_l = _inproj_call(
            x, mod_l, None, norm_mix, w_all, layer, qg, kg, bg, cos128, sin128, use_rope=True, tm=tm_l)
        qt_c, k_c, vt_c, mqk_c, mv_c, mo_c, g_c = _inproj_call(
            ctx, mod_l, bsz, norm_mix, w_all, layer, qg, kg, bg, cos128, sin128, use_rope=False, tm=tm_c)
        att_l = _attn_call(qt_l, [k_c, k_l], [vt_c, vt_l], tq=tq_l)
        mem_c, mem_l = _mlstm_call((mqk_c, mv_c, mo_c, g_c), (mqk_l, mv_l, mo_l, g_l),
                                   conv_qk, mlstm_norm, layer, emit_ctx=emit_ctx)
        x = _outmlp_call(x, att_l, mem_l, mod_l, None, wo, norm_mlp, w1, w2, layer,
                         None if emit_ctx else gf, tm=tm_l)
        if emit_ctx:
            att_c = _attn_call(qt_c, [k_c], [vt_c], tq=tq_c)
            ctx = _outmlp_call(ctx, att_c, mem_c, mod_l, bsz, wo, norm_mlp, w1, w2, layer, None, tm=tm_c)
    return x
```

```python
import functools

import numpy as np
import jax
import jax.numpy as jnp
from jax import lax
from jax.experimental import pallas as pl
from jax.experimental.pallas import tpu as pltpu

D_MODEL = 1024
N_MOD = 6
GRID_W = 64
ATTN_WIDTH = 512
ATTN_HEADS = 8
HEAD_DIM = 64
KV_HEADS = 2
ATTN_GROUP = ATTN_HEADS // KV_HEADS
KV_WIDTH = KV_HEADS * HEAD_DIM
ROPE_THETA = 10000.0
ROPE_AXIS_DIM = HEAD_DIM // 2
ML_WIDTH = 512
ML_HEADS = 4
ML_DIM = 128
CHUNK = 128
CONV_WIDTH = 3
N_GATES = 4 * ML_HEADS
GATE_ROWS = 8
FFN_DIM = 4 * D_MODEL
NORM_EPS = 1e-6
LOG2_E = 1.4426950408889634

C_MQK = 0
C_Q = C_MQK + 2 * ML_WIDTH
C_MV = C_Q + ATTN_WIDTH
C_MO = C_MV + ML_WIDTH
C_K = C_MO + ML_WIDTH
C_V = C_K + KV_WIDTH
C_G = C_V + KV_WIDTH
W_COLS = C_G + 128

VMEM_LIMIT_BYTES = 56 * 1024 * 1024

BF16 = jnp.bfloat16
F32 = jnp.float32


def _cparams(n_grid):
    return pltpu.CompilerParams(
        dimension_semantics=("arbitrary",) * n_grid,
        vmem_limit_bytes=VMEM_LIMIT_BYTES)


def _const_spec(shape, index_map):
    return pl.BlockSpec(shape, index_map, pipeline_mode=pl.Buffered(1))


def _aligned(i, m):
    return i if isinstance(i, int) else pl.multiple_of(i, m)


def _dot(a, b):
    return jnp.dot(a, b, preferred_element_type=F32)


def _sigmoid(x):
    return 1.0 / (1.0 + jnp.exp(-x))


def _log_sigmoid(x):
    return jnp.minimum(x, 0.0) - jnp.log(1.0 + jnp.exp(-jnp.abs(x)))


def _group_ones(n, group):
    r = lax.broadcasted_iota(jnp.int32, (n, n), 0) // group
    c = lax.broadcasted_iota(jnp.int32, (n, n), 1) // group
    return jnp.where(r == c, 1.0, 0.0).astype(BF16)


def _split_dot(a, ones_mat):
    hi = a.astype(BF16)
    lo = (a - hi.astype(F32)).astype(BF16)
    return _dot(hi, ones_mat) + _dot(lo, ones_mat)


def _ada_kernel(c_ref, w_ref, b_ref, o_ref):
    c = c_ref[...]
    sc = (c * _sigmoid(c)).astype(BF16)
    o_ref[0] = _dot(sc, w_ref[0].astype(BF16)) + b_ref[0]


def _ada_call(cvec, w_ada, b_ada):
    depth, d, n = w_ada.shape
    rows = cvec.shape[0]
    tn = 1536
    return pl.pallas_call(
        _ada_kernel,
        grid=(depth, n // tn),
        in_specs=[
            pl.BlockSpec((rows, d), lambda l, j: (0, 0)),
            pl.BlockSpec((1, d, tn), lambda l, j: (l, 0, j)),
            pl.BlockSpec((1, 1, tn), lambda l, j: (l, 0, j)),
        ],
        out_specs=pl.BlockSpec((1, rows, tn), lambda l, j: (l, 0, j)),
        out_shape=jax.ShapeDtypeStruct((depth, rows, n), F32),
        compiler_params=_cparams(2),
        name="ada_mod",
    )(cvec, w_ada, b_ada.reshape(depth, 1, n))


def _rms_modulate(x, gain, shift, scale):
    ms = jnp.mean(x * x, axis=-1, keepdims=True)
    y = x * lax.rsqrt(ms + NORM_EPS) * gain
    return y * (1.0 + scale) + shift


def _swap32(x):
    lane = lax.broadcasted_iota(jnp.int32, x.shape, 1)
    up = pltpu.roll(x, 96, 1)
    down = pltpu.roll(x, 32, 1)
    return jnp.where((lane % HEAD_DIM) < ROPE_AXIS_DIM, up, down)


def _head_norm_rope(t, gain, cos, sin, use_rope):
    ss = _dot((t * t).astype(BF16), _group_ones(128, HEAD_DIM))
    tn = t * lax.rsqrt(ss * (1.0 / HEAD_DIM) + NORM_EPS) * gain
    if use_rope:
        tn = tn * cos + _swap32(tn) * sin
    return tn


def _inproj_kernel(x_ref, mod_ref, gain_ref, w_ref, qg_ref, kg_ref, bg_ref, cos_ref, sin_ref,
                   qt_out, k_out, vt_out, mqk_out, mv_out, mo_out, g_out, *, use_rope):
    x = x_ref[...]
    mod = mod_ref[...]
    h = _rms_modulate(x, gain_ref[...], mod[0:1], mod[1:2]).astype(BF16)
    tm = x.shape[0]

    mqk_out[...] = _dot(h, w_ref[:, C_MQK:C_MQK + 2 * ML_WIDTH])
    mv_out[...] = _dot(h, w_ref[:, C_MV:C_MV + ML_WIDTH]).astype(BF16)
    mo_out[...] = _sigmoid(_dot(h, w_ref[:, C_MO:C_MO + ML_WIDTH])).astype(BF16)

    vt = _dot(h, w_ref[:, C_V:C_V + KV_WIDTH]).T
    pad_row = lax.broadcasted_iota(jnp.int32, (HEAD_DIM, tm), 0)
    pad = jnp.where(pad_row == 0, 1.0, 0.0)
    for kvh in range(KV_HEADS):
        vt_out[128 * kvh:128 * kvh + HEAD_DIM, :] = vt[HEAD_DIM * kvh:HEAD_DIM * (kvh + 1)].astype(BF16)
        vt_out[128 * kvh + HEAD_DIM:128 * (kvh + 1), :] = pad.astype(BF16)

    cos = cos_ref[...] if use_rope else None
    sin = sin_ref[...] if use_rope else None

    q = _dot(h, w_ref[:, C_Q:C_Q + ATTN_WIDTH])
    qscale = HEAD_DIM ** -0.5 * LOG2_E
    for j in range(ATTN_WIDTH // 128):
        blk = _head_norm_rope(q[:, 128 * j:128 * (j + 1)], qg_ref[...], cos, sin, use_rope)
        qt_out[128 * j:128 * (j + 1), :] = (blk * qscale).T.astype(BF16)

    k = _dot(h, w_ref[:, C_K:C_K + KV_WIDTH])
    k_out[...] = _head_norm_rope(k, kg_ref[...], cos, sin, use_rope).astype(BF16)

    g = _dot(h, w_ref[:, C_G:C_G + 128])
    gt = g.T[0:ML_HEADS * GATE_ROWS, :] + bg_ref[...]
    row = lax.broadcasted_iota(jnp.int32, gt.shape, 0) % GATE_ROWS
    gt = jnp.where((row == 1) | (row == 3), _log_sigmoid(gt), gt)
    for j in range(tm // CHUNK):
        g_out[j] = gt[:, CHUNK * j:CHUNK * (j + 1)]


def _inproj_call(x, mod, mod_row, gain, w_all, layer, qg, kg, bg, cos, sin, *, use_rope, tm):
    bsz, t, d = x.shape
    nt = t // tm
    nc = t // CHUNK
    cpt = tm // CHUNK
    if mod_row is None:
        mod_map = lambda b, i: (b, 0, 0)
    else:
        mod_map = lambda b, i: (mod_row, 0, 0)
    tok = lambda w: pl.BlockSpec((None, tm, w), lambda b, i: (b, i, 0))
    out_shapes = (
        jax.ShapeDtypeStruct((bsz, ATTN_WIDTH, t), BF16),
        jax.ShapeDtypeStruct((bsz, t, KV_WIDTH), BF16),
        jax.ShapeDtypeStruct((bsz, KV_HEADS * 128, t), BF16),
        jax.ShapeDtypeStruct((bsz, t, 2 * ML_WIDTH), F32),
        jax.ShapeDtypeStruct((bsz, t, ML_WIDTH), BF16),
        jax.ShapeDtypeStruct((bsz, t, ML_WIDTH), BF16),
        jax.ShapeDtypeStruct((bsz, nc, ML_HEADS * GATE_ROWS, CHUNK), F32),
    )
    out_specs = (
        pl.BlockSpec((None, ATTN_WIDTH, tm), lambda b, i: (b, 0, i)),
        tok(KV_WIDTH),
        pl.BlockSpec((None, KV_HEADS * 128, tm), lambda b, i: (b, 0, i)),
        tok(2 * ML_WIDTH),
        tok(ML_WIDTH),
        tok(ML_WIDTH),
        pl.BlockSpec((None, cpt, ML_HEADS * GATE_ROWS, CHUNK), lambda b, i: (b, i, 0, 0)),
    )
    in_specs = [
        tok(d),
        pl.BlockSpec((None, N_MOD, d), mod_map),
        _const_spec((None, 1, d), lambda b, i: (layer, 0, 0)),
        _const_spec((None, d, W_COLS), lambda b, i: (layer, 0, 0)),
        _const_spec((None, 1, 128), lambda b, i: (layer, 0, 0)),
        _const_spec((None, 1, 128), lambda b, i: (layer, 0, 0)),
        _const_spec((None, ML_HEADS * GATE_ROWS, 1), lambda b, i: (layer, 0, 0)),
        pl.BlockSpec((tm, 128), lambda b, i: (i, 0)),
        pl.BlockSpec((tm, 128), lambda b, i: (i, 0)),
    ]
    return pl.pallas_call(
        functools.partial(_inproj_kernel, use_rope=use_rope),
        grid=(bsz, nt),
        in_specs=in_specs,
        out_specs=out_specs,
        out_shape=out_shapes,
        compiler_params=_cparams(2),
        name="inproj_rope" if use_rope else "inproj_ctx",
    )(x, mod, gain, w_all, qg, kg, bg, cos, sin)


def _attn_kernel(*refs, n_seg, key_chunk):
    qt_ref = refs[0]
    k_refs = refs[1:1 + n_seg]
    vt_refs = refs[1 + n_seg:1 + 2 * n_seg]
    o_ref = refs[1 + 2 * n_seg]
    st_ref, acc_ref = refs[2 + 2 * n_seg:]
    tq = qt_ref.shape[1]
    cols = ATTN_GROUP * tq

    def for_chunks(fn, carry):
        base = 0
        for seg in range(n_seg):
            s_len = k_refs[seg].shape[0]
            ck = min(key_chunk, s_len)
            n = s_len // ck
            if n == 1:
                carry = fn(seg, 0, base, ck, carry)
            else:
                def body(c, carry, seg=seg, base=base, ck=ck):
                    return fn(seg, pl.multiple_of(c * ck, ck), base, ck, carry)
                carry = lax.fori_loop(0, n, body, carry)
            base += n * ck
        return carry

    zeros = jnp.zeros((HEAD_DIM, tq), BF16)
    for kvh in range(KV_HEADS):
        pieces = []
        for g in range(ATTN_GROUP):
            row0 = HEAD_DIM * (ATTN_GROUP * kvh + g)
            qg = qt_ref[row0:row0 + HEAD_DIM, :]
            pieces.append(jnp.concatenate([qg, zeros] if kvh == 0 else [zeros, qg], axis=0))
        qz = jnp.concatenate(pieces, axis=1)

        def scores(seg, off, base, ck, mx, qz=qz):
            st = _dot(k_refs[seg][pl.ds(off, ck), :], qz)
            st_ref[pl.ds(_aligned(base + off, 128), ck), :] = st
            return jnp.maximum(mx, jnp.max(st.reshape(ck // 8, 8, cols), axis=0))

        mx = for_chunks(scores, jnp.full((8, cols), -jnp.inf, F32))
        m = jnp.max(mx, axis=0, keepdims=True)
        acc_ref[...] = jnp.zeros((128, cols), F32)

        def weighted(seg, off, base, ck, carry, kvh=kvh, m=m):
            p = jnp.exp2(st_ref[pl.ds(_aligned(base + off, 128), ck), :] - m).astype(BF16)
            acc_ref[...] += _dot(vt_refs[seg][128 * kvh:128 * (kvh + 1), pl.ds(off, ck)], p)
            return carry

        for_chunks(weighted, 0)

        acc = acc_ref[...]
        ot = acc[0:HEAD_DIM] * (1.0 / acc[HEAD_DIM:HEAD_DIM + 1])
        for pair in range(ATTN_GROUP // 2):
            two = jnp.concatenate([ot[:, tq * (2 * pair):tq * (2 * pair + 1)],
                                   ot[:, tq * (2 * pair + 1):tq * (2 * pair + 2)]], axis=0)
            lane0 = HEAD_DIM * (ATTN_GROUP * kvh + 2 * pair)
            o_ref[:, lane0:lane0 + 2 * HEAD_DIM] = two.T.astype(BF16)


def _attn_call(qt, ks, vts, *, tq, key_chunk=2048):
    bsz, _, t = qt.shape
    n_seg = len(ks)
    s_tot = sum(k.shape[1] for k in ks)
    cols = ATTN_GROUP * tq
    in_specs = [pl.BlockSpec((None, ATTN_WIDTH, tq), lambda b, i: (b, 0, i))]
    for k in ks:
        in_specs.append(pl.BlockSpec((None, k.shape[1], KV_WIDTH), lambda b, i: (b, 0, 0)))
    for vt in vts:
        in_specs.append(pl.BlockSpec((None, KV_HEADS * 128, vt.shape[2]), lambda b, i: (b, 0, 0)))
    return pl.pallas_call(
        functools.partial(_attn_kernel, n_seg=n_seg, key_chunk=key_chunk),
        grid=(bsz, t // tq),
        in_specs=in_specs,
        out_specs=pl.BlockSpec((None, tq, ATTN_WIDTH), lambda b, i: (b, i, 0)),
        out_shape=jax.ShapeDtypeStruct((bsz, t, ATTN_WIDTH), BF16),
        scratch_shapes=[
            pltpu.VMEM((s_tot, cols), F32),
            pltpu.VMEM((128, cols), F32),
        ],
        compiler_params=_cparams(2),
        name="attention_%dseg" % n_seg,
    )(qt, *ks, *vts)


def _mlstm_kernel(*refs, emit_ctx):
    (mq_c, mk_c, mv_c, mo_c, g_c, mq_l, mk_l, mv_l, mo_l, g_l, cw_q, cw_k, gain_ref) = refs[:13]
    if emit_ctx:
        out_c, out_l = refs[13:15]
        scratch = refs[15:]
    else:
        out_c = None
        out_l = refs[13]
        scratch = refs[14:]
    q_s, kt_s, cum_s, hf_s, hb_s, ct_s, m_s = scratch

    segs = [(mq_c, mk_c, mv_c, mo_c, g_c, out_c, emit_ctx),
            (mq_l, mk_l, mv_l, mo_l, g_l, out_l, True)]
    L = CHUNK
    ri = lax.broadcasted_iota(jnp.int32, (L, L), 0)
    ci = lax.broadcasted_iota(jnp.int32, (L, L), 1)
    ones_le = jnp.where(ri <= ci, 1.0, 0.0).astype(BF16)
    ones_ge = jnp.where(ri >= ci, 1.0, 0.0).astype(BF16)
    mask_f = ci <= ri
    mask_b = ci >= ri
    row_i = lax.broadcasted_iota(jnp.int32, (L, ML_DIM), 0)
    ones_col = jnp.where(lax.broadcasted_iota(jnp.int32, (L, 128), 1) == 0, 1.0, 0.0).astype(BF16)
    kscale = ML_DIM ** -0.5

    chunk_base = 0
    for (mq, mk, mv, mo, g, out, emit) in segs:
        t_seg = mq.shape[0]
        n = t_seg // L

        def conv_silu(src, cw, c, t_seg=t_seg, n=n):
            t0 = pl.multiple_of(c * L, L)
            xc = src[pl.ds(t0, L), :]
            prev = src[pl.ds(jnp.maximum(t0 - 1, 0), 1), :] * jnp.where(c > 0, 1.0, 0.0)
            nxt = src[pl.ds(jnp.minimum(t0 + L, t_seg - 1), 1), :] * jnp.where(c < n - 1, 1.0, 0.0)
            xm = jnp.where(row_i == 0, prev, pltpu.roll(xc, 1, 0))
            xp = jnp.where(row_i == L - 1, nxt, pltpu.roll(xc, L - 1, 0))
            y = xm * cw[0:1, :] + xc * cw[1:2, :] + xp * cw[2:3, :]
            return y * _sigmoid(y)

        def prep(c, carry, mq=mq, mk=mk, base=chunk_base, conv_silu=conv_silu):
            t0g = pl.multiple_of((base + c) * L, L)
            q_s[pl.ds(t0g, L), :] = conv_silu(mq, cw_q, c).astype(BF16)
            kt_s[base + c] = (conv_silu(mk, cw_k, c) * kscale).T.astype(BF16)
            return carry

        lax.fori_loop(0, n, prep, 0, unroll=2 if n % 2 == 0 else 1)
        g2 = g[...].reshape(n * GATE_ROWS, L)
        cum_s[0, chunk_base:chunk_base + n] = _split_dot(g2, ones_le).reshape(n, GATE_ROWS, L)
        cum_s[1, chunk_base:chunk_base + n] = _split_dot(g2, ones_ge).reshape(n, GATE_ROWS, L)
        chunk_base += n

    ct_s[...] = jnp.zeros(ct_s.shape, F32)
    m_s[...] = jnp.zeros(m_s.shape, F32)

    def direction(d, qc, ktc, vc, ig, lf, brow, total, mask, emit):
        m_prev = m_s[d, 0:1, 0:1]
        ct = ct_s[d]
        v_aug = jnp.concatenate([vc, ones_col], axis=1)
        a = ig - brow
        m_last = jnp.maximum(m_prev, jnp.max(a, axis=-1, keepdims=True))
        h = None
        if emit:
            a_vis = jnp.where(mask, a, -jnp.inf)
            m_col = jnp.maximum(jnp.max(a_vis, axis=-1, keepdims=True), m_prev)
            bcol = jnp.sum(jnp.where(mask, lf, 0.0), axis=-1, keepdims=True)
            s = (_dot(qc, ktc) * jnp.exp(a_vis - m_col)).astype(BF16)
            qw = (qc.astype(F32) * jnp.exp(m_prev - m_col)).astype(BF16)
            tot = _dot(jnp.concatenate([s, qw], axis=1),
                       jnp.concatenate([v_aug, ct.astype(BF16)], axis=0))
            den = tot[:, ML_DIM:ML_DIM + 1]
            h = tot[:, 0:ML_DIM] / jnp.maximum(jnp.abs(den), jnp.exp(-bcol - m_col))
        kw = (ktc.astype(F32) * jnp.exp(a - m_last)).astype(BF16)
        ct_s[d] = jnp.exp(m_prev - m_last) * ct + _dot(kw, v_aug)
        m_s[d] = jnp.broadcast_to(total + m_last, m_s.shape[1:])
        return h

    chunk_base = 0
    row_base = 0
    for (mq, mk, mv, mo, g, out, emit) in segs:
        t_seg = mq.shape[0]
        n = t_seg // L

        def step(i, carry, mv=mv, g=g, n=n, cb=chunk_base, rb=row_base, emit=emit):
            for d, c in ((0, i), (1, n - 1 - i)):
                t0 = pl.multiple_of(c * L, L)
                t0g = pl.multiple_of(rb + c * L, L)
                gr = g[c]
                cum = cum_s[d, cb + c]
                if d == 0:
                    ig, lf, brow = gr[0:1], gr[1:2], cum[1:2]
                    total = brow[:, L - 1:L]
                    mask = mask_f
                else:
                    ig, lf, brow = gr[2:3], gr[3:4], cum[3:4]
                    total = brow[:, 0:1]
                    mask = mask_b
                h = direction(d, q_s[pl.ds(t0g, L), :], kt_s[cb + c], mv[pl.ds(t0, L), :],
                              ig, lf, brow, total, mask, emit)
                if emit:
                    (hf_s if d == 0 else hb_s)[pl.ds(t0g, L), :] = h
            return carry

        lax.fori_loop(0, n, step, 0, unroll=2 if n % 2 == 0 else 1)

        if emit:
            def merge(c, carry, mo=mo, out=out, rb=row_base):
                t0 = pl.multiple_of(c * L, L)
                t0g = pl.multiple_of(rb + c * L, L)
                hsum = hf_s[pl.ds(t0g, L), :] + hb_s[pl.ds(t0g, L), :]
                ms = jnp.mean(hsum * hsum, axis=-1, keepdims=True)
                hn = hsum * lax.rsqrt(ms + NORM_EPS) * gain_ref[...]
                out[pl.ds(t0, L), :] = (mo[pl.ds(t0, L), :].astype(F32) * hn).astype(BF16)
                return carry
            lax.fori_loop(0, n, merge, 0, unroll=2 if n % 2 == 0 else 1)
        chunk_base += n
        row_base += t_seg


def _mlstm_call(ctx_p, lat_p, conv_w, ml_gain, layer, *, emit_ctx):
    bsz = lat_p[0].shape[0]
    in_specs = []
    args = []
    t_tot = 0
    for (mqk, mv, mo, g) in (ctx_p, lat_p):
        t = mqk.shape[1]
        n = t // CHUNK
        t_tot += t
        in_specs += [
            pl.BlockSpec((None, t, ML_DIM), lambda b, h: (b, 0, h)),
            pl.BlockSpec((None, t, ML_DIM), lambda b, h: (b, 0, ML_HEADS + h)),
            pl.BlockSpec((None, t, ML_DIM), lambda b, h: (b, 0, h)),
            pl.BlockSpec((None, t, ML_DIM), lambda b, h: (b, 0, h)),
            pl.BlockSpec((None, n, GATE_ROWS, CHUNK), lambda b, h: (b, 0, h, 0)),
        ]
        args += [mqk, mqk, mv, mo, g]
    in_specs += [
        pl.BlockSpec((None, CONV_WIDTH, ML_DIM), lambda b, h: (layer, 0, h)),
        pl.BlockSpec((None, CONV_WIDTH, ML_DIM), lambda b, h: (layer, 0, ML_HEADS + h)),
        pl.BlockSpec((None, 1, ML_DIM), lambda b, h: (layer, 0, h)),
    ]
    args += [conv_w, conv_w, ml_gain]
    t_c, t_l = ctx_p[0].shape[1], lat_p[0].shape[1]
    out_l_shape = jax.ShapeDtypeStruct((bsz, t_l, ML_WIDTH), BF16)
    out_l_spec = pl.BlockSpec((None, t_l, ML_DIM), lambda b, h: (b, 0, h))
    if emit_ctx:
        out_shape = (jax.ShapeDtypeStruct((bsz, t_c, ML_WIDTH), BF16), out_l_shape)
        out_specs = (pl.BlockSpec((None, t_c, ML_DIM), lambda b, h: (b, 0, h)), out_l_spec)
    else:
        out_shape = out_l_shape
        out_specs = out_l_spec
    n_tot = t_tot // CHUNK
    res = pl.pallas_call(
        functools.partial(_mlstm_kernel, emit_ctx=emit_ctx),
        grid=(bsz, ML_HEADS),
        in_specs=in_specs,
        out_specs=out_specs,
        out_shape=out_shape,
        scratch_shapes=[
            pltpu.VMEM((t_tot, ML_DIM), BF16),
            pltpu.VMEM((n_tot, ML_DIM, CHUNK), BF16),
            pltpu.VMEM((2, n_tot, GATE_ROWS, CHUNK), F32),
            pltpu.VMEM((t_tot, ML_DIM), F32),
            pltpu.VMEM((t_tot, ML_DIM), F32),
            pltpu.VMEM((2, ML_DIM, 2 * ML_DIM), F32),
            pltpu.VMEM((2, 8, 128), F32),
        ],
        compiler_params=_cparams(2),
        name="mlstm_emit" if emit_ctx else "mlstm_last",
    )(*args)
    if emit_ctx:
        return res
    return None, res


def _outmlp_kernel(*refs, final, ffn_chunk):
    x_ref, att_ref, mem_ref, mod_ref, wo_ref, gain_ref, w1_ref, w2_ref = refs[:8]
    if final:
        gf_ref, o_ref = refs[8:]
    else:
        o_ref = refs[8]
    mod = mod_ref[...]
    y = _dot(att_ref[...], wo_ref[0:ATTN_WIDTH, :]) + _dot(mem_ref[...], wo_ref[ATTN_WIDTH:D_MODEL, :])
    x1 = x_ref[...] + mod[2:3] * y
    h = _rms_modulate(x1, gain_ref[...], mod[3:4], mod[4:5]).astype(BF16)
    acc = jnp.zeros(x1.shape, F32)
    for j in range(FFN_DIM // ffn_chunk):
        f = jnp.maximum(_dot(h, w1_ref[:, ffn_chunk * j:ffn_chunk * (j + 1)]), 0.0)
        acc = acc + _dot((f * f).astype(BF16), w2_ref[ffn_chunk * j:ffn_chunk * (j + 1), :])
    x2 = x1 + mod[5:6] * acc
    if final:
        ms = jnp.mean(x2 * x2, axis=-1, keepdims=True)
        x2 = x2 * lax.rsqrt(ms + NORM_EPS) * gf_ref[...]
    o_ref[...] = x2


def _outmlp_call(x, att, mem, mod, mod_row, wo, gain, w1, w2, layer, gf, *, tm, ffn_chunk=1024):
    bsz, t, d = x.shape
    final = gf is not None
    if mod_row is None:
        mod_map = lambda b, i: (b, 0, 0)
    else:
        mod_map = lambda b, i: (mod_row, 0, 0)
    in_specs = [
        pl.BlockSpec((None, tm, d), lambda b, i: (b, i, 0)),
        pl.BlockSpec((None, tm, ATTN_WIDTH), lambda b, i: (b, i, 0)),
        pl.BlockSpec((None, tm, ML_WIDTH), lambda b, i: (b, i, 0)),
        pl.BlockSpec((None, N_MOD, d), mod_map),
        _const_spec((None, d, d), lambda b, i: (layer, 0, 0)),
        _const_spec((None, 1, d), lambda b, i: (layer, 0, 0)),
        _const_spec((None, d, FFN_DIM), lambda b, i: (layer, 0, 0)),
        _const_spec((None, FFN_DIM, d), lambda b, i: (layer, 0, 0)),
    ]
    args = [x, att, mem, mod, wo, gain, w1, w2]
    if final:
        in_specs.append(_const_spec((1, d), lambda b, i: (0, 0)))
        args.append(gf)
    return pl.pallas_call(
        functools.partial(_outmlp_kernel, final=final, ffn_chunk=ffn_chunk),
        grid=(bsz, t // tm),
        in_specs=in_specs,
        out_specs=pl.BlockSpec((None, tm, d), lambda b, i: (b, i, 0)),
        out_shape=jax.ShapeDtypeStruct((bsz, t, d), F32),
        compiler_params=_cparams(2),
        name="outproj_mlp_final" if final else "outproj_mlp",
    )(*args)


def _projection_columns():
    a_q, a_k, a_v = 0, ATTN_WIDTH, ATTN_WIDTH + KV_WIDTH
    m_qk = ATTN_WIDTH + 2 * KV_WIDTH
    m_v = m_qk + 2 * ML_WIDTH
    m_o = m_v + ML_WIDTH
    gates = m_o + ML_WIDTH
    deint = np.concatenate([np.arange(0, HEAD_DIM, 2), np.arange(1, HEAD_DIM, 2)])
    idx = np.full((W_COLS,), -1, np.int64)
    idx[C_MQK:C_MQK + 2 * ML_WIDTH] = m_qk + np.arange(2 * ML_WIDTH)
    idx[C_Q:C_Q + ATTN_WIDTH] = a_q + (HEAD_DIM * np.arange(ATTN_HEADS)[:, None] + deint[None, :]).reshape(-1)
    idx[C_MV:C_MV + ML_WIDTH] = m_v + np.arange(ML_WIDTH)
    idx[C_MO:C_MO + ML_WIDTH] = m_o + np.arange(ML_WIDTH)
    idx[C_K:C_K + KV_WIDTH] = a_k + (HEAD_DIM * np.arange(KV_HEADS)[:, None] + deint[None, :]).reshape(-1)
    idx[C_V:C_V + KV_WIDTH] = a_v + np.arange(KV_WIDTH)
    for h in range(ML_HEADS):
        for kind in range(4):
            idx[C_G + GATE_ROWS * h + kind] = gates + kind * ML_HEADS + h
    return idx, deint


def _rope_tables(t):
    rows = t // GRID_W
    row_idx = jnp.repeat(jnp.arange(rows, dtype=F32), GRID_W)
    col_idx = jnp.tile(jnp.arange(GRID_W, dtype=F32), rows)
    inv_freq = jnp.power(ROPE_THETA, -jnp.arange(0, ROPE_AXIS_DIM, 2, dtype=F32) / ROPE_AXIS_DIM)
    ang = jnp.concatenate([row_idx[:, None] * inv_freq, col_idx[:, None] * inv_freq], axis=-1)
    cos, sin = jnp.cos(ang), jnp.sin(ang)
    cos128 = jnp.tile(cos, (1, 4))
    sin128 = jnp.tile(jnp.concatenate([-sin, sin], axis=-1), (1, 2))
    return cos128, sin128


def _pick_tile(t, pref):
    tm = min(pref, t)
    while t % tm:
        tm //= 2
    return tm


def kernel(x, c, ctx, c_ctx, w_ada, b_ada, norm_mix, norm_mlp, w_in, b_gates, conv_qk,
           q_norm, k_norm, mlstm_norm, w_out, w_mlp_in, w_mlp_out, norm_final):
    bsz, t, d = x.shape
    t_c = ctx.shape[1]
    depth = w_ada.shape[0]
    assert d == D_MODEL and t % CHUNK == 0 and t_c % CHUNK == 0 and t % GRID_W == 0

    rows = -(-(bsz + 1) // 8) * 8
    cvec = jnp.zeros((rows, d), F32).at[:bsz].set(c).at[bsz].set(c_ctx)
    mod = _ada_call(cvec, w_ada, b_ada).reshape(depth, rows, N_MOD, d)

    idx, deint = _projection_columns()
    w_all = jnp.where(idx[None, None, :] >= 0, jnp.take(w_in, np.maximum(idx, 0), axis=2), 0.0).astype(BF16)
    qg = jnp.tile(q_norm[:, deint], (1, 2)).reshape(depth, 1, 128)
    kg = jnp.tile(k_norm[:, deint], (1, 2)).reshape(depth, 1, 128)
    norm_mix = norm_mix.reshape(depth, 1, d)
    norm_mlp = norm_mlp.reshape(depth, 1, d)
    mlstm_norm = mlstm_norm.reshape(depth, 1, ML_WIDTH)
    bg = jnp.zeros((depth, ML_HEADS, GATE_ROWS), F32).at[:, :, 0:4].set(
        b_gates.reshape(depth, 4, ML_HEADS).transpose(0, 2, 1)).reshape(depth, ML_HEADS * GATE_ROWS, 1)
    wo = w_out.astype(BF16)
    w1 = w_mlp_in.astype(BF16)
    w2 = w_mlp_out.astype(BF16)
    cos128, sin128 = _rope_tables(t)
    gf = norm_final.reshape(1, d)

    tm_l = _pick_tile(t, 512)
    tm_c = _pick_tile(t_c, 512)
    tq_l = _pick_tile(t, 128)
    tq_c = _pick_tile(t_c, 128)

    for layer in range(depth):
        emit_ctx = layer < depth - 1
        mod_l = mod[layer]
        qt_l, k_l, vt_l, mqk_l, mv_l, mo_l, g_l = _inproj_call(
            x, mod_l, None, norm_mix, w_all, layer, qg, kg, bg, cos128, sin128, use_rope=True, tm=tm_l)
        qt_c, k_c, vt_c, mqk_c, mv_c, mo_c, g_c = _inproj_call(
            ctx, mod_l, bsz, norm_mix, w_all, layer, qg, kg, bg, cos128, sin128, use_rope=False, tm=tm_c)
        att_l = _attn_call(qt_l, [k_c, k_l], [vt_c, vt_l], tq=tq_l)
        mem_c, mem_l = _mlstm_call((mqk_c, mv_c, mo_c, g_c), (mqk_l, mv_l, mo_l, g_l),
                                   conv_qk, mlstm_norm, layer, emit_ctx=emit_ctx)
        x = _outmlp_call(x, att_l, mem_l, mod_l, None, wo, norm_mlp, w1, w2, layer,
                         None if emit_ctx else gf, tm=tm_l)
        if emit_ctx:
            att_c = _attn_call(qt_c, [k_c], [vt_c], tq=tq_c)
            ctx = _outmlp_call(ctx, att_c, mem_c, mod_l, bsz, wo, norm_mlp, w1, w2, layer, None, tm=tm_c)
    return x
```

```python
import functools

import numpy as np
import jax
import jax.numpy as jnp
from jax import lax
from jax.experimental import pallas as pl
from jax.experimental.pallas import tpu as pltpu

D_MODEL = 1024
N_MOD = 6
GRID_W = 64
ATTN_WIDTH = 512
ATTN_HEADS = 8
HEAD_DIM = 64
KV_HEADS = 2
ATTN_GROUP = ATTN_HEADS // KV_HEADS
KV_WIDTH = KV_HEADS * HEAD_DIM
ROPE_THETA = 10000.0
ROPE_AXIS_DIM = HEAD_DIM // 2
ML_WIDTH = 512
ML_HEADS = 4
ML_DIM = 128
CHUNK = 128
CONV_WIDTH = 3
N_GATES = 4 * ML_HEADS
GATE_ROWS = 8
FFN_DIM = 4 * D_MODEL
NORM_EPS = 1e-6
LOG2_E = 1.4426950408889634

C_MQK = 0
C_Q = C_MQK + 2 * ML_WIDTH
C_MV = C_Q + ATTN_WIDTH
C_MO = C_MV + ML_WIDTH
C_K = C_MO + ML_WIDTH
C_V = C_K + KV_WIDTH
C_G = C_V + KV_WIDTH
W_COLS = C_G + 128

VMEM_LIMIT_BYTES = 56 * 1024 * 1024

BF16 = jnp.bfloat16
F32 = jnp.float32


def _cparams(n_grid):
    return pltpu.CompilerParams(
        dimension_semantics=("arbitrary",) * n_grid,
        vmem_limit_bytes=VMEM_LIMIT_BYTES)


def _const_spec(shape, index_map):
    return pl.BlockSpec(shape, index_map, pipeline_mode=pl.Buffered(1))


def _aligned(i, m):
    return i if isinstance(i, int) else pl.multiple_of(i, m)


def _dot(a, b):
    return jnp.dot(a, b, preferred_element_type=F32)


def _sigmoid(x):
    return 1.0 / (1.0 + jnp.exp(-x))


def _log_sigmoid(x):
    return jnp.minimum(x, 0.0) - jnp.log(1.0 + jnp.exp(-jnp.abs(x)))


def _group_ones(n, group):
    r = lax.broadcasted_iota(jnp.int32, (n, n), 0) // group
    c = lax.broadcasted_iota(jnp.int32, (n, n), 1) // group
    return jnp.where(r == c, 1.0, 0.0).astype(BF16)


def _split_dot(a, ones_mat):
    hi = a.astype(BF16)
    lo = (a - hi.astype(F32)).astype(BF16)
    return _dot(hi, ones_mat) + _dot(lo, ones_mat)


def _ada_kernel(c_ref, w_ref, b_ref, o_ref):
    c = c_ref[...]
    sc = (c * _sigmoid(c)).astype(BF16)
    o_ref[0] = _dot(sc, w_ref[0].astype(BF16)) + b_ref[0]


def _ada_call(cvec, w_ada, b_ada):
    depth, d, n = w_ada.shape
    rows = cvec.shape[0]
    tn = 1536
    return pl.pallas_call(
        _ada_kernel,
        grid=(depth, n // tn),
        in_specs=[
            pl.BlockSpec((rows, d), lambda l, j: (0, 0)),
            pl.BlockSpec((1, d, tn), lambda l, j: (l, 0, j)),
            pl.BlockSpec((1, 1, tn), lambda l, j: (l, 0, j)),
        ],
        out_specs=pl.BlockSpec((1, rows, tn), lambda l, j: (l, 0, j)),
        out_shape=jax.ShapeDtypeStruct((depth, rows, n), F32),
        compiler_params=_cparams(2),
        name="ada_mod",
    )(cvec, w_ada, b_ada.reshape(depth, 1, n))


def _rms_modulate(x, gain, shift, scale):
    ms = jnp.mean(x * x, axis=-1, keepdims=True)
    y = x * lax.rsqrt(ms + NORM_EPS) * gain
    return y * (1.0 + scale) + shift


def _swap32(x):
    lane = lax.broadcasted_iota(jnp.int32, x.shape, 1)
    up = pltpu.roll(x, 96, 1)
    down = pltpu.roll(x, 32, 1)
    return jnp.where((lane % HEAD_DIM) < ROPE_AXIS_DIM, up, down)


def _head_norm_rope(t, gain, cos, sin, use_rope):
    ss = _dot((t * t).astype(BF16), _group_ones(128, HEAD_DIM))
    tn = t * lax.rsqrt(ss * (1.0 / HEAD_DIM) + NORM_EPS) * gain
    if use_rope:
        tn = tn * cos + _swap32(tn) * sin
    return tn


def _inproj_kernel(x_ref, mod_ref, gain_ref, w_ref, qg_ref, kg_ref, bg_ref, cos_ref, sin_ref,
                   qt_out, k_out, vt_out, mqk_out, mv_out, mo_out, g_out, *, use_rope):
    x = x_ref[...]
    mod = mod_ref[...]
    h = _rms_modulate(x, gain_ref[...], mod[0:1], mod[1:2]).astype(BF16)
    tm = x.shape[0]

    mqk_out[...] = _dot(h, w_ref[:, C_MQK:C_MQK + 2 * ML_WIDTH])
    mv_out[...] = _dot(h, w_ref[:, C_MV:C_MV + ML_WIDTH]).astype(BF16)
    mo_out[...] = _sigmoid(_dot(h, w_ref[:, C_MO:C_MO + ML_WIDTH])).astype(BF16)

    vt = _dot(h, w_ref[:, C_V:C_V + KV_WIDTH]).T
    pad_row = lax.broadcasted_iota(jnp.int32, (HEAD_DIM, tm), 0)
    pad = jnp.where(pad_row == 0, 1.0, 0.0)
    for kvh in range(KV_HEADS):
        vt_out[128 * kvh:128 * kvh + HEAD_DIM, :] = vt[HEAD_DIM * kvh:HEAD_DIM * (kvh + 1)].astype(BF16)
        vt_out[128 * kvh + HEAD_DIM:128 * (kvh + 1), :] = pad.astype(BF16)

    cos = cos_ref[...] if use_rope else None
    sin = sin_ref[...] if use_rope else None

    q = _dot(h, w_ref[:, C_Q:C_Q + ATTN_WIDTH])
    qscale = HEAD_DIM ** -0.5 * LOG2_E
    for j in range(ATTN_WIDTH // 128):
        blk = _head_norm_rope(q[:, 128 * j:128 * (j + 1)], qg_ref[...], cos, sin, use_rope)
        qt_out[128 * j:128 * (j + 1), :] = (blk * qscale).T.astype(BF16)

    k = _dot(h, w_ref[:, C_K:C_K + KV_WIDTH])
    k_out[...] = _head_norm_rope(k, kg_ref[...], cos, sin, use_rope).astype(BF16)

    g = _dot(h, w_ref[:, C_G:C_G + 128])
    gt = g.T[0:ML_HEADS * GATE_ROWS, :] + bg_ref[...]
    row = lax.broadcasted_iota(jnp.int32, gt.shape, 0) % GATE_ROWS
    gt = jnp.where((row == 1) | (row == 3), _log_sigmoid(gt), gt)
    for j in range(tm // CHUNK):
        g_out[j] = gt[:, CHUNK * j:CHUNK * (j + 1)]


def _inproj_call(x, mod, mod_row, gain, w_all, layer, qg, kg, bg, cos, sin, *, use_rope, tm):
    bsz, t, d = x.shape
    nt = t // tm
    nc = t // CHUNK
    cpt = tm // CHUNK
    if mod_row is None:
        mod_map = lambda b, i: (b, 0, 0)
    else:
        mod_map = lambda b, i: (mod_row, 0, 0)
    tok = lambda w: pl.BlockSpec((None, tm, w), lambda b, i: (b, i, 0))
    out_shapes = (
        jax.ShapeDtypeStruct((bsz, ATTN_WIDTH, t), BF16),
        jax.ShapeDtypeStruct((bsz, t, KV_WIDTH), BF16),
        jax.ShapeDtypeStruct((bsz, KV_HEADS * 128, t), BF16),
        jax.ShapeDtypeStruct((bsz, t, 2 * ML_WIDTH), F32),
        jax.ShapeDtypeStruct((bsz, t, ML_WIDTH), BF16),
        jax.ShapeDtypeStruct((bsz, t, ML_WIDTH), BF16),
        jax.ShapeDtypeStruct((bsz, nc, ML_HEADS * GATE_ROWS, CHUNK), F32),
    )
    out_specs = (
        pl.BlockSpec((None, ATTN_WIDTH, tm), lambda b, i: (b, 0, i)),
        tok(KV_WIDTH),
        pl.BlockSpec((None, KV_HEADS * 128, tm), lambda b, i: (b, 0, i)),
        tok(2 * ML_WIDTH),
        tok(ML_WIDTH),
        tok(ML_WIDTH),
        pl.BlockSpec((None, cpt, ML_HEADS * GATE_ROWS, CHUNK), lambda b, i: (b, i, 0, 0)),
    )
    in_specs = [
        tok(d),
        pl.BlockSpec((None, N_MOD, d), mod_map),
        _const_spec((None, 1, d), lambda b, i: (layer, 0, 0)),
        _const_spec((None, d, W_COLS), lambda b, i: (layer, 0, 0)),
        _const_spec((None, 1, 128), lambda b, i: (layer, 0, 0)),
        _const_spec((None, 1, 128), lambda b, i: (layer, 0, 0)),
        _const_spec((None, ML_HEADS * GATE_ROWS, 1), lambda b, i: (layer, 0, 0)),
        pl.BlockSpec((tm, 128), lambda b, i: (i, 0)),
        pl.BlockSpec((tm, 128), lambda b, i: (i, 0)),
    ]
    return pl.pallas_call(
        functools.partial(_inproj_kernel, use_rope=use_rope),
        grid=(bsz, nt),
        in_specs=in_specs,
        out_specs=out_specs,
        out_shape=out_shapes,
        compiler_params=_cparams(2),
        name="inproj_rope" if use_rope else "inproj_ctx",
    )(x, mod, gain, w_all, qg, kg, bg, cos, sin)


def _attn_kernel(*refs, n_seg, n_chunks, q_block):
    qt_ref = refs[0]
    k_refs = refs[1:1 + n_seg]
    vt_refs = refs[1 + n_seg:1 + 2 * n_seg]
    o_ref = refs[1 + 2 * n_seg]
    k_all, vt_all, st_ref, acc_ref = refs[2 + 2 * n_seg:]
    tq = qt_ref.shape[1]
    cols = ATTN_GROUP * q_block
    s_tot = k_all.shape[0]
    ck = s_tot // n_chunks

    @pl.when(pl.program_id(1) == 0)
    def _():
        base = 0
        for seg in range(n_seg):
            s_len = k_refs[seg].shape[0]
            k_all[base:base + s_len, :] = k_refs[seg][...]
            vt_all[:, base:base + s_len] = vt_refs[seg][...]
            base += s_len

    groups = [(qb, kvh) for qb in range(tq // q_block) for kvh in range(KV_HEADS)]
    zeros = jnp.zeros((HEAD_DIM, q_block), BF16)

    def padded_queries(qb, kvh):
        pieces = []
        for g in range(ATTN_GROUP):
            row0 = HEAD_DIM * (ATTN_GROUP * kvh + g)
            qg = qt_ref[row0:row0 + HEAD_DIM, q_block * qb:q_block * (qb + 1)]
            pieces.append(jnp.concatenate([qg, zeros] if kvh == 0 else [zeros, qg], axis=0))
        return jnp.concatenate(pieces, axis=1)

    def finalize(qb, kvh):
        acc = acc_ref[...]
        ot = acc[0:HEAD_DIM] * (1.0 / acc[HEAD_DIM:HEAD_DIM + 1])
        for pair in range(ATTN_GROUP // 2):
            two = jnp.concatenate([ot[:, q_block * (2 * pair):q_block * (2 * pair + 1)],
                                   ot[:, q_block * (2 * pair + 1):q_block * (2 * pair + 2)]], axis=0)
            lane0 = HEAD_DIM * (ATTN_GROUP * kvh + 2 * pair)
            o_ref[q_block * qb:q_block * (qb + 1), lane0:lane0 + 2 * HEAD_DIM] = two.T.astype(BF16)

    m_prev = None
    for j in range(len(groups) + 1):
        qz = padded_queries(*groups[j]) if j < len(groups) else None
        if j >= 1:
            acc_ref[...] = jnp.zeros((128, cols), F32)

        def body(c, mx, j=j, qz=qz, m_prev=m_prev):
            off = _aligned(c * ck, 128)
            if qz is not None:
                st = _dot(k_all[pl.ds(off, ck), :], qz)
                st_ref[j % 2, pl.ds(off, ck), :] = st
                mx = jnp.maximum(mx, jnp.max(st.reshape(ck // 8, 8, cols), axis=0))
            if j >= 1:
                kvh = groups[j - 1][1]
                p = jnp.exp2(st_ref[(j - 1) % 2, pl.ds(off, ck), :] - m_prev).astype(BF16)
                acc_ref[...] += _dot(vt_all[128 * kvh:128 * (kvh + 1), pl.ds(off, ck)], p)
            return mx

        mx0 = jnp.full((8, cols), -jnp.inf, F32)
        if n_chunks == 1:
            mx = body(0, mx0)
        else:
            mx = lax.fori_loop(0, n_chunks, body, mx0)
        if j >= 1:
            finalize(*groups[j - 1])
        m_prev = jnp.max(mx, axis=0, keepdims=True)


def _attn_call(qt, ks, vts, *, tq, q_block=128):
    bsz, _, t = qt.shape
    n_seg = len(ks)
    s_tot = sum(k.shape[1] for k in ks)
    n_chunks = 2 if s_tot % 256 == 0 and s_tot > 1024 else 1
    q_block = min(q_block, tq)
    cols = ATTN_GROUP * q_block
    in_specs = [pl.BlockSpec((None, ATTN_WIDTH, tq), lambda b, i: (b, 0, i))]
    for k in ks:
        in_specs.append(pl.BlockSpec((None, k.shape[1], KV_WIDTH), lambda b, i: (b, 0, 0)))
    for vt in vts:
        in_specs.append(pl.BlockSpec((None, KV_HEADS * 128, vt.shape[2]), lambda b, i: (b, 0, 0)))
    return pl.pallas_call(
        functools.partial(_attn_kernel, n_seg=n_seg, n_chunks=n_chunks, q_block=q_block),
        grid=(bsz, t // tq),
        in_specs=in_specs,
        out_specs=pl.BlockSpec((None, tq, ATTN_WIDTH), lambda b, i: (b, i, 0)),
        out_shape=jax.ShapeDtypeStruct((bsz, t, ATTN_WIDTH), BF16),
        scratch_shapes=[
            pltpu.VMEM((s_tot, KV_WIDTH), BF16),
            pltpu.VMEM((KV_HEADS * 128, s_tot), BF16),
            pltpu.VMEM((2, s_tot, cols), F32),
            pltpu.VMEM((128, cols), F32),
        ],
        compiler_params=_cparams(2),
        name="attention_%dseg" % n_seg,
    )(qt, *ks, *vts)


def _mlstm_kernel(*refs, emit_ctx):
    (mq_c, mk_c, mv_c, mo_c, g_c, mq_l, mk_l, mv_l, mo_l, g_l, cw_q, cw_k, gain_ref) = refs[:13]
    if emit_ctx:
        out_c, out_l = refs[13:15]
        scratch = refs[15:]
    else:
        out_c = None
        out_l = refs[13]
        scratch = refs[14:]
    q_s, kt_s, cum_s, hf_s, hb_s, ct_s, m_s = scratch

    segs = [(mq_c, mk_c, mv_c, mo_c, g_c, out_c, emit_ctx),
            (mq_l, mk_l, mv_l, mo_l, g_l, out_l, True)]
    L = CHUNK
    ri = lax.broadcasted_iota(jnp.int32, (L, L), 0)
    ci = lax.broadcasted_iota(jnp.int32, (L, L), 1)
    ones_le = jnp.where(ri <= ci, 1.0, 0.0).astype(BF16)
    ones_ge = jnp.where(ri >= ci, 1.0, 0.0).astype(BF16)
    mask_f = ci <= ri
    mask_b = ci >= ri
    row_i = lax.broadcasted_iota(jnp.int32, (L, ML_DIM), 0)
    ones_col = jnp.where(lax.broadcasted_iota(jnp.int32, (L, 128), 1) == 0, 1.0, 0.0).astype(BF16)
    kscale = ML_DIM ** -0.5

    chunk_base = 0
    for (mq, mk, mv, mo, g, out, emit) in segs:
        t_seg = mq.shape[0]
        n = t_seg // L

        def conv_silu(src, cw, c, t_seg=t_seg, n=n):
            t0 = pl.multiple_of(c * L, L)
            xc = src[pl.ds(t0, L), :]
            prev = src[pl.ds(jnp.maximum(t0 - 1, 0), 1), :] * jnp.where(c > 0, 1.0, 0.0)
            nxt = src[pl.ds(jnp.minimum(t0 + L, t_seg - 1), 1), :] * jnp.where(c < n - 1, 1.0, 0.0)
            xm = jnp.where(row_i == 0, prev, pltpu.roll(xc, 1, 0))
            xp = jnp.where(row_i == L - 1, nxt, pltpu.roll(xc, L - 1, 0))
            y = xm * cw[0:1, :] + xc * cw[1:2, :] + xp * cw[2:3, :]
            return y * _sigmoid(y)

        def prep(c, carry, mq=mq, mk=mk, base=chunk_base, conv_silu=conv_silu):
            t0g = pl.multiple_of((base + c) * L, L)
            q_s[pl.ds(t0g, L), :] = conv_silu(mq, cw_q, c).astype(BF16)
            kt_s[base + c] = (conv_silu(mk, cw_k, c) * kscale).T.astype(BF16)
            return carry

        lax.fori_loop(0, n, prep, 0, unroll=2 if n % 2 == 0 else 1)
        g2 = g[...].reshape(n * GATE_ROWS, L)
        cum_s[0, chunk_base:chunk_base + n] = _split_dot(g2, ones_le).reshape(n, GATE_ROWS, L)
        cum_s[1, chunk_base:chunk_base + n] = _split_dot(g2, ones_ge).reshape(n, GATE_ROWS, L)
        chunk_base += n

    ct_s[...] = jnp.zeros(ct_s.shape, F32)
    m_s[...] = jnp.zeros(m_s.shape, F32)

    def direction(d, qc, ktc, vc, ig, lf, brow, total, mask, emit):
        m_prev = m_s[d, 0:1, 0:1]
        ct = ct_s[d]
        v_aug = jnp.concatenate([vc, ones_col], axis=1)
        a = ig - brow
        m_last = jnp.maximum(m_prev, jnp.max(a, axis=-1, keepdims=True))
        h = None
        if emit:
            a_vis = jnp.where(mask, a, -jnp.inf)
            m_col = jnp.maximum(jnp.max(a_vis, axis=-1, keepdims=True), m_prev)
            bcol = jnp.sum(jnp.where(mask, lf, 0.0), axis=-1, keepdims=True)
            s = (_dot(qc, ktc) * jnp.exp(a_vis - m_col)).astype(BF16)
            qw = (qc.astype(F32) * jnp.exp(m_prev - m_col)).astype(BF16)
            tot = _dot(jnp.concatenate([s, qw], axis=1),
                       jnp.concatenate([v_aug, ct.astype(BF16)], axis=0))
            den = tot[:, ML_DIM:ML_DIM + 1]
            h = tot[:, 0:ML_DIM] / jnp.maximum(jnp.abs(den), jnp.exp(-bcol - m_col))
        kw = (ktc.astype(F32) * jnp.exp(a - m_last)).astype(BF16)
        ct_s[d] = jnp.exp(m_prev - m_last) * ct + _dot(kw, v_aug)
        m_s[d] = jnp.broadcast_to(total + m_last, m_s.shape[1:])
        return h

    chunk_base = 0
    row_base = 0
    for (mq, mk, mv, mo, g, out, emit) in segs:
        t_seg = mq.shape[0]
        n = t_seg // L

        def step(i, carry, mv=mv, g=g, n=n, cb=chunk_base, rb=row_base, emit=emit):
            for d, c in ((0, i), (1, n - 1 - i)):
                t0 = pl.multiple_of(c * L, L)
                t0g = pl.multiple_of(rb + c * L, L)
                gr = g[c]
                cum = cum_s[d, cb + c]
                if d == 0:
                    ig, lf, brow = gr[0:1], gr[1:2], cum[1:2]
                    total = brow[:, L - 1:L]
                    mask = mask_f
                else:
                    ig, lf, brow = gr[2:3], gr[3:4], cum[3:4]
                    total = brow[:, 0:1]
                    mask = mask_b
                h = direction(d, q_s[pl.ds(t0g, L), :], kt_s[cb + c], mv[pl.ds(t0, L), :],
                              ig, lf, brow, total, mask, emit)
                if emit:
                    (hf_s if d == 0 else hb_s)[pl.ds(t0g, L), :] = h
            return carry

        lax.fori_loop(0, n, step, 0, unroll=2 if n % 2 == 0 else 1)

        if emit:
            def merge(c, carry, mo=mo, out=out, rb=row_base):
                t0 = pl.multiple_of(c * L, L)
                t0g = pl.multiple_of(rb + c * L, L)
                hsum = hf_s[pl.ds(t0g, L), :] + hb_s[pl.ds(t0g, L), :]
                ms = jnp.mean(hsum * hsum, axis=-1, keepdims=True)
                hn = hsum * lax.rsqrt(ms + NORM_EPS) * gain_ref[...]
                out[pl.ds(t0, L), :] = (mo[pl.ds(t0, L), :].astype(F32) * hn).astype(BF16)
                return carry
            lax.fori_loop(0, n, merge, 0, unroll=2 if n % 2 == 0 else 1)
        chunk_base += n
        row_base += t_seg


def _mlstm_call(ctx_p, lat_p, conv_w, ml_gain, layer, *, emit_ctx):
    bsz = lat_p[0].shape[0]
    in_specs = []
    args = []
    t_tot = 0
    for (mqk, mv, mo, g) in (ctx_p, lat_p):
        t = mqk.shape[1]
        n = t // CHUNK
        t_tot += t
        in_specs += [
            pl.BlockSpec((None, t, ML_DIM), lambda b, h: (b, 0, h)),
            pl.BlockSpec((None, t, ML_DIM), lambda b, h: (b, 0, ML_HEADS + h)),
            pl.BlockSpec((None, t, ML_DIM), lambda b, h: (b, 0, h)),
            pl.BlockSpec((None, t, ML_DIM), lambda b, h: (b, 0, h)),
            pl.BlockSpec((None, n, GATE_ROWS, CHUNK), lambda b, h: (b, 0, h, 0)),
        ]
        args += [mqk, mqk, mv, mo, g]
    in_specs += [
        pl.BlockSpec((None, CONV_WIDTH, ML_DIM), lambda b, h: (layer, 0, h)),
        pl.BlockSpec((None, CONV_WIDTH, ML_DIM), lambda b, h: (layer, 0, ML_HEADS + h)),
        pl.BlockSpec((None, 1, ML_DIM), lambda b, h: (layer, 0, h)),
    ]
    args += [conv_w, conv_w, ml_gain]
    t_c, t_l = ctx_p[0].shape[1], lat_p[0].shape[1]
    out_l_shape = jax.ShapeDtypeStruct((bsz, t_l, ML_WIDTH), BF16)
    out_l_spec = pl.BlockSpec((None, t_l, ML_DIM), lambda b, h: (b, 0, h))
    if emit_ctx:
        out_shape = (jax.ShapeDtypeStruct((bsz, t_c, ML_WIDTH), BF16), out_l_shape)
        out_specs = (pl.BlockSpec((None, t_c, ML_DIM), lambda b, h: (b, 0, h)), out_l_spec)
    else:
        out_shape = out_l_shape
        out_specs = out_l_spec
    n_tot = t_tot // CHUNK
    res = pl.pallas_call(
        functools.partial(_mlstm_kernel, emit_ctx=emit_ctx),
        grid=(bsz, ML_HEADS),
        in_specs=in_specs,
        out_specs=out_specs,
        out_shape=out_shape,
        scratch_shapes=[
            pltpu.VMEM((t_tot, ML_DIM), BF16),
            pltpu.VMEM((n_tot, ML_DIM, CHUNK), BF16),
            pltpu.VMEM((2, n_tot, GATE_ROWS, CHUNK), F32),
            pltpu.VMEM((t_tot, ML_DIM), F32),
            pltpu.VMEM((t_tot, ML_DIM), F32),
            pltpu.VMEM((2, ML_DIM, 2 * ML_DIM), F32),
            pltpu.VMEM((2, 8, 128), F32),
        ],
        compiler_params=_cparams(2),
        name="mlstm_emit" if emit_ctx else "mlstm_last",
    )(*args)
    if emit_ctx:
        return res
    return None, res


def _outmlp_kernel(*refs, final, ffn_chunk):
    x_ref, att_ref, mem_ref, mod_ref, wo_ref, gain_ref, w1_ref, w2_ref = refs[:8]
    if final:
        gf_ref, o_ref = refs[8:]
    else:
        o_ref = refs[8]
    mod = mod_ref[...]
    y = _dot(att_ref[...], wo_ref[0:ATTN_WIDTH, :]) + _dot(mem_ref[...], wo_ref[ATTN_WIDTH:D_MODEL, :])
    x1 = x_ref[...] + mod[2:3] * y
    h = _rms_modulate(x1, gain_ref[...], mod[3:4], mod[4:5]).astype(BF16)
    acc = jnp.zeros(x1.shape, F32)
    for j in range(FFN_DIM // ffn_chunk):
        f = jnp.maximum(_dot(h, w1_ref[:, ffn_chunk * j:ffn_chunk * (j + 1)]), 0.0)
        acc = acc + _dot((f * f).astype(BF16), w2_ref[ffn_chunk * j:ffn_chunk * (j + 1), :])
    x2 = x1 + mod[5:6] * acc
    if final:
        ms = jnp.mean(x2 * x2, axis=-1, keepdims=True)
        x2 = x2 * lax.rsqrt(ms + NORM_EPS) * gf_ref[...]
    o_ref[...] = x2


def _outmlp_call(x, att, mem, mod, mod_row, wo, gain, w1, w2, layer, gf, *, tm, ffn_chunk=1024):
    bsz, t, d = x.shape
    final = gf is not None
    if mod_row is None:
        mod_map = lambda b, i: (b, 0, 0)
    else:
        mod_map = lambda b, i: (mod_row, 0, 0)
    in_specs = [
        pl.BlockSpec((None, tm, d), lambda b, i: (b, i, 0)),
        pl.BlockSpec((None, tm, ATTN_WIDTH), lambda b, i: (b, i, 0)),
        pl.BlockSpec((None, tm, ML_WIDTH), lambda b, i: (b, i, 0)),
        pl.BlockSpec((None, N_MOD, d), mod_map),
        _const_spec((None, d, d), lambda b, i: (layer, 0, 0)),
        _const_spec((None, 1, d), lambda b, i: (layer, 0, 0)),
        _const_spec((None, d, FFN_DIM), lambda b, i: (layer, 0, 0)),
        _const_spec((None, FFN_DIM, d), lambda b, i: (layer, 0, 0)),
    ]
    args = [x, att, mem, mod, wo, gain, w1, w2]
    if final:
        in_specs.append(_const_spec((1, d), lambda b, i: (0, 0)))
        args.append(gf)
    return pl.pallas_call(
        functools.partial(_outmlp_kernel, final=final, ffn_chunk=ffn_chunk),
        grid=(bsz, t // tm),
        in_specs=in_specs,
        out_specs=pl.BlockSpec((None, tm, d), lambda b, i: (b, i, 0)),
        out_shape=jax.ShapeDtypeStruct((bsz, t, d), F32),
        compiler_params=_cparams(2),
        name="outproj_mlp_final" if final else "outproj_mlp",
    )(*args)


def _projection_columns():
    a_q, a_k, a_v = 0, ATTN_WIDTH, ATTN_WIDTH + KV_WIDTH
    m_qk = ATTN_WIDTH + 2 * KV_WIDTH
    m_v = m_qk + 2 * ML_WIDTH
    m_o = m_v + ML_WIDTH
    gates = m_o + ML_WIDTH
    deint = np.concatenate([np.arange(0, HEAD_DIM, 2), np.arange(1, HEAD_DIM, 2)])
    idx = np.full((W_COLS,), -1, np.int64)
    idx[C_MQK:C_MQK + 2 * ML_WIDTH] = m_qk + np.arange(2 * ML_WIDTH)
    idx[C_Q:C_Q + ATTN_WIDTH] = a_q + (HEAD_DIM * np.arange(ATTN_HEADS)[:, None] + deint[None, :]).reshape(-1)
    idx[C_MV:C_MV + ML_WIDTH] = m_v + np.arange(ML_WIDTH)
    idx[C_MO:C_MO + ML_WIDTH] = m_o + np.arange(ML_WIDTH)
    idx[C_K:C_K + KV_WIDTH] = a_k + (HEAD_DIM * np.arange(KV_HEADS)[:, None] + deint[None, :]).reshape(-1)
    idx[C_V:C_V + KV_WIDTH] = a_v + np.arange(KV_WIDTH)
    for h in range(ML_HEADS):
        for kind in range(4):
            idx[C_G + GATE_ROWS * h + kind] = gates + kind * ML_HEADS + h
    return idx, deint


def _rope_tables(t):
    rows = t // GRID_W
    row_idx = jnp.repeat(jnp.arange(rows, dtype=F32), GRID_W)
    col_idx = jnp.tile(jnp.arange(GRID_W, dtype=F32), rows)
    inv_freq = jnp.power(ROPE_THETA, -jnp.arange(0, ROPE_AXIS_DIM, 2, dtype=F32) / ROPE_AXIS_DIM)
    ang = jnp.concatenate([row_idx[:, None] * inv_freq, col_idx[:, None] * inv_freq], axis=-1)
    cos, sin = jnp.cos(ang), jnp.sin(ang)
    cos128 = jnp.tile(cos, (1, 4))
    sin128 = jnp.tile(jnp.concatenate([-sin, sin], axis=-1), (1, 2))
    return cos128, sin128


def _pick_tile(t, pref):
    tm = min(pref, t)
    while t % tm:
        tm //= 2
    return tm


def kernel(x, c, ctx, c_ctx, w_ada, b_ada, norm_mix, norm_mlp, w_in, b_gates, conv_qk,
           q_norm, k_norm, mlstm_norm, w_out, w_mlp_in, w_mlp_out, norm_final):
    bsz, t, d = x.shape
    t_c = ctx.shape[1]
    depth = w_ada.shape[0]
    assert d == D_MODEL and t % CHUNK == 0 and t_c % CHUNK == 0 and t % GRID_W == 0

    rows = -(-(bsz + 1) // 8) * 8
    cvec = jnp.zeros((rows, d), F32).at[:bsz].set(c).at[bsz].set(c_ctx)
    mod = _ada_call(cvec, w_ada, b_ada).reshape(depth, rows, N_MOD, d)

    idx, deint = _projection_columns()
    w_all = jnp.where(idx[None, None, :] >= 0, jnp.take(w_in, np.maximum(idx, 0), axis=2), 0.0).astype(BF16)
    qg = jnp.tile(q_norm[:, deint], (1, 2)).reshape(depth, 1, 128)
    kg = jnp.tile(k_norm[:, deint], (1, 2)).reshape(depth, 1, 128)
    norm_mix = norm_mix.reshape(depth, 1, d)
    norm_mlp = norm_mlp.reshape(depth, 1, d)
    mlstm_norm = mlstm_norm.reshape(depth, 1, ML_WIDTH)
    bg = jnp.zeros((depth, ML_HEADS, GATE_ROWS), F32).at[:, :, 0:4].set(
        b_gates.reshape(depth, 4, ML_HEADS).transpose(0, 2, 1)).reshape(depth, ML_HEADS * GATE_ROWS, 1)
    wo = w_out.astype(BF16)
    w1 = w_mlp_in.astype(BF16)
    w2 = w_mlp_out.astype(BF16)
    cos128, sin128 = _rope_tables(t)
    gf = norm_final.reshape(1, d)

    tm_l = _pick_tile(t, 512)
    tm_c = _pick_tile(t_c, 512)
    tq_l = _pick_tile(t, 512)
    tq_c = _pick_tile(t_c, 256)

    for layer in range(depth):
        emit_ctx = layer < depth - 1
        mod_l = mod[layer]
        qt_l, k_l, vt_l, mqk_l, mv_l, mo_l, g_l = _inproj_call(
            x, mod_l, None, norm_mix, w_all, layer, qg, kg, bg, cos128, sin128, use_rope=True, tm=tm_l)
        qt_c, k_c, vt_c, mqk_c, mv_c, mo_c, g_c = _inproj_call(
            ctx, mod_l, bsz, norm_mix, w_all, layer, qg, kg, bg, cos128, sin128, use_rope=False, tm=tm_c)
        att_l = _attn_call(qt_l, [k_c, k_l], [vt_c, vt_l], tq=tq_l)
        mem_c, mem_l = _mlstm_call((mqk_c, mv_c, mo_c, g_c), (mqk_l, mv_l, mo_l, g_l),
                                   conv_qk, mlstm_norm, layer, emit_ctx=emit_ctx)
        x = _outmlp_call(x, att_l, mem_l, mod_l, None, wo, norm_mlp, w1, w2, layer,
                         None if emit_ctx else gf, tm=tm_l)
        if emit_ctx:
            att_c = _attn_call(qt_c, [k_c], [vt_c], tq=tq_c)
            ctx = _outmlp_call(ctx, att_c, mem_c, mod_l, bsz, wo, norm_mlp, w1, w2, layer, None, tm=tm_c)
    return x
```

```python
import functools

import numpy as np
import jax
import jax.numpy as jnp
from jax import lax
from jax.experimental import pallas as pl
from jax.experimental.pallas import tpu as pltpu

D_MODEL = 1024
N_MOD = 6
GRID_W = 64
ATTN_WIDTH = 512
ATTN_HEADS = 8
HEAD_DIM = 64
KV_HEADS = 2
ATTN_GROUP = ATTN_HEADS // KV_HEADS
KV_WIDTH = KV_HEADS * HEAD_DIM
ROPE_THETA = 10000.0
ROPE_AXIS_DIM = HEAD_DIM // 2
ML_WIDTH = 512
ML_HEADS = 4
ML_DIM = 128
CHUNK = 128
CONV_WIDTH = 3
N_GATES = 4 * ML_HEADS
GATE_ROWS = 8
FFN_DIM = 4 * D_MODEL
NORM_EPS = 1e-6
LOG2_E = 1.4426950408889634

C_MQK = 0
C_Q = C_MQK + 2 * ML_WIDTH
C_MV = C_Q + ATTN_WIDTH
C_MO = C_MV + ML_WIDTH
C_K = C_MO + ML_WIDTH
C_V = C_K + KV_WIDTH
C_G = C_V + KV_WIDTH
W_COLS = C_G + 128

VMEM_LIMIT_BYTES = 56 * 1024 * 1024

BF16 = jnp.bfloat16
F32 = jnp.float32


def _cparams(n_grid):
    return pltpu.CompilerParams(
        dimension_semantics=("arbitrary",) * n_grid,
        vmem_limit_bytes=VMEM_LIMIT_BYTES)


def _const_spec(shape, index_map):
    return pl.BlockSpec(shape, index_map, pipeline_mode=pl.Buffered(1))


def _aligned(i, m):
    return i if isinstance(i, int) else pl.multiple_of(i, m)


def _dot(a, b):
    return jnp.dot(a, b, preferred_element_type=F32)


def _sigmoid(x):
    return 1.0 / (1.0 + jnp.exp(-x))


def _log_sigmoid(x):
    return jnp.minimum(x, 0.0) - jnp.log(1.0 + jnp.exp(-jnp.abs(x)))


def _group_ones(n, group):
    r = lax.broadcasted_iota(jnp.int32, (n, n), 0) // group
    c = lax.broadcasted_iota(jnp.int32, (n, n), 1) // group
    return jnp.where(r == c, 1.0, 0.0).astype(BF16)


def _split_dot(a, ones_mat):
    hi = a.astype(BF16)
    lo = (a - hi.astype(F32)).astype(BF16)
    return _dot(hi, ones_mat) + _dot(lo, ones_mat)


def _ada_kernel(c_ref, w_ref, b_ref, o_ref):
    c = c_ref[...]
    sc = (c * _sigmoid(c)).astype(BF16)
    o_ref[0] = _dot(sc, w_ref[0].astype(BF16)) + b_ref[0]


def _ada_call(cvec, w_ada, b_ada):
    depth, d, n = w_ada.shape
    rows = cvec.shape[0]
    tn = 1536
    return pl.pallas_call(
        _ada_kernel,
        grid=(depth, n // tn),
        in_specs=[
            pl.BlockSpec((rows, d), lambda l, j: (0, 0)),
            pl.BlockSpec((1, d, tn), lambda l, j: (l, 0, j)),
            pl.BlockSpec((1, 1, tn), lambda l, j: (l, 0, j)),
        ],
        out_specs=pl.BlockSpec((1, rows, tn), lambda l, j: (l, 0, j)),
        out_shape=jax.ShapeDtypeStruct((depth, rows, n), F32),
        compiler_params=_cparams(2),
        name="ada_mod",
    )(cvec, w_ada, b_ada.reshape(depth, 1, n))


def _rms_modulate(x, gain, shift, scale):
    ms = jnp.mean(x * x, axis=-1, keepdims=True)
    y = x * lax.rsqrt(ms + NORM_EPS) * gain
    return y * (1.0 + scale) + shift


def _swap32(x):
    lane = lax.broadcasted_iota(jnp.int32, x.shape, 1)
    up = pltpu.roll(x, 96, 1)
    down = pltpu.roll(x, 32, 1)
    return jnp.where((lane % HEAD_DIM) < ROPE_AXIS_DIM, up, down)


def _head_norm_rope(t, gain, cos, sin, use_rope):
    ss = _dot((t * t).astype(BF16), _group_ones(128, HEAD_DIM))
    tn = t * lax.rsqrt(ss * (1.0 / HEAD_DIM) + NORM_EPS) * gain
    if use_rope:
        tn = tn * cos + _swap32(tn) * sin
    return tn


def _inproj_kernel(x_ref, mod_ref, gain_ref, w_ref, qg_ref, kg_ref, bg_ref, cos_ref, sin_ref,
                   qt_out, k_out, vt_out, mqk_out, mv_out, mo_out, g_out, *, use_rope):
    x = x_ref[...]
    mod = mod_ref[...]
    h = _rms_modulate(x, gain_ref[...], mod[0:1], mod[1:2]).astype(BF16)
    tm = x.shape[0]

    mqk_out[...] = _dot(h, w_ref[:, C_MQK:C_MQK + 2 * ML_WIDTH])
    mv_out[...] = _dot(h, w_ref[:, C_MV:C_MV + ML_WIDTH]).astype(BF16)
    mo_out[...] = _sigmoid(_dot(h, w_ref[:, C_MO:C_MO + ML_WIDTH])).astype(BF16)

    vt = _dot(h, w_ref[:, C_V:C_V + KV_WIDTH]).T
    pad_row = lax.broadcasted_iota(jnp.int32, (HEAD_DIM, tm), 0)
    pad = jnp.where(pad_row == 0, 1.0, 0.0)
    for kvh in range(KV_HEADS):
        vt_out[128 * kvh:128 * kvh + HEAD_DIM, :] = vt[HEAD_DIM * kvh:HEAD_DIM * (kvh + 1)].astype(BF16)
        vt_out[128 * kvh + HEAD_DIM:128 * (kvh + 1), :] = pad.astype(BF16)

    cos = cos_ref[...] if use_rope else None
    sin = sin_ref[...] if use_rope else None

    q = _dot(h, w_ref[:, C_Q:C_Q + ATTN_WIDTH])
    qscale = HEAD_DIM ** -0.5 * LOG2_E
    for j in range(ATTN_WIDTH // 128):
        blk = _head_norm_rope(q[:, 128 * j:128 * (j + 1)], qg_ref[...], cos, sin, use_rope)
        qt_out[128 * j:128 * (j + 1), :] = (blk * qscale).T.astype(BF16)

    k = _dot(h, w_ref[:, C_K:C_K + KV_WIDTH])
    k_out[...] = _head_norm_rope(k, kg_ref[...], cos, sin, use_rope).astype(BF16)

    g = _dot(h, w_ref[:, C_G:C_G + 128])
    gt = g.T[0:ML_HEADS * GATE_ROWS, :] + bg_ref[...]
    row = lax.broadcasted_iota(jnp.int32, gt.shape, 0) % GATE_ROWS
    gt = jnp.where((row == 1) | (row == 3), _log_sigmoid(gt), gt)
    for j in range(tm // CHUNK):
        g_out[j] = gt[:, CHUNK * j:CHUNK * (j + 1)]


def _inproj_call(x, mod, mod_row, gain, w_all, layer, qg, kg, bg, cos, sin, *, use_rope, tm):
    bsz, t, d = x.shape
    nt = t // tm
    nc = t // CHUNK
    cpt = tm // CHUNK
    if mod_row is None:
        mod_map = lambda b, i: (b, 0, 0)
    else:
        mod_map = lambda b, i: (mod_row, 0, 0)
    tok = lambda w: pl.BlockSpec((None, tm, w), lambda b, i: (b, i, 0))
    out_shapes = (
        jax.ShapeDtypeStruct((bsz, ATTN_WIDTH, t), BF16),
        jax.ShapeDtypeStruct((bsz, t, KV_WIDTH), BF16),
        jax.ShapeDtypeStruct((bsz, KV_HEADS * 128, t), BF16),
        jax.ShapeDtypeStruct((bsz, t, 2 * ML_WIDTH), F32),
        jax.ShapeDtypeStruct((bsz, t, ML_WIDTH), BF16),
        jax.ShapeDtypeStruct((bsz, t, ML_WIDTH), BF16),
        jax.ShapeDtypeStruct((bsz, nc, ML_HEADS * GATE_ROWS, CHUNK), F32),
    )
    out_specs = (
        pl.BlockSpec((None, ATTN_WIDTH, tm), lambda b, i: (b, 0, i)),
        tok(KV_WIDTH),
        pl.BlockSpec((None, KV_HEADS * 128, tm), lambda b, i: (b, 0, i)),
        tok(2 * ML_WIDTH),
        tok(ML_WIDTH),
        tok(ML_WIDTH),
        pl.BlockSpec((None, cpt, ML_HEADS * GATE_ROWS, CHUNK), lambda b, i: (b, i, 0, 0)),
    )
    in_specs = [
        tok(d),
        pl.BlockSpec((None, N_MOD, d), mod_map),
        _const_spec((None, 1, d), lambda b, i: (layer, 0, 0)),
        _const_spec((None, d, W_COLS), lambda b, i: (layer, 0, 0)),
        _const_spec((None, 1, 128), lambda b, i: (layer, 0, 0)),
        _const_spec((None, 1, 128), lambda b, i: (layer, 0, 0)),
        _const_spec((None, ML_HEADS * GATE_ROWS, 1), lambda b, i: (layer, 0, 0)),
        pl.BlockSpec((tm, 128), lambda b, i: (i, 0)),
        pl.BlockSpec((tm, 128), lambda b, i: (i, 0)),
    ]
    return pl.pallas_call(
        functools.partial(_inproj_kernel, use_rope=use_rope),
        grid=(bsz, nt),
        in_specs=in_specs,
        out_specs=out_specs,
        out_shape=out_shapes,
        compiler_params=_cparams(2),
        name="inproj_rope" if use_rope else "inproj_ctx",
    )(x, mod, gain, w_all, qg, kg, bg, cos, sin)


def _attn_kernel(*refs, n_seg, n_chunks, q_block):
    qt_ref = refs[0]
    k_refs = refs[1:1 + n_seg]
    vt_refs = refs[1 + n_seg:1 + 2 * n_seg]
    o_ref = refs[1 + 2 * n_seg]
    k_all, vt_all, st_ref, acc_ref = refs[2 + 2 * n_seg:]
    tq = qt_ref.shape[1]
    cols = ATTN_GROUP * q_block
    s_tot = k_all.shape[0]
    ck = s_tot // n_chunks

    @pl.when(pl.program_id(1) == 0)
    def _():
        base = 0
        for seg in range(n_seg):
            s_len = k_refs[seg].shape[0]
            k_all[base:base + s_len, :] = k_refs[seg][...]
            vt_all[:, base:base + s_len] = vt_refs[seg][...]
            base += s_len

    groups = [(qb, kvh) for qb in range(tq // q_block) for kvh in range(KV_HEADS)]
    zeros = jnp.zeros((HEAD_DIM, q_block), BF16)

    def padded_queries(qb, kvh):
        pieces = []
        for g in range(ATTN_GROUP):
            row0 = HEAD_DIM * (ATTN_GROUP * kvh + g)
            qg = qt_ref[row0:row0 + HEAD_DIM, q_block * qb:q_block * (qb + 1)]
            pieces.append(jnp.concatenate([qg, zeros] if kvh == 0 else [zeros, qg], axis=0))
        return jnp.concatenate(pieces, axis=1)

    def finalize(qb, kvh):
        acc = acc_ref[...]
        ot = acc[0:HEAD_DIM] * (1.0 / acc[HEAD_DIM:HEAD_DIM + 1])
        for pair in range(ATTN_GROUP // 2):
            two = jnp.concatenate([ot[:, q_block * (2 * pair):q_block * (2 * pair + 1)],
                                   ot[:, q_block * (2 * pair + 1):q_block * (2 * pair + 2)]], axis=0)
            lane0 = HEAD_DIM * (ATTN_GROUP * kvh + 2 * pair)
            o_ref[q_block * qb:q_block * (qb + 1), lane0:lane0 + 2 * HEAD_DIM] = two.T.astype(BF16)

    m_prev = None
    for j in range(len(groups) + 1):
        qz = padded_queries(*groups[j]) if j < len(groups) else None
        if j >= 1:
            acc_ref[...] = jnp.zeros((128, cols), F32)

        def body(c, mx, j=j, qz=qz, m_prev=m_prev):
            off = _aligned(c * ck, 128)
            if qz is not None:
                st = _dot(k_all[pl.ds(off, ck), :], qz)
                st_ref[j % 2, pl.ds(off, ck), :] = st
                mx = jnp.maximum(mx, jnp.max(st.reshape(ck // 8, 8, cols), axis=0))
            if j >= 1:
                kvh = groups[j - 1][1]
                p = jnp.exp2(st_ref[(j - 1) % 2, pl.ds(off, ck), :] - m_prev).astype(BF16)
                acc_ref[...] += _dot(vt_all[128 * kvh:128 * (kvh + 1), pl.ds(off, ck)], p)
            return mx

        mx0 = jnp.full((8, cols), -jnp.inf, F32)
        if n_chunks == 1:
            mx = body(0, mx0)
        else:
            mx = lax.fori_loop(0, n_chunks, body, mx0)
        if j >= 1:
            finalize(*groups[j - 1])
        m_prev = jnp.max(mx, axis=0, keepdims=True)


def _attn_call(qt, ks, vts, *, tq, q_block=128):
    bsz, _, t = qt.shape
    n_seg = len(ks)
    s_tot = sum(k.shape[1] for k in ks)
    n_chunks = 1
    q_block = min(q_block, tq)
    cols = ATTN_GROUP * q_block
    in_specs = [pl.BlockSpec((None, ATTN_WIDTH, tq), lambda b, i: (b, 0, i))]
    for k in ks:
        in_specs.append(pl.BlockSpec((None, k.shape[1], KV_WIDTH), lambda b, i: (b, 0, 0)))
    for vt in vts:
        in_specs.append(pl.BlockSpec((None, KV_HEADS * 128, vt.shape[2]), lambda b, i: (b, 0, 0)))
    return pl.pallas_call(
        functools.partial(_attn_kernel, n_seg=n_seg, n_chunks=n_chunks, q_block=q_block),
        grid=(bsz, t // tq),
        in_specs=in_specs,
        out_specs=pl.BlockSpec((None, tq, ATTN_WIDTH), lambda b, i: (b, i, 0)),
        out_shape=jax.ShapeDtypeStruct((bsz, t, ATTN_WIDTH), BF16),
        scratch_shapes=[
            pltpu.VMEM((s_tot, KV_WIDTH), BF16),
            pltpu.VMEM((KV_HEADS * 128, s_tot), BF16),
            pltpu.VMEM((2, s_tot, cols), F32),
            pltpu.VMEM((128, cols), F32),
        ],
        compiler_params=_cparams(2),
        name="attention_%dseg" % n_seg,
    )(qt, *ks, *vts)


def _mlstm_kernel(*refs, emit_ctx):
    (mq_c, mk_c, mv_c, mo_c, g_c, mq_l, mk_l, mv_l, mo_l, g_l, cw_q, cw_k, gain_ref) = refs[:13]
    if emit_ctx:
        out_c, out_l = refs[13:15]
        scratch = refs[15:]
    else:
        out_c = None
        out_l = refs[13]
        scratch = refs[14:]
    q_s, kt_s, cum_s, x_s, u_s, pm_s, bc_s, sc_s, hf_s, ct_s, m_s = scratch

    L = CHUNK
    segs = []
    chunk_base = 0
    for (mq, mk, mv, mo, g, out, emit) in ((mq_c, mk_c, mv_c, mo_c, g_c, out_c, emit_ctx),
                                           (mq_l, mk_l, mv_l, mo_l, g_l, out_l, True)):
        n = mq.shape[0] // L
        segs.append(dict(mq=mq, mk=mk, mv=mv, mo=mo, g=g, out=out, emit=emit, n=n, cb=chunk_base))
        chunk_base += n

    ri = lax.broadcasted_iota(jnp.int32, (L, L), 0)
    ci = lax.broadcasted_iota(jnp.int32, (L, L), 1)
    ones_le = jnp.where(ri <= ci, 1.0, 0.0).astype(BF16)
    ones_ge = jnp.where(ri >= ci, 1.0, 0.0).astype(BF16)
    masks = (ci <= ri, ci >= ri)
    row_i = lax.broadcasted_iota(jnp.int32, (L, ML_DIM), 0)
    ones_blk = jnp.ones((L, 128), BF16)
    kscale = ML_DIM ** -0.5

    def unroll_of(n, want):
        while n % want:
            want //= 2
        return want

    for sg in segs:
        n, cb = sg["n"], sg["cb"]
        t_seg = n * L

        def conv_silu(src, cw, c, t_seg=t_seg, n=n):
            t0 = pl.multiple_of(c * L, L)
            xc = src[pl.ds(t0, L), :]
            prev = src[pl.ds(jnp.maximum(t0 - 1, 0), 1), :] * jnp.where(c > 0, 1.0, 0.0)
            nxt = src[pl.ds(jnp.minimum(t0 + L, t_seg - 1), 1), :] * jnp.where(c < n - 1, 1.0, 0.0)
            xm = jnp.where(row_i == 0, prev, pltpu.roll(xc, 1, 0))
            xp = jnp.where(row_i == L - 1, nxt, pltpu.roll(xc, L - 1, 0))
            y = xm * cw[0:1, :] + xc * cw[1:2, :] + xp * cw[2:3, :]
            return y * _sigmoid(y)

        def prep(c, carry, sg=sg, cb=cb, conv_silu=conv_silu):
            t0g = pl.multiple_of((cb + c) * L, L)
            q_s[pl.ds(t0g, L), :] = conv_silu(sg["mq"], cw_q, c).astype(BF16)
            kt_s[cb + c] = (conv_silu(sg["mk"], cw_k, c) * kscale).T.astype(BF16)
            return carry

        lax.fori_loop(0, n, prep, 0, unroll=unroll_of(n, 2))
        g2 = sg["g"][...].reshape(n * GATE_ROWS, L)
        cum_s[0, cb:cb + n] = _split_dot(g2, ones_le).reshape(n, GATE_ROWS, L)
        cum_s[1, cb:cb + n] = _split_dot(g2, ones_ge).reshape(n, GATE_ROWS, L)

    def gate_rows(d, sg, c):
        gr = sg["g"][c]
        cum = cum_s[d, sg["cb"] + c]
        if d == 0:
            return gr[0:1], gr[1:2], cum[1:2], cum[1:2, L - 1:L]
        return gr[2:3], gr[3:4], cum[3:4], cum[3:4, 0:1]

    def state_free_pass(d):
        for sg in segs:
            group = unroll_of(sg["n"], 4)

            def body(i, carry, sg=sg, group=group):
                pending = []
                for j in range(group):
                    c = i * group + j
                    cg = sg["cb"] + c
                    t0 = pl.multiple_of(c * L, L)
                    t0g = pl.multiple_of(cg * L, L)
                    ig, lf, brow, total = gate_rows(d, sg, c)
                    ktc = kt_s[cg]
                    v_aug = jnp.concatenate([sg["mv"][pl.ds(t0, L), :], ones_blk], axis=1)
                    a = ig - brow
                    amax = jnp.max(a, axis=-1, keepdims=True)
                    kw = (ktc.astype(F32) * jnp.exp(a - amax)).astype(BF16)
                    u_s[cg] = _dot(kw, v_aug)
                    sc_s[cg] = jnp.concatenate(
                        [jnp.broadcast_to(amax, (1, L)), jnp.broadcast_to(total, (1, L)),
                         jnp.zeros((GATE_ROWS - 2, L), F32)], axis=0)
                    if sg["emit"]:
                        a_vis = jnp.where(masks[d], a, -jnp.inf)
                        pm = jnp.broadcast_to(jnp.max(a_vis, axis=-1, keepdims=True), (L, L))
                        bc = jnp.broadcast_to(
                            jnp.sum(jnp.where(masks[d], lf, 0.0), axis=-1, keepdims=True), (L, L))
                        pm_s[cg] = pm
                        bc_s[cg] = bc
                        s0 = _dot(q_s[pl.ds(t0g, L), :], ktc)
                        pending.append((cg, s0, jnp.exp(a_vis - pm), v_aug))
                for cg, s0, e, v_aug in pending:
                    x_s[cg] = _dot((s0 * e).astype(BF16), v_aug)
                return carry
            lax.fori_loop(0, sg["n"] // group, body, 0)

    def sequential_pass(d):
        ct_s[...] = jnp.zeros(ct_s.shape, F32)
        m_s[...] = jnp.zeros(m_s.shape, F32)
        for sg in segs:
            n = sg["n"]
            group = unroll_of(n, 4)

            def body(i, carry, sg=sg, n=n, group=group):
                m_prev = m_s[0:1, 0:1]
                ct = ct_s[...]
                pending = []
                for j in range(group):
                    c = i * group + j if d == 0 else n - 1 - (i * group + j)
                    cg = sg["cb"] + c
                    t0 = pl.multiple_of(c * L, L)
                    t0g = pl.multiple_of(cg * L, L)
                    sc = sc_s[cg]
                    amax, total = sc[0:1, 0:1], sc[1:2, 0:1]
                    m_last = jnp.maximum(m_prev, amax)
                    if sg["emit"]:
                        inter = _dot(q_s[pl.ds(t0g, L), :], ct.astype(BF16))
                        pending.append((cg, t0, t0g, inter, m_prev))
                    ct = jnp.exp(m_prev - m_last) * ct + jnp.exp(amax - m_last) * u_s[cg]
                    m_prev = total + m_last
                ct_s[...] = ct
                m_s[...] = jnp.broadcast_to(m_prev, m_s.shape)
                for cg, t0, t0g, inter, m_in in pending:
                    pm = pm_s[cg]
                    mx = jnp.maximum(pm, m_in)
                    r = jnp.exp(pm - mx)
                    wi = jnp.exp(m_in - mx)
                    x = x_s[cg]
                    num = r * x[:, 0:ML_DIM] + wi * inter[:, 0:ML_DIM]
                    den = r * x[:, ML_DIM:] + wi * inter[:, ML_DIM:]
                    h = num / jnp.maximum(jnp.abs(den), jnp.exp(-bc_s[cg] - mx))
                    if d == 0:
                        hf_s[pl.ds(t0g, L), :] = h
                    else:
                        hsum = hf_s[pl.ds(t0g, L), :] + h
                        ms = jnp.mean(hsum * hsum, axis=-1, keepdims=True)
                        hn = hsum * lax.rsqrt(ms + NORM_EPS) * gain_ref[...]
                        sg["out"][pl.ds(t0, L), :] = (sg["mo"][pl.ds(t0, L), :].astype(F32) * hn).astype(BF16)
                return carry

            lax.fori_loop(0, n // group, body, 0)

    for d in range(2):
        state_free_pass(d)
        sequential_pass(d)


def _mlstm_call(ctx_p, lat_p, conv_w, ml_gain, layer, *, emit_ctx):
    bsz = lat_p[0].shape[0]
    in_specs = []
    args = []
    t_tot = 0
    for (mqk, mv, mo, g) in (ctx_p, lat_p):
        t = mqk.shape[1]
        n = t // CHUNK
        t_tot += t
        in_specs += [
            pl.BlockSpec((None, t, ML_DIM), lambda b, h: (b, 0, h)),
            pl.BlockSpec((None, t, ML_DIM), lambda b, h: (b, 0, ML_HEADS + h)),
            pl.BlockSpec((None, t, ML_DIM), lambda b, h: (b, 0, h)),
            pl.BlockSpec((None, t, ML_DIM), lambda b, h: (b, 0, h)),
            pl.BlockSpec((None, n, GATE_ROWS, CHUNK), lambda b, h: (b, 0, h, 0)),
        ]
        args += [mqk, mqk, mv, mo, g]
    in_specs += [
        pl.BlockSpec((None, CONV_WIDTH, ML_DIM), lambda b, h: (layer, 0, h)),
        pl.BlockSpec((None, CONV_WIDTH, ML_DIM), lambda b, h: (layer, 0, ML_HEADS + h)),
        pl.BlockSpec((None, 1, ML_DIM), lambda b, h: (layer, 0, h)),
    ]
    args += [conv_w, conv_w, ml_gain]
    t_c, t_l = ctx_p[0].shape[1], lat_p[0].shape[1]
    out_l_shape = jax.ShapeDtypeStruct((bsz, t_l, ML_WIDTH), BF16)
    out_l_spec = pl.BlockSpec((None, t_l, ML_DIM), lambda b, h: (b, 0, h))
    if emit_ctx:
        out_shape = (jax.ShapeDtypeStruct((bsz, t_c, ML_WIDTH), BF16), out_l_shape)
        out_specs = (pl.BlockSpec((None, t_c, ML_DIM), lambda b, h: (b, 0, h)), out_l_spec)
    else:
        out_shape = out_l_shape
        out_specs = out_l_spec
    n_tot = t_tot // CHUNK
    res = pl.pallas_call(
        functools.partial(_mlstm_kernel, emit_ctx=emit_ctx),
        grid=(bsz, ML_HEADS),
        in_specs=in_specs,
        out_specs=out_specs,
        out_shape=out_shape,
        scratch_shapes=[
            pltpu.VMEM((t_tot, ML_DIM), BF16),
            pltpu.VMEM((n_tot, ML_DIM, CHUNK), BF16),
            pltpu.VMEM((2, n_tot, GATE_ROWS, CHUNK), F32),
            pltpu.VMEM((n_tot, CHUNK, 2 * ML_DIM), F32),
            pltpu.VMEM((n_tot, ML_DIM, 2 * ML_DIM), F32),
            pltpu.VMEM((n_tot, CHUNK, 128), F32),
            pltpu.VMEM((n_tot, CHUNK, 128), F32),
            pltpu.VMEM((n_tot, GATE_ROWS, CHUNK), F32),
            pltpu.VMEM((t_tot, ML_DIM), F32),
            pltpu.VMEM((ML_DIM, 2 * ML_DIM), F32),
            pltpu.VMEM((8, 128), F32),
        ],
        compiler_params=_cparams(2),
        name="mlstm_emit" if emit_ctx else "mlstm_last",
    )(*args)
    if emit_ctx:
        return res
    return None, res


def _outmlp_kernel(*refs, final, ffn_chunk):
    x_ref, att_ref, mem_ref, mod_ref, wo_ref, gain_ref, w1_ref, w2_ref = refs[:8]
    if final:
        gf_ref, o_ref = refs[8:]
    else:
        o_ref = refs[8]
    mod = mod_ref[...]
    y = _dot(att_ref[...], wo_ref[0:ATTN_WIDTH, :]) + _dot(mem_ref[...], wo_ref[ATTN_WIDTH:D_MODEL, :])
    x1 = x_ref[...] + mod[2:3] * y
    h = _rms_modulate(x1, gain_ref[...], mod[3:4], mod[4:5]).astype(BF16)
    acc = jnp.zeros(x1.shape, F32)
    for j in range(FFN_DIM // ffn_chunk):
        f = jnp.maximum(_dot(h, w1_ref[:, ffn_chunk * j:ffn_chunk * (j + 1)]), 0.0)
        acc = acc + _dot((f * f).astype(BF16), w2_ref[ffn_chunk * j:ffn_chunk * (j + 1), :])
    x2 = x1 + mod[5:6] * acc
    if final:
        ms = jnp.mean(x2 * x2, axis=-1, keepdims=True)
        x2 = x2 * lax.rsqrt(ms + NORM_EPS) * gf_ref[...]
    o_ref[...] = x2


def _outmlp_call(x, att, mem, mod, mod_row, wo, gain, w1, w2, layer, gf, *, tm, ffn_chunk=1024):
    bsz, t, d = x.shape
    final = gf is not None
    if mod_row is None:
        mod_map = lambda b, i: (b, 0, 0)
    else:
        mod_map = lambda b, i: (mod_row, 0, 0)
    in_specs = [
        pl.BlockSpec((None, tm, d), lambda b, i: (b, i, 0)),
        pl.BlockSpec((None, tm, ATTN_WIDTH), lambda b, i: (b, i, 0)),
        pl.BlockSpec((None, tm, ML_WIDTH), lambda b, i: (b, i, 0)),
        pl.BlockSpec((None, N_MOD, d), mod_map),
        _const_spec((None, d, d), lambda b, i: (layer, 0, 0)),
        _const_spec((None, 1, d), lambda b, i: (layer, 0, 0)),
        _const_spec((None, d, FFN_DIM), lambda b, i: (layer, 0, 0)),
        _const_spec((None, FFN_DIM, d), lambda b, i: (layer, 0, 0)),
    ]
    args = [x, att, mem, mod, wo, gain, w1, w2]
    if final:
        in_specs.append(_const_spec((1, d), lambda b, i: (0, 0)))
        args.append(gf)
    return pl.pallas_call(
        functools.partial(_outmlp_kernel, final=final, ffn_chunk=ffn_chunk),
        grid=(bsz, t // tm),
        in_specs=in_specs,
        out_specs=pl.BlockSpec((None, tm, d), lambda b, i: (b, i, 0)),
        out_shape=jax.ShapeDtypeStruct((bsz, t, d), F32),
        compiler_params=_cparams(2),
        name="outproj_mlp_final" if final else "outproj_mlp",
    )(*args)


def _projection_columns():
    a_q, a_k, a_v = 0, ATTN_WIDTH, ATTN_WIDTH + KV_WIDTH
    m_qk = ATTN_WIDTH + 2 * KV_WIDTH
    m_v = m_qk + 2 * ML_WIDTH
    m_o = m_v + ML_WIDTH
    gates = m_o + ML_WIDTH
    deint = np.concatenate([np.arange(0, HEAD_DIM, 2), np.arange(1, HEAD_DIM, 2)])
    idx = np.full((W_COLS,), -1, np.int64)
    idx[C_MQK:C_MQK + 2 * ML_WIDTH] = m_qk + np.arange(2 * ML_WIDTH)
    idx[C_Q:C_Q + ATTN_WIDTH] = a_q + (HEAD_DIM * np.arange(ATTN_HEADS)[:, None] + deint[None, :]).reshape(-1)
    idx[C_MV:C_MV + ML_WIDTH] = m_v + np.arange(ML_WIDTH)
    idx[C_MO:C_MO + ML_WIDTH] = m_o + np.arange(ML_WIDTH)
    idx[C_K:C_K + KV_WIDTH] = a_k + (HEAD_DIM * np.arange(KV_HEADS)[:, None] + deint[None, :]).reshape(-1)
    idx[C_V:C_V + KV_WIDTH] = a_v + np.arange(KV_WIDTH)
    for h in range(ML_HEADS):
        for kind in range(4):
            idx[C_G + GATE_ROWS * h + kind] = gates + kind * ML_HEADS + h
    return idx, deint


def _rope_tables(t):
    rows = t // GRID_W
    row_idx = jnp.repeat(jnp.arange(rows, dtype=F32), GRID_W)
    col_idx = jnp.tile(jnp.arange(GRID_W, dtype=F32), rows)
    inv_freq = jnp.power(ROPE_THETA, -jnp.arange(0, ROPE_AXIS_DIM, 2, dtype=F32) / ROPE_AXIS_DIM)
    ang = jnp.concatenate([row_idx[:, None] * inv_freq, col_idx[:, None] * inv_freq], axis=-1)
    cos, sin = jnp.cos(ang), jnp.sin(ang)
    cos128 = jnp.tile(cos, (1, 4))
    sin128 = jnp.tile(jnp.concatenate([-sin, sin], axis=-1), (1, 2))
    return cos128, sin128


def _pick_tile(t, pref):
    tm = min(pref, t)
    while t % tm:
        tm //= 2
    return tm


def kernel(x, c, ctx, c_ctx, w_ada, b_ada, norm_mix, norm_mlp, w_in, b_gates, conv_qk,
           q_norm, k_norm, mlstm_norm, w_out, w_mlp_in, w_mlp_out, norm_final):
    bsz, t, d = x.shape
    t_c = ctx.shape[1]
    depth = w_ada.shape[0]
    assert d == D_MODEL and t % CHUNK == 0 and t_c % CHUNK == 0 and t % GRID_W == 0

    rows = -(-(bsz + 1) // 8) * 8
    cvec = jnp.zeros((rows, d), F32).at[:bsz].set(c).at[bsz].set(c_ctx)
    mod = _ada_call(cvec, w_ada, b_ada).reshape(depth, rows, N_MOD, d)

    idx, deint = _projection_columns()
    w_all = jnp.where(idx[None, None, :] >= 0, jnp.take(w_in, np.maximum(idx, 0), axis=2), 0.0).astype(BF16)
    qg = jnp.tile(q_norm[:, deint], (1, 2)).reshape(depth, 1, 128)
    kg = jnp.tile(k_norm[:, deint], (1, 2)).reshape(depth, 1, 128)
    norm_mix = norm_mix.reshape(depth, 1, d)
    norm_mlp = norm_mlp.reshape(depth, 1, d)
    mlstm_norm = mlstm_norm.reshape(depth, 1, ML_WIDTH)
    bg = jnp.zeros((depth, ML_HEADS, GATE_ROWS), F32).at[:, :, 0:4].set(
        b_gates.reshape(depth, 4, ML_HEADS).transpose(0, 2, 1)).reshape(depth, ML_HEADS * GATE_ROWS, 1)
    wo = w_out.astype(BF16)
    w1 = w_mlp_in.astype(BF16)
    w2 = w_mlp_out.astype(BF16)
    cos128, sin128 = _rope_tables(t)
    gf = norm_final.reshape(1, d)

    tm_l = _pick_tile(t, 512)
    tm_c = _pick_tile(t_c, 512)
    tq_l = _pick_tile(t, 512)
    tq_c = _pick_tile(t_c, 256)

    for layer in range(depth):
        emit_ctx = layer < depth - 1
        mod_l = mod[layer]
        qt_l, k_l, vt_l, mqk_l, mv_l, mo_l, g_l = _inproj_call(
            x, mod_l, None, norm_mix, w_all, layer, qg, kg, bg, cos128, sin128, use_rope=True, tm=tm_l)
        qt_c, k_c, vt_c, mqk_c, mv_c, mo_c, g_c = _inproj_call(
            ctx, mod_l, bsz, norm_mix, w_all, layer, qg, kg, bg, cos128, sin128, use_rope=False, tm=tm_c)
        att_l = _attn_call(qt_l, [k_c, k_l], [vt_c, vt_l], tq=tq_l)
        mem_c, mem_l = _mlstm_call((mqk_c, mv_c, mo_c, g_c), (mqk_l, mv_l, mo_l, g_l),
                                   conv_qk, mlstm_norm, layer, emit_ctx=emit_ctx)
        x = _outmlp_call(x, att_l, mem_l, mod_l, None, wo, norm_mlp, w1, w2, layer,
                         None if emit_ctx else gf, tm=tm_l)
        if emit_ctx:
            att_c = _attn_call(qt_c, [k_c], [vt_c], tq=tq_c)
            ctx = _outmlp_call(ctx, att_c, mem_c, mod_l, bsz, wo, norm_mlp, w1, w2, layer, None, tm=tm_c)
    return x
```

```python
import functools

import numpy as np
import jax
import jax.numpy as jnp
from jax import lax
from jax.experimental import pallas as pl
from jax.experimental.pallas import tpu as pltpu

D_MODEL = 1024
N_MOD = 6
GRID_W = 64
ATTN_WIDTH = 512
ATTN_HEADS = 8
HEAD_DIM = 64
KV_HEADS = 2
ATTN_GROUP = ATTN_HEADS // KV_HEADS
KV_WIDTH = KV_HEADS * HEAD_DIM
ROPE_THETA = 10000.0
ROPE_AXIS_DIM = HEAD_DIM // 2
ML_WIDTH = 512
ML_HEADS = 4
ML_DIM = 128
CHUNK = 128
CONV_WIDTH = 3
N_GATES = 4 * ML_HEADS
GATE_ROWS = 8
FFN_DIM = 4 * D_MODEL
NORM_EPS = 1e-6
LOG2_E = 1.4426950408889634
V_ROWS = HEAD_DIM + 16

C_MQK = 0
C_Q = C_MQK + 2 * ML_WIDTH
C_MV = C_Q + ATTN_WIDTH
C_MO = C_MV + ML_WIDTH
C_K = C_MO + ML_WIDTH
C_V = C_K + KV_WIDTH
C_G = C_V + KV_WIDTH
W_COLS = C_G + 128

VMEM_LIMIT_BYTES = 56 * 1024 * 1024

BF16 = jnp.bfloat16
F32 = jnp.float32


def _cparams(n_grid, flags=None):
    return pltpu.CompilerParams(
        dimension_semantics=("arbitrary",) * n_grid,
        vmem_limit_bytes=VMEM_LIMIT_BYTES,
        flags=flags)


def _const_spec(shape, index_map):
    return pl.BlockSpec(shape, index_map, pipeline_mode=pl.Buffered(1))


def _aligned(i, m):
    return i if isinstance(i, int) else pl.multiple_of(i, m)


def _dot(a, b):
    return jnp.dot(a, b, preferred_element_type=F32)


def _sigmoid(x):
    return 1.0 / (1.0 + jnp.exp(-x))


def _log_sigmoid(x):
    return jnp.minimum(x, 0.0) - jnp.log(1.0 + jnp.exp(-jnp.abs(x)))


def _group_ones(n, group):
    r = lax.broadcasted_iota(jnp.int32, (n, n), 0) // group
    c = lax.broadcasted_iota(jnp.int32, (n, n), 1) // group
    return jnp.where(r == c, 1.0, 0.0).astype(BF16)


def _split_dot(a, ones_mat):
    hi = a.astype(BF16)
    lo = (a - hi.astype(F32)).astype(BF16)
    return _dot(hi, ones_mat) + _dot(lo, ones_mat)


def _ada_kernel(c_ref, w_ref, b_ref, o_ref):
    c = c_ref[...]
    sc = (c * _sigmoid(c)).astype(BF16)
    o_ref[0] = _dot(sc, w_ref[0].astype(BF16)) + b_ref[0]


def _ada_call(cvec, w_ada, b_ada):
    depth, d, n = w_ada.shape
    rows = cvec.shape[0]
    tn = 1536
    return pl.pallas_call(
        _ada_kernel,
        grid=(depth, n // tn),
        in_specs=[
            pl.BlockSpec((rows, d), lambda l, j: (0, 0)),
            pl.BlockSpec((1, d, tn), lambda l, j: (l, 0, j)),
            pl.BlockSpec((1, 1, tn), lambda l, j: (l, 0, j)),
        ],
        out_specs=pl.BlockSpec((1, rows, tn), lambda l, j: (l, 0, j)),
        out_shape=jax.ShapeDtypeStruct((depth, rows, n), F32),
        compiler_params=_cparams(2),
        name="ada_mod",
    )(cvec, w_ada, b_ada.reshape(depth, 1, n))


def _rms_modulate(x, gain, shift, scale):
    ms = jnp.mean(x * x, axis=-1, keepdims=True)
    y = x * lax.rsqrt(ms + NORM_EPS) * gain
    return y * (1.0 + scale) + shift


def _swap32(x):
    lane = lax.broadcasted_iota(jnp.int32, x.shape, 1)
    up = pltpu.roll(x, 96, 1)
    down = pltpu.roll(x, 32, 1)
    return jnp.where((lane % HEAD_DIM) < ROPE_AXIS_DIM, up, down)


def _head_norm_rope(t, gain, cos, sin, use_rope):
    ss = _dot((t * t).astype(BF16), _group_ones(128, HEAD_DIM))
    tn = t * lax.rsqrt(ss * (1.0 / HEAD_DIM) + NORM_EPS) * gain
    if use_rope:
        tn = tn * cos + _swap32(tn) * sin
    return tn


def _inproj_kernel(x_ref, mod_ref, gain_ref, w_ref, qg_ref, kg_ref, bg_ref, cos_ref, sin_ref,
                   qt_out, k_out, vt_out, mqk_out, mv_out, mo_out, g_out, *, use_rope):
    x = x_ref[...]
    mod = mod_ref[...]
    h = _rms_modulate(x, gain_ref[...], mod[0:1], mod[1:2]).astype(BF16)
    tm = x.shape[0]

    mqk_out[...] = _dot(h, w_ref[:, C_MQK:C_MQK + 2 * ML_WIDTH])
    mv_out[...] = _dot(h, w_ref[:, C_MV:C_MV + ML_WIDTH]).astype(BF16)
    mo_out[...] = _sigmoid(_dot(h, w_ref[:, C_MO:C_MO + ML_WIDTH])).astype(BF16)

    vt = _dot(h, w_ref[:, C_V:C_V + KV_WIDTH]).T
    pad_row = lax.broadcasted_iota(jnp.int32, (HEAD_DIM, tm), 0)
    pad = jnp.where(pad_row == 0, 1.0, 0.0)
    for kvh in range(KV_HEADS):
        vt_out[128 * kvh:128 * kvh + HEAD_DIM, :] = vt[HEAD_DIM * kvh:HEAD_DIM * (kvh + 1)].astype(BF16)
        vt_out[128 * kvh + HEAD_DIM:128 * (kvh + 1), :] = pad.astype(BF16)

    cos = cos_ref[...] if use_rope else None
    sin = sin_ref[...] if use_rope else None

    q = _dot(h, w_ref[:, C_Q:C_Q + ATTN_WIDTH])
    qscale = HEAD_DIM ** -0.5 * LOG2_E
    for j in range(ATTN_WIDTH // 128):
        blk = _head_norm_rope(q[:, 128 * j:128 * (j + 1)], qg_ref[...], cos, sin, use_rope)
        qt_out[128 * j:128 * (j + 1), :] = (blk * qscale).T.astype(BF16)

    k = _dot(h, w_ref[:, C_K:C_K + KV_WIDTH])
    k_out[...] = _head_norm_rope(k, kg_ref[...], cos, sin, use_rope).astype(BF16)

    g = _dot(h, w_ref[:, C_G:C_G + 128])
    gt = g.T[0:ML_HEADS * GATE_ROWS, :] + bg_ref[...]
    row = lax.broadcasted_iota(jnp.int32, gt.shape, 0) % GATE_ROWS
    gt = jnp.where((row == 1) | (row == 3), _log_sigmoid(gt), gt)
    for j in range(tm // CHUNK):
        g_out[j] = gt[:, CHUNK * j:CHUNK * (j + 1)]


def _inproj_call(x, mod, mod_row, gain, w_all, layer, qg, kg, bg, cos, sin, *, use_rope, tm):
    bsz, t, d = x.shape
    nt = t // tm
    nc = t // CHUNK
    cpt = tm // CHUNK
    if mod_row is None:
        mod_map = lambda b, i: (b, 0, 0)
    else:
        mod_map = lambda b, i: (mod_row, 0, 0)
    tok = lambda w: pl.BlockSpec((None, tm, w), lambda b, i: (b, i, 0))
    out_shapes = (
        jax.ShapeDtypeStruct((bsz, ATTN_WIDTH, t), BF16),
        jax.ShapeDtypeStruct((bsz, t, KV_WIDTH), BF16),
        jax.ShapeDtypeStruct((bsz, KV_HEADS * 128, t), BF16),
        jax.ShapeDtypeStruct((bsz, t, 2 * ML_WIDTH), F32),
        jax.ShapeDtypeStruct((bsz, t, ML_WIDTH), BF16),
        jax.ShapeDtypeStruct((bsz, t, ML_WIDTH), BF16),
        jax.ShapeDtypeStruct((bsz, nc, ML_HEADS * GATE_ROWS, CHUNK), F32),
    )
    out_specs = (
        pl.BlockSpec((None, ATTN_WIDTH, tm), lambda b, i: (b, 0, i)),
        tok(KV_WIDTH),
        pl.BlockSpec((None, KV_HEADS * 128, tm), lambda b, i: (b, 0, i)),
        tok(2 * ML_WIDTH),
        tok(ML_WIDTH),
        tok(ML_WIDTH),
        pl.BlockSpec((None, cpt, ML_HEADS * GATE_ROWS, CHUNK), lambda b, i: (b, i, 0, 0)),
    )
    in_specs = [
        tok(d),
        pl.BlockSpec((None, N_MOD, d), mod_map),
        _const_spec((None, 1, d), lambda b, i: (layer, 0, 0)),
        _const_spec((None, d, W_COLS), lambda b, i: (layer, 0, 0)),
        _const_spec((None, 1, 128), lambda b, i: (layer, 0, 0)),
        _const_spec((None, 1, 128), lambda b, i: (layer, 0, 0)),
        _const_spec((None, ML_HEADS * GATE_ROWS, 1), lambda b, i: (layer, 0, 0)),
        pl.BlockSpec((tm, 128), lambda b, i: (i, 0)),
        pl.BlockSpec((tm, 128), lambda b, i: (i, 0)),
    ]
    return pl.pallas_call(
        functools.partial(_inproj_kernel, use_rope=use_rope),
        grid=(bsz, nt),
        in_specs=in_specs,
        out_specs=out_specs,
        out_shape=out_shapes,
        compiler_params=_cparams(2),
        name="inproj_rope" if use_rope else "inproj_ctx",
    )(x, mod, gain, w_all, qg, kg, bg, cos, sin)


def _attn_kernel(*refs, n_seg, q_block):
    qt_ref = refs[0]
    k_refs = refs[1:1 + n_seg]
    vt_refs = refs[1 + n_seg:1 + 2 * n_seg]
    o_ref = refs[1 + 2 * n_seg]
    k_all, vt_all, st_ref = refs[2 + 2 * n_seg:]
    s_tot = k_all.shape[0]
    tq = qt_ref.shape[1]
    cols = ATTN_GROUP * q_block

    @pl.when(pl.program_id(1) == 0)
    def _():
        base = 0
        for seg in range(n_seg):
            s_len = k_refs[seg].shape[0]
            k_all[base:base + s_len, :] = k_refs[seg][...]
            vt_all[:, base:base + s_len] = vt_refs[seg][...]
            base += s_len

    groups = [(qb, kvh) for qb in range(tq // q_block) for kvh in range(KV_HEADS)]
    zeros = jnp.zeros((HEAD_DIM, q_block), BF16)

    def padded_queries(qb, kvh):
        pieces = []
        for g in range(ATTN_GROUP):
            row0 = HEAD_DIM * (ATTN_GROUP * kvh + g)
            qg = qt_ref[row0:row0 + HEAD_DIM, q_block * qb:q_block * (qb + 1)]
            pieces.append(jnp.concatenate([qg, zeros] if kvh == 0 else [zeros, qg], axis=0))
        return jnp.concatenate(pieces, axis=1)

    def finalize(qb, kvh, acc):
        ot = acc[0:HEAD_DIM] * (1.0 / acc[HEAD_DIM:HEAD_DIM + 1])
        for pair in range(ATTN_GROUP // 2):
            two = jnp.concatenate([ot[:, q_block * (2 * pair):q_block * (2 * pair + 1)],
                                   ot[:, q_block * (2 * pair + 1):q_block * (2 * pair + 2)]], axis=0)
            lane0 = HEAD_DIM * (ATTN_GROUP * kvh + 2 * pair)
            o_ref[q_block * qb:q_block * (qb + 1), lane0:lane0 + 2 * HEAD_DIM] = two.T.astype(BF16)

    m_prev = None
    for j in range(len(groups) + 1):
        qz = padded_queries(*groups[j]) if j < len(groups) else None
        m_next = None
        if qz is not None:
            st = _dot(k_all[...], qz)
            st_ref[j % 2] = st
            m_next = jnp.max(jnp.max(st.reshape(s_tot // 8, 8, cols), axis=0), axis=0, keepdims=True)
        if j >= 1:
            qb, kvh = groups[j - 1]
            p = jnp.exp2(st_ref[(j - 1) % 2] - m_prev).astype(BF16)
            finalize(qb, kvh, _dot(vt_all[128 * kvh:128 * kvh + V_ROWS, :], p))
        m_prev = m_next


def _attn_call(qt, ks, vts, *, tq, q_block=128):
    bsz, _, t = qt.shape
    n_seg = len(ks)
    s_tot = sum(k.shape[1] for k in ks)
    q_block = min(q_block, tq)
    cols = ATTN_GROUP * q_block
    in_specs = [pl.BlockSpec((None, ATTN_WIDTH, tq), lambda b, i: (b, 0, i))]
    for k in ks:
        in_specs.append(pl.BlockSpec((None, k.shape[1], KV_WIDTH), lambda b, i: (b, 0, 0)))
    for vt in vts:
        in_specs.append(pl.BlockSpec((None, KV_HEADS * 128, vt.shape[2]), lambda b, i: (b, 0, 0)))
    return pl.pallas_call(
        functools.partial(_attn_kernel, n_seg=n_seg, q_block=q_block),
        grid=(bsz, t // tq),
        in_specs=in_specs,
        out_specs=pl.BlockSpec((None, tq, ATTN_WIDTH), lambda b, i: (b, i, 0)),
        out_shape=jax.ShapeDtypeStruct((bsz, t, ATTN_WIDTH), BF16),
        scratch_shapes=[
            pltpu.VMEM((s_tot, KV_WIDTH), BF16),
            pltpu.VMEM((KV_HEADS * 128, s_tot), BF16),
            pltpu.VMEM((2, s_tot, cols), F32),
        ],
        compiler_params=_cparams(2),
        name="attention_%dseg" % n_seg,
    )(qt, *ks, *vts)


def _mlstm_kernel(*refs, emit_ctx):
    (mq_c, mk_c, mv_c, mo_c, g_c, mq_l, mk_l, mv_l, mo_l, g_l, cw_q, cw_k, gain_ref) = refs[:13]
    if emit_ctx:
        out_c, out_l = refs[13:15]
        scratch = refs[15:]
    else:
        out_c = None
        out_l = refs[13]
        scratch = refs[14:]
    q_s, kt_s, cum_s, x_s, u_s, pm_s, bc_s, sc_s, hf_s, ct_s, m_s = scratch

    L = CHUNK
    segs = []
    chunk_base = 0
    for (mq, mk, mv, mo, g, out, emit) in ((mq_c, mk_c, mv_c, mo_c, g_c, out_c, emit_ctx),
                                           (mq_l, mk_l, mv_l, mo_l, g_l, out_l, True)):
        n = mq.shape[0] // L
        segs.append(dict(mq=mq, mk=mk, mv=mv, mo=mo, g=g, out=out, emit=emit, n=n, cb=chunk_base))
        chunk_base += n

    ri = lax.broadcasted_iota(jnp.int32, (L, L), 0)
    ci = lax.broadcasted_iota(jnp.int32, (L, L), 1)
    ones_le = jnp.where(ri <= ci, 1.0, 0.0).astype(BF16)
    ones_ge = jnp.where(ri >= ci, 1.0, 0.0).astype(BF16)
    masks = (ci <= ri, ci >= ri)
    row_i = lax.broadcasted_iota(jnp.int32, (L, ML_DIM), 0)
    ones_blk = jnp.ones((L, 128), BF16)
    kscale = ML_DIM ** -0.5

    def unroll_of(n, want):
        while n % want:
            want //= 2
        return want

    for sg in segs:
        n, cb = sg["n"], sg["cb"]
        t_seg = n * L

        def conv_silu(src, cw, c, t_seg=t_seg, n=n):
            t0 = pl.multiple_of(c * L, L)
            xc = src[pl.ds(t0, L), :]
            prev = src[pl.ds(jnp.maximum(t0 - 1, 0), 1), :] * jnp.where(c > 0, 1.0, 0.0)
            nxt = src[pl.ds(jnp.minimum(t0 + L, t_seg - 1), 1), :] * jnp.where(c < n - 1, 1.0, 0.0)
            xm = jnp.where(row_i == 0, prev, pltpu.roll(xc, 1, 0))
            xp = jnp.where(row_i == L - 1, nxt, pltpu.roll(xc, L - 1, 0))
            y = xm * cw[0:1, :] + xc * cw[1:2, :] + xp * cw[2:3, :]
            return y * _sigmoid(y)

        def prep(c, carry, sg=sg, cb=cb, conv_silu=conv_silu):
            t0g = pl.multiple_of((cb + c) * L, L)
            q_s[pl.ds(t0g, L), :] = conv_silu(sg["mq"], cw_q, c).astype(BF16)
            kt_s[cb + c] = (conv_silu(sg["mk"], cw_k, c) * kscale).T.astype(BF16)
            return carry

        lax.fori_loop(0, n, prep, 0, unroll=unroll_of(n, 2))
        g2 = sg["g"][...].reshape(n * GATE_ROWS, L)
        cum_s[0, cb:cb + n] = _split_dot(g2, ones_le).reshape(n, GATE_ROWS, L)
        cum_s[1, cb:cb + n] = _split_dot(g2, ones_ge).reshape(n, GATE_ROWS, L)

    def gate_rows(d, sg, c):
        gr = sg["g"][c]
        cum = cum_s[d, sg["cb"] + c]
        if d == 0:
            return gr[0:1], gr[1:2], cum[1:2], cum[1:2, L - 1:L]
        return gr[2:3], gr[3:4], cum[3:4], cum[3:4, 0:1]

    def state_free_pass(d):
        for sg in segs:
            group = unroll_of(sg["n"], 4)

            def body(i, carry, sg=sg, group=group):
                pending = []
                for j in range(group):
                    c = i * group + j
                    cg = sg["cb"] + c
                    t0 = pl.multiple_of(c * L, L)
                    t0g = pl.multiple_of(cg * L, L)
                    ig, lf, brow, total = gate_rows(d, sg, c)
                    ktc = kt_s[cg]
                    v_aug = jnp.concatenate([sg["mv"][pl.ds(t0, L), :], ones_blk], axis=1)
                    a = ig - brow
                    amax = jnp.max(a, axis=-1, keepdims=True)
                    kw = (ktc.astype(F32) * jnp.exp(a - amax)).astype(BF16)
                    u_s[cg] = _dot(kw, v_aug)
                    sc_s[cg] = jnp.concatenate(
                        [jnp.broadcast_to(amax, (1, L)), jnp.broadcast_to(total, (1, L)),
                         jnp.zeros((GATE_ROWS - 2, L), F32)], axis=0)
                    if sg["emit"]:
                        a_vis = jnp.where(masks[d], a, -jnp.inf)
                        pm = jnp.broadcast_to(jnp.max(a_vis, axis=-1, keepdims=True), (L, L))
                        bc = jnp.broadcast_to(
                            jnp.sum(jnp.where(masks[d], lf, 0.0), axis=-1, keepdims=True), (L, L))
                        pm_s[cg] = pm
                        bc_s[cg] = bc
                        s0 = _dot(q_s[pl.ds(t0g, L), :], ktc)
                        pending.append((cg, s0, jnp.exp(a_vis - pm), v_aug))
                for cg, s0, e, v_aug in pending:
                    x_s[cg] = _dot((s0 * e).astype(BF16), v_aug)
                return carry
            lax.fori_loop(0, sg["n"] // group, body, 0)

    def sequential_pass(d):
        ct_s[...] = jnp.zeros(ct_s.shape, F32)
        m_s[...] = jnp.zeros(m_s.shape, F32)
        for sg in segs:
            n = sg["n"]
            group = unroll_of(n, 4)

            def body(i, carry, sg=sg, n=n, group=group):
                m_prev = m_s[0:1, 0:1]
                ct = ct_s[...]
                pending = []
                for j in range(group):
                    c = i * group + j if d == 0 else n - 1 - (i * group + j)
                    cg = sg["cb"] + c
                    t0 = pl.multiple_of(c * L, L)
                    t0g = pl.multiple_of(cg * L, L)
                    sc = sc_s[cg]
                    amax, total = sc[0:1, 0:1], sc[1:2, 0:1]
                    m_last = jnp.maximum(m_prev, amax)
                    if sg["emit"]:
                        inter = _dot(q_s[pl.ds(t0g, L), :], ct.astype(BF16))
                        pending.append((cg, t0, t0g, inter, m_prev))
                    ct = jnp.exp(m_prev - m_last) * ct + jnp.exp(amax - m_last) * u_s[cg]
                    m_prev = total + m_last
                ct_s[...] = ct
                m_s[...] = jnp.broadcast_to(m_prev, m_s.shape)
                for cg, t0, t0g, inter, m_in in pending:
                    pm = pm_s[cg]
                    mx = jnp.maximum(pm, m_in)
                    r = jnp.exp(pm - mx)
                    wi = jnp.exp(m_in - mx)
                    x = x_s[cg]
                    num = r * x[:, 0:ML_DIM] + wi * inter[:, 0:ML_DIM]
                    den = r * x[:, ML_DIM:] + wi * inter[:, ML_DIM:]
                    h = num / jnp.maximum(jnp.abs(den), jnp.exp(-bc_s[cg] - mx))
                    if d == 0:
                        hf_s[pl.ds(t0g, L), :] = h
                    else:
                        hsum = hf_s[pl.ds(t0g, L), :] + h
                        ms = jnp.mean(hsum * hsum, axis=-1, keepdims=True)
                        hn = hsum * lax.rsqrt(ms + NORM_EPS) * gain_ref[...]
                        sg["out"][pl.ds(t0, L), :] = (sg["mo"][pl.ds(t0, L), :].astype(F32) * hn).astype(BF16)
                return carry

            lax.fori_loop(0, n // group, body, 0)

    for d in range(2):
        state_free_pass(d)
        sequential_pass(d)


def _mlstm_call(ctx_p, lat_p, conv_w, ml_gain, layer, *, emit_ctx):
    bsz = lat_p[0].shape[0]
    in_specs = []
    args = []
    t_tot = 0
    for (mqk, mv, mo, g) in (ctx_p, lat_p):
        t = mqk.shape[1]
        n = t // CHUNK
        t_tot += t
        in_specs += [
            pl.BlockSpec((None, t, ML_DIM), lambda b, h: (b, 0, h)),
            pl.BlockSpec((None, t, ML_DIM), lambda b, h: (b, 0, ML_HEADS + h)),
            pl.BlockSpec((None, t, ML_DIM), lambda b, h: (b, 0, h)),
            pl.BlockSpec((None, t, ML_DIM), lambda b, h: (b, 0, h)),
            pl.BlockSpec((None, n, GATE_ROWS, CHUNK), lambda b, h: (b, 0, h, 0)),
        ]
        args += [mqk, mqk, mv, mo, g]
    in_specs += [
        pl.BlockSpec((None, CONV_WIDTH, ML_DIM), lambda b, h: (layer, 0, h)),
        pl.BlockSpec((None, CONV_WIDTH, ML_DIM), lambda b, h: (layer, 0, ML_HEADS + h)),
        pl.BlockSpec((None, 1, ML_DIM), lambda b, h: (layer, 0, h)),
    ]
    args += [conv_w, conv_w, ml_gain]
    t_c, t_l = ctx_p[0].shape[1], lat_p[0].shape[1]
    out_l_shape = jax.ShapeDtypeStruct((bsz, t_l, ML_WIDTH), BF16)
    out_l_spec = pl.BlockSpec((None, t_l, ML_DIM), lambda b, h: (b, 0, h))
    if emit_ctx:
        out_shape = (jax.ShapeDtypeStruct((bsz, t_c, ML_WIDTH), BF16), out_l_shape)
        out_specs = (pl.BlockSpec((None, t_c, ML_DIM), lambda b, h: (b, 0, h)), out_l_spec)
    else:
        out_shape = out_l_shape
        out_specs = out_l_spec
    n_tot = t_tot // CHUNK
    res = pl.pallas_call(
        functools.partial(_mlstm_kernel, emit_ctx=emit_ctx),
        grid=(bsz, ML_HEADS),
        in_specs=in_specs,
        out_specs=out_specs,
        out_shape=out_shape,
        scratch_shapes=[
            pltpu.VMEM((t_tot, ML_DIM), BF16),
            pltpu.VMEM((n_tot, ML_DIM, CHUNK), BF16),
            pltpu.VMEM((2, n_tot, GATE_ROWS, CHUNK), F32),
            pltpu.VMEM((n_tot, CHUNK, 2 * ML_DIM), F32),
            pltpu.VMEM((n_tot, ML_DIM, 2 * ML_DIM), F32),
            pltpu.VMEM((n_tot, CHUNK, 128), F32),
            pltpu.VMEM((n_tot, CHUNK, 128), F32),
            pltpu.VMEM((n_tot, GATE_ROWS, CHUNK), F32),
            pltpu.VMEM((t_tot, ML_DIM), F32),
            pltpu.VMEM((ML_DIM, 2 * ML_DIM), F32),
            pltpu.VMEM((8, 128), F32),
        ],
        compiler_params=_cparams(2),
        name="mlstm_emit" if emit_ctx else "mlstm_last",
    )(*args)
    if emit_ctx:
        return res
    return None, res


def _outmlp_kernel(*refs, final, ffn_chunk):
    x_ref, att_ref, mem_ref, mod_ref, wo_ref, gain_ref, w1_ref, w2_ref = refs[:8]
    if final:
        gf_ref, o_ref = refs[8:]
    else:
        o_ref = refs[8]
    mod = mod_ref[...]
    y = _dot(att_ref[...], wo_ref[0:ATTN_WIDTH, :]) + _dot(mem_ref[...], wo_ref[ATTN_WIDTH:D_MODEL, :])
    x1 = x_ref[...] + mod[2:3] * y
    h = _rms_modulate(x1, gain_ref[...], mod[3:4], mod[4:5]).astype(BF16)
    acc = jnp.zeros(x1.shape, F32)
    for j in range(FFN_DIM // ffn_chunk):
        f = jnp.maximum(_dot(h, w1_ref[:, ffn_chunk * j:ffn_chunk * (j + 1)]), 0.0)
        acc = acc + _dot((f * f).astype(BF16), w2_ref[ffn_chunk * j:ffn_chunk * (j + 1), :])
    x2 = x1 + mod[5:6] * acc
    if final:
        ms = jnp.mean(x2 * x2, axis=-1, keepdims=True)
        x2 = x2 * lax.rsqrt(ms + NORM_EPS) * gf_ref[...]
    o_ref[...] = x2


def _outmlp_call(x, att, mem, mod, mod_row, wo, gain, w1, w2, layer, gf, *, tm, ffn_chunk=1024):
    bsz, t, d = x.shape
    final = gf is not None
    if mod_row is None:
        mod_map = lambda b, i: (b, 0, 0)
    else:
        mod_map = lambda b, i: (mod_row, 0, 0)
    in_specs = [
        pl.BlockSpec((None, tm, d), lambda b, i: (b, i, 0)),
        pl.BlockSpec((None, tm, ATTN_WIDTH), lambda b, i: (b, i, 0)),
        pl.BlockSpec((None, tm, ML_WIDTH), lambda b, i: (b, i, 0)),
        pl.BlockSpec((None, N_MOD, d), mod_map),
        _const_spec((None, d, d), lambda b, i: (layer, 0, 0)),
        _const_spec((None, 1, d), lambda b, i: (layer, 0, 0)),
        _const_spec((None, d, FFN_DIM), lambda b, i: (layer, 0, 0)),
        _const_spec((None, FFN_DIM, d), lambda b, i: (layer, 0, 0)),
    ]
    args = [x, att, mem, mod, wo, gain, w1, w2]
    if final:
        in_specs.append(_const_spec((1, d), lambda b, i: (0, 0)))
        args.append(gf)
    return pl.pallas_call(
        functools.partial(_outmlp_kernel, final=final, ffn_chunk=ffn_chunk),
        grid=(bsz, t // tm),
        in_specs=in_specs,
        out_specs=pl.BlockSpec((None, tm, d), lambda b, i: (b, i, 0)),
        out_shape=jax.ShapeDtypeStruct((bsz, t, d), F32),
        compiler_params=_cparams(2),
        name="outproj_mlp_final" if final else "outproj_mlp",
    )(*args)


def _projection_columns():
    a_q, a_k, a_v = 0, ATTN_WIDTH, ATTN_WIDTH + KV_WIDTH
    m_qk = ATTN_WIDTH + 2 * KV_WIDTH
    m_v = m_qk + 2 * ML_WIDTH
    m_o = m_v + ML_WIDTH
    gates = m_o + ML_WIDTH
    deint = np.concatenate([np.arange(0, HEAD_DIM, 2), np.arange(1, HEAD_DIM, 2)])
    idx = np.full((W_COLS,), -1, np.int64)
    idx[C_MQK:C_MQK + 2 * ML_WIDTH] = m_qk + np.arange(2 * ML_WIDTH)
    idx[C_Q:C_Q + ATTN_WIDTH] = a_q + (HEAD_DIM * np.arange(ATTN_HEADS)[:, None] + deint[None, :]).reshape(-1)
    idx[C_MV:C_MV + ML_WIDTH] = m_v + np.arange(ML_WIDTH)
    idx[C_MO:C_MO + ML_WIDTH] = m_o + np.arange(ML_WIDTH)
    idx[C_K:C_K + KV_WIDTH] = a_k + (HEAD_DIM * np.arange(KV_HEADS)[:, None] + deint[None, :]).reshape(-1)
    idx[C_V:C_V + KV_WIDTH] = a_v + np.arange(KV_WIDTH)
    for h in range(ML_HEADS):
        for kind in range(4):
            idx[C_G + GATE_ROWS * h + kind] = gates + kind * ML_HEADS + h
    return idx, deint


def _rope_tables(t):
    rows = t // GRID_W
    row_idx = jnp.repeat(jnp.arange(rows, dtype=F32), GRID_W)
    col_idx = jnp.tile(jnp.arange(GRID_W, dtype=F32), rows)
    inv_freq = jnp.power(ROPE_THETA, -jnp.arange(0, ROPE_AXIS_DIM, 2, dtype=F32) / ROPE_AXIS_DIM)
    ang = jnp.concatenate([row_idx[:, None] * inv_freq, col_idx[:, None] * inv_freq], axis=-1)
    cos, sin = jnp.cos(ang), jnp.sin(ang)
    cos128 = jnp.tile(cos, (1, 4))
    sin128 = jnp.tile(jnp.concatenate([-sin, sin], axis=-1), (1, 2))
    return cos128, sin128


def _pick_tile(t, pref):
    tm = min(pref, t)
    while t % tm:
        tm //= 2
    return tm


def kernel(x, c, ctx, c_ctx, w_ada, b_ada, norm_mix, norm_mlp, w_in, b_gates, conv_qk,
           q_norm, k_norm, mlstm_norm, w_out, w_mlp_in, w_mlp_out, norm_final):
    bsz, t, d = x.shape
    t_c = ctx.shape[1]
    depth = w_ada.shape[0]
    assert d == D_MODEL and t % CHUNK == 0 and t_c % CHUNK == 0 and t % GRID_W == 0

    rows = -(-(bsz + 1) // 8) * 8
    cvec = jnp.zeros((rows, d), F32).at[:bsz].set(c).at[bsz].set(c_ctx)
    mod = _ada_call(cvec, w_ada, b_ada).reshape(depth, rows, N_MOD, d)

    idx, deint = _projection_columns()
    w_all = jnp.where(idx[None, None, :] >= 0, jnp.take(w_in.astype(BF16), np.maximum(idx, 0), axis=2),
                      jnp.zeros((), BF16))
    qg = jnp.tile(q_norm[:, deint], (1, 2)).reshape(depth, 1, 128)
    kg = jnp.tile(k_norm[:, deint], (1, 2)).reshape(depth, 1, 128)
    norm_mix = norm_mix.reshape(depth, 1, d)
    norm_mlp = norm_mlp.reshape(depth, 1, d)
    mlstm_norm = mlstm_norm.reshape(depth, 1, ML_WIDTH)
    bg = jnp.zeros((depth, ML_HEADS, GATE_ROWS), F32).at[:, :, 0:4].set(
        b_gates.reshape(depth, 4, ML_HEADS).transpose(0, 2, 1)).reshape(depth, ML_HEADS * GATE_ROWS, 1)
    wo = w_out.astype(BF16)
    w1 = w_mlp_in.astype(BF16)
    w2 = w_mlp_out.astype(BF16)
    cos128, sin128 = _rope_tables(t)
    gf = norm_final.reshape(1, d)

    tm_l = _pick_tile(t, 512)
    tm_c = _pick_tile(t_c, 512)
    tq_l = _pick_tile(t, 512)
    tq_c = _pick_tile(t_c, 256)

    for layer in range(depth):
        emit_ctx = layer < depth - 1
        mod_l = mod[layer]
        qt_l, k_l, vt_l, mqk_l, mv_l, mo_l, g_l = _inproj_call(
            x, mod_l, None, norm_mix, w_all, layer, qg, kg, bg, cos128, sin128, use_rope=True, tm=tm_l)
        qt_c, k_c, vt_c, mqk_c, mv_c, mo_c, g_c = _inproj_call(
            ctx, mod_l, bsz, norm_mix, w_all, layer, qg, kg, bg, cos128, sin128, use_rope=False, tm=tm_c)
        att_l = _attn_call(qt_l, [k_c, k_l], [vt_c, vt_l], tq=tq_l)
        mem_c, mem_l = _mlstm_call((mqk_c, mv_c, mo_c, g_c), (mqk_l, mv_l, mo_l, g_l),
                                   conv_qk, mlstm_norm, layer, emit_ctx=emit_ctx)
        x = _outmlp_call(x, att_l, mem_l, mod_l, None, wo, norm_mlp, w1, w2, layer,
                         None if emit_ctx else gf, tm=tm_l)
        if emit_ctx:
            att_c = _attn_call(qt_c, [k_c], [vt_c], tq=tq_c)
            ctx = _outmlp_call(ctx, att_c, mem_c, mod_l, bsz, wo, norm_mlp, w1, w2, layer, None, tm=tm_c)
    return x
```

```python
import functools

import numpy as np
import jax
import jax.numpy as jnp
from jax import lax
from jax.experimental import pallas as pl
from jax.experimental.pallas import tpu as pltpu

D_MODEL = 1024
N_MOD = 6
GRID_W = 64
ATTN_WIDTH = 512
ATTN_HEADS = 8
HEAD_DIM = 64
KV_HEADS = 2
ATTN_GROUP = ATTN_HEADS // KV_HEADS
KV_WIDTH = KV_HEADS * HEAD_DIM
ROPE_THETA = 10000.0
ROPE_AXIS_DIM = HEAD_DIM // 2
ML_WIDTH = 512
ML_HEADS = 4
ML_DIM = 128
CHUNK = 128
CONV_WIDTH = 3
N_GATES = 4 * ML_HEADS
GATE_ROWS = 8
FFN_DIM = 4 * D_MODEL
NORM_EPS = 1e-6
LOG2_E = 1.4426950408889634

C_MQK = 0
C_Q = C_MQK + 2 * ML_WIDTH
C_MV = C_Q + ATTN_WIDTH
C_MO = C_MV + ML_WIDTH
C_K = C_MO + ML_WIDTH
C_V = C_K + KV_WIDTH
C_G = C_V + KV_WIDTH
W_COLS = C_G + 128

VMEM_LIMIT_BYTES = 56 * 1024 * 1024

BF16 = jnp.bfloat16
F32 = jnp.float32


def _cparams(n_grid, flags=None):
    return pltpu.CompilerParams(
        dimension_semantics=("arbitrary",) * n_grid,
        vmem_limit_bytes=VMEM_LIMIT_BYTES,
        flags=flags)


def _const_spec(shape, index_map):
    return pl.BlockSpec(shape, index_map, pipeline_mode=pl.Buffered(1))


def _aligned(i, m):
    return i if isinstance(i, int) else pl.multiple_of(i, m)


def _dot(a, b):
    return jnp.dot(a, b, preferred_element_type=F32)


def _sigmoid(x):
    return 1.0 / (1.0 + jnp.exp(-x))


def _log_sigmoid(x):
    return jnp.minimum(x, 0.0) - jnp.log(1.0 + jnp.exp(-jnp.abs(x)))


def _group_ones(n, group):
    r = lax.broadcasted_iota(jnp.int32, (n, n), 0) // group
    c = lax.broadcasted_iota(jnp.int32, (n, n), 1) // group
    return jnp.where(r == c, 1.0, 0.0).astype(BF16)


def _split_dot(a, ones_mat):
    hi = a.astype(BF16)
    lo = (a - hi.astype(F32)).astype(BF16)
    return _dot(hi, ones_mat) + _dot(lo, ones_mat)


def _ada_kernel(c_ref, w_ref, b_ref, o_ref):
    c = c_ref[...]
    sc = (c * _sigmoid(c)).astype(BF16)
    o_ref[0] = _dot(sc, w_ref[0].astype(BF16)) + b_ref[0]


def _ada_call(cvec, w_ada, b_ada):
    depth, d, n = w_ada.shape
    rows = cvec.shape[0]
    tn = 1536
    return pl.pallas_call(
        _ada_kernel,
        grid=(depth, n // tn),
        in_specs=[
            pl.BlockSpec((rows, d), lambda l, j: (0, 0)),
            pl.BlockSpec((1, d, tn), lambda l, j: (l, 0, j)),
            pl.BlockSpec((1, 1, tn), lambda l, j: (l, 0, j)),
        ],
        out_specs=pl.BlockSpec((1, rows, tn), lambda l, j: (l, 0, j)),
        out_shape=jax.ShapeDtypeStruct((depth, rows, n), F32),
        compiler_params=_cparams(2),
        name="ada_mod",
    )(cvec, w_ada, b_ada.reshape(depth, 1, n))


def _rms_modulate(x, gain, shift, scale):
    ms = jnp.mean(x * x, axis=-1, keepdims=True)
    y = x * lax.rsqrt(ms + NORM_EPS) * gain
    return y * (1.0 + scale) + shift


def _swap32(x):
    lane = lax.broadcasted_iota(jnp.int32, x.shape, 1)
    up = pltpu.roll(x, 96, 1)
    down = pltpu.roll(x, 32, 1)
    return jnp.where((lane % HEAD_DIM) < ROPE_AXIS_DIM, up, down)


def _head_norm_rope(t, gain, cos, sin, use_rope):
    ss = _dot((t * t).astype(BF16), _group_ones(128, HEAD_DIM))
    tn = t * lax.rsqrt(ss * (1.0 / HEAD_DIM) + NORM_EPS) * gain
    if use_rope:
        tn = tn * cos + _swap32(tn) * sin
    return tn


def _inproj_kernel(x_ref, mod_ref, gain_ref, w_ref, qg_ref, kg_ref, bg_ref, cos_ref, sin_ref,
                   qt_out, k_out, vt_out, mqk_out, mv_out, mo_out, g_out, *, use_rope):
    x = x_ref[...]
    mod = mod_ref[...]
    h = _rms_modulate(x, gain_ref[...], mod[0:1], mod[1:2]).astype(BF16)
    tm = x.shape[0]

    mqk_out[...] = _dot(h, w_ref[:, C_MQK:C_MQK + 2 * ML_WIDTH])
    mv_out[...] = _dot(h, w_ref[:, C_MV:C_MV + ML_WIDTH]).astype(BF16)
    mo_out[...] = _sigmoid(_dot(h, w_ref[:, C_MO:C_MO + ML_WIDTH])).astype(BF16)

    vt = _dot(h, w_ref[:, C_V:C_V + KV_WIDTH]).T
    pad_row = lax.broadcasted_iota(jnp.int32, (HEAD_DIM, tm), 0)
    pad = jnp.where(pad_row == 0, 1.0, 0.0)
    for kvh in range(KV_HEADS):
        vt_out[128 * kvh:128 * kvh + HEAD_DIM, :] = vt[HEAD_DIM * kvh:HEAD_DIM * (kvh + 1)].astype(BF16)
        vt_out[128 * kvh + HEAD_DIM:128 * (kvh + 1), :] = pad.astype(BF16)

    cos = cos_ref[...] if use_rope else None
    sin = sin_ref[...] if use_rope else None

    q = _dot(h, w_ref[:, C_Q:C_Q + ATTN_WIDTH])
    qscale = HEAD_DIM ** -0.5 * LOG2_E
    for j in range(ATTN_WIDTH // 128):
        blk = _head_norm_rope(q[:, 128 * j:128 * (j + 1)], qg_ref[...], cos, sin, use_rope)
        qt_out[128 * j:128 * (j + 1), :] = (blk * qscale).T.astype(BF16)

    k = _dot(h, w_ref[:, C_K:C_K + KV_WIDTH])
    k_out[...] = _head_norm_rope(k, kg_ref[...], cos, sin, use_rope).astype(BF16)

    g = _dot(h, w_ref[:, C_G:C_G + 128])
    gt = g.T[0:ML_HEADS * GATE_ROWS, :] + bg_ref[...]
    row = lax.broadcasted_iota(jnp.int32, gt.shape, 0) % GATE_ROWS
    gt = jnp.where((row == 1) | (row == 3), _log_sigmoid(gt), gt)
    for j in range(tm // CHUNK):
        g_out[j] = gt[:, CHUNK * j:CHUNK * (j + 1)]


def _inproj_call(x, mod, mod_row, gain, w_all, layer, qg, kg, bg, cos, sin, *, use_rope, tm):
    bsz, t, d = x.shape
    nt = t // tm
    nc = t // CHUNK
    cpt = tm // CHUNK
    if mod_row is None:
        mod_map = lambda b, i: (b, 0, 0)
    else:
        mod_map = lambda b, i: (mod_row, 0, 0)
    tok = lambda w: pl.BlockSpec((None, tm, w), lambda b, i: (b, i, 0))
    out_shapes = (
        jax.ShapeDtypeStruct((bsz, ATTN_WIDTH, t), BF16),
        jax.ShapeDtypeStruct((bsz, t, KV_WIDTH), BF16),
        jax.ShapeDtypeStruct((bsz, KV_HEADS * 128, t), BF16),
        jax.ShapeDtypeStruct((bsz, t, 2 * ML_WIDTH), F32),
        jax.ShapeDtypeStruct((bsz, t, ML_WIDTH), BF16),
        jax.ShapeDtypeStruct((bsz, t, ML_WIDTH), BF16),
        jax.ShapeDtypeStruct((bsz, nc, ML_HEADS * GATE_ROWS, CHUNK), F32),
    )
    out_specs = (
        pl.BlockSpec((None, ATTN_WIDTH, tm), lambda b, i: (b, 0, i)),
        tok(KV_WIDTH),
        pl.BlockSpec((None, KV_HEADS * 128, tm), lambda b, i: (b, 0, i)),
        tok(2 * ML_WIDTH),
        tok(ML_WIDTH),
        tok(ML_WIDTH),
        pl.BlockSpec((None, cpt, ML_HEADS * GATE_ROWS, CHUNK), lambda b, i: (b, i, 0, 0)),
    )
    in_specs = [
        tok(d),
        pl.BlockSpec((None, N_MOD, d), mod_map),
        _const_spec((None, 1, d), lambda b, i: (layer, 0, 0)),
        _const_spec((None, d, W_COLS), lambda b, i: (layer, 0, 0)),
        _const_spec((None, 1, 128), lambda b, i: (layer, 0, 0)),
        _const_spec((None, 1, 128), lambda b, i: (layer, 0, 0)),
        _const_spec((None, ML_HEADS * GATE_ROWS, 1), lambda b, i: (layer, 0, 0)),
        pl.BlockSpec((tm, 128), lambda b, i: (i, 0)),
        pl.BlockSpec((tm, 128), lambda b, i: (i, 0)),
    ]
    return pl.pallas_call(
        functools.partial(_inproj_kernel, use_rope=use_rope),
        grid=(bsz, nt),
        in_specs=in_specs,
        out_specs=out_specs,
        out_shape=out_shapes,
        compiler_params=_cparams(2),
        name="inproj_rope" if use_rope else "inproj_ctx",
    )(x, mod, gain, w_all, qg, kg, bg, cos, sin)


def _attn_kernel(*refs, n_seg, q_block):
    qt_ref = refs[0]
    k_refs = refs[1:1 + n_seg]
    vt_refs = refs[1 + n_seg:1 + 2 * n_seg]
    o_ref = refs[1 + 2 * n_seg]
    k_all, vt_all, st_ref = refs[2 + 2 * n_seg:]
    s_tot = k_all.shape[0]
    tq = qt_ref.shape[1]
    cols = ATTN_GROUP * q_block

    @pl.when(pl.program_id(1) == 0)
    def _():
        base = 0
        for seg in range(n_seg):
            s_len = k_refs[seg].shape[0]
            k_all[base:base + s_len, :] = k_refs[seg][...]
            vt_all[:, base:base + s_len] = vt_refs[seg][...]
            base += s_len

    groups = [(qb, kvh) for qb in range(tq // q_block) for kvh in range(KV_HEADS)]
    zeros = jnp.zeros((HEAD_DIM, q_block), BF16)

    def padded_queries(qb, kvh):
        pieces = []
        for g in range(ATTN_GROUP):
            row0 = HEAD_DIM * (ATTN_GROUP * kvh + g)
            qg = qt_ref[row0:row0 + HEAD_DIM, q_block * qb:q_block * (qb + 1)]
            pieces.append(jnp.concatenate([qg, zeros] if kvh == 0 else [zeros, qg], axis=0))
        return jnp.concatenate(pieces, axis=1)

    def finalize(qb, kvh, acc):
        ot = acc[0:HEAD_DIM] * (1.0 / acc[HEAD_DIM:HEAD_DIM + 1])
        for pair in range(ATTN_GROUP // 2):
            two = jnp.concatenate([ot[:, q_block * (2 * pair):q_block * (2 * pair + 1)],
                                   ot[:, q_block * (2 * pair + 1):q_block * (2 * pair + 2)]], axis=0)
            lane0 = HEAD_DIM * (ATTN_GROUP * kvh + 2 * pair)
            o_ref[q_block * qb:q_block * (qb + 1), lane0:lane0 + 2 * HEAD_DIM] = two.T.astype(BF16)

    m_prev = None
    for j in range(len(groups) + 1):
        qz = padded_queries(*groups[j]) if j < len(groups) else None
        m_next = None
        if qz is not None:
            st = _dot(k_all[...], qz)
            st_ref[j % 2] = st
            m_next = jnp.max(jnp.max(st.reshape(s_tot // 8, 8, cols), axis=0), axis=0, keepdims=True)
        if j >= 1:
            qb, kvh = groups[j - 1]
            p = jnp.exp2(st_ref[(j - 1) % 2] - m_prev).astype(BF16)
            finalize(qb, kvh, _dot(vt_all[128 * kvh:128 * (kvh + 1), :], p))
        m_prev = m_next


def _attn_call(qt, ks, vts, *, tq, q_block=128):
    bsz, _, t = qt.shape
    n_seg = len(ks)
    s_tot = sum(k.shape[1] for k in ks)
    q_block = min(q_block, tq)
    cols = ATTN_GROUP * q_block
    in_specs = [pl.BlockSpec((None, ATTN_WIDTH, tq), lambda b, i: (b, 0, i))]
    for k in ks:
        in_specs.append(pl.BlockSpec((None, k.shape[1], KV_WIDTH), lambda b, i: (b, 0, 0)))
    for vt in vts:
        in_specs.append(pl.BlockSpec((None, KV_HEADS * 128, vt.shape[2]), lambda b, i: (b, 0, 0)))
    return pl.pallas_call(
        functools.partial(_attn_kernel, n_seg=n_seg, q_block=q_block),
        grid=(bsz, t // tq),
        in_specs=in_specs,
        out_specs=pl.BlockSpec((None, tq, ATTN_WIDTH), lambda b, i: (b, i, 0)),
        out_shape=jax.ShapeDtypeStruct((bsz, t, ATTN_WIDTH), BF16),
        scratch_shapes=[
            pltpu.VMEM((s_tot, KV_WIDTH), BF16),
            pltpu.VMEM((KV_HEADS * 128, s_tot), BF16),
            pltpu.VMEM((2, s_tot, cols), F32),
        ],
        compiler_params=_cparams(2),
        name="attention_%dseg" % n_seg,
    )(qt, *ks, *vts)


def _mlstm_kernel(*refs, emit_ctx):
    (mq_c, mk_c, mv_c, mo_c, g_c, mq_l, mk_l, mv_l, mo_l, g_l, cw_q, cw_k, gain_ref) = refs[:13]
    if emit_ctx:
        out_c, out_l = refs[13:15]
        scratch = refs[15:]
    else:
        out_c = None
        out_l = refs[13]
        scratch = refs[14:]
    q_s, kt_s, cum_s, x_s, u_s, pm_s, bc_s, sc_s, hf_s, ct_s, m_s = scratch

    L = CHUNK
    segs = []
    chunk_base = 0
    for (mq, mk, mv, mo, g, out, emit) in ((mq_c, mk_c, mv_c, mo_c, g_c, out_c, emit_ctx),
                                           (mq_l, mk_l, mv_l, mo_l, g_l, out_l, True)):
        n = mq.shape[0] // L
        segs.append(dict(mq=mq, mk=mk, mv=mv, mo=mo, g=g, out=out, emit=emit, n=n, cb=chunk_base))
        chunk_base += n

    ri = lax.broadcasted_iota(jnp.int32, (L, L), 0)
    ci = lax.broadcasted_iota(jnp.int32, (L, L), 1)
    ones_le = jnp.where(ri <= ci, 1.0, 0.0).astype(BF16)
    ones_ge = jnp.where(ri >= ci, 1.0, 0.0).astype(BF16)
    masks = (ci <= ri, ci >= ri)
    row_i = lax.broadcasted_iota(jnp.int32, (L, ML_DIM), 0)
    ones_blk = jnp.ones((L, 128), BF16)
    kscale = ML_DIM ** -0.5

    def unroll_of(n, want):
        while n % want:
            want //= 2
        return want

    for sg in segs:
        n, cb = sg["n"], sg["cb"]
        t_seg = n * L

        def conv_silu(src, cw, c, t_seg=t_seg, n=n):
            t0 = pl.multiple_of(c * L, L)
            xc = src[pl.ds(t0, L), :]
            prev = src[pl.ds(jnp.maximum(t0 - 1, 0), 1), :] * jnp.where(c > 0, 1.0, 0.0)
            nxt = src[pl.ds(jnp.minimum(t0 + L, t_seg - 1), 1), :] * jnp.where(c < n - 1, 1.0, 0.0)
            xm = jnp.where(row_i == 0, prev, pltpu.roll(xc, 1, 0))
            xp = jnp.where(row_i == L - 1, nxt, pltpu.roll(xc, L - 1, 0))
            y = xm * cw[0:1, :] + xc * cw[1:2, :] + xp * cw[2:3, :]
            return y * _sigmoid(y)

        def prep(c, carry, sg=sg, cb=cb, conv_silu=conv_silu):
            t0g = pl.multiple_of((cb + c) * L, L)
            q_s[pl.ds(t0g, L), :] = conv_silu(sg["mq"], cw_q, c).astype(BF16)
            kt_s[cb + c] = (conv_silu(sg["mk"], cw_k, c) * kscale).T.astype(BF16)
            return carry

        lax.fori_loop(0, n, prep, 0, unroll=unroll_of(n, 2))
        g2 = sg["g"][...].reshape(n * GATE_ROWS, L) * LOG2_E
        cum_s[0, cb:cb + n] = _split_dot(g2, ones_le).reshape(n, GATE_ROWS, L)
        cum_s[1, cb:cb + n] = _split_dot(g2, ones_ge).reshape(n, GATE_ROWS, L)

    def gate_rows(d, sg, c):
        gr = sg["g"][c] * LOG2_E
        cum = cum_s[d, sg["cb"] + c]
        if d == 0:
            return gr[0:1], gr[1:2], cum[1:2], cum[1:2, L - 1:L]
        return gr[2:3], gr[3:4], cum[3:4], cum[3:4, 0:1]

    def state_free_pass(d):
        for sg in segs:
            group = unroll_of(sg["n"], 8)

            def body(i, carry, sg=sg, group=group):
                pending = []
                for j in range(group):
                    c = i * group + j
                    cg = sg["cb"] + c
                    t0 = pl.multiple_of(c * L, L)
                    t0g = pl.multiple_of(cg * L, L)
                    ig, lf, brow, total = gate_rows(d, sg, c)
                    ktc = kt_s[cg]
                    v_aug = jnp.concatenate([sg["mv"][pl.ds(t0, L), :], ones_blk], axis=1)
                    a = ig - brow
                    amax = jnp.max(a, axis=-1, keepdims=True)
                    kw = (ktc.astype(F32) * jnp.exp2(a - amax)).astype(BF16)
                    u_s[cg] = _dot(kw, v_aug)
                    sc_s[cg] = jnp.concatenate(
                        [jnp.broadcast_to(amax, (1, L)), jnp.broadcast_to(total, (1, L)),
                         jnp.zeros((GATE_ROWS - 2, L), F32)], axis=0)
                    if sg["emit"]:
                        a_vis = jnp.where(masks[d], a, -jnp.inf)
                        pm = jnp.broadcast_to(jnp.max(a_vis, axis=-1, keepdims=True), (L, L))
                        bc = jnp.broadcast_to(
                            jnp.sum(jnp.where(masks[d], lf, 0.0), axis=-1, keepdims=True), (L, L))
                        pm_s[cg] = pm
                        bc_s[cg] = bc
                        s0 = _dot(q_s[pl.ds(t0g, L), :], ktc)
                        pending.append((cg, s0, jnp.exp2(a_vis - pm), v_aug))
                for cg, s0, e, v_aug in pending:
                    x_s[cg] = _dot((s0 * e).astype(BF16), v_aug)
                return carry
            lax.fori_loop(0, sg["n"] // group, body, 0)

    def sequential_pass(d):
        ct_s[...] = jnp.zeros(ct_s.shape, F32)
        m_s[...] = jnp.zeros(m_s.shape, F32)
        for sg in segs:
            n = sg["n"]
            group = unroll_of(n, 4)

            def body(i, carry, sg=sg, n=n, group=group):
                m_prev = m_s[0:1, 0:1]
                ct = ct_s[...]
                pending = []
                for j in range(group):
                    c = i * group + j if d == 0 else n - 1 - (i * group + j)
                    cg = sg["cb"] + c
                    t0 = pl.multiple_of(c * L, L)
                    t0g = pl.multiple_of(cg * L, L)
                    sc = sc_s[cg]
                    amax, total = sc[0:1, 0:1], sc[1:2, 0:1]
                    m_last = jnp.maximum(m_prev, amax)
                    if sg["emit"]:
                        inter = _dot(q_s[pl.ds(t0g, L), :], ct.astype(BF16))
                        pending.append((cg, t0, t0g, inter, m_prev))
                    ct = jnp.exp2(m_prev - m_last) * ct + jnp.exp2(amax - m_last) * u_s[cg]
                    m_prev = total + m_last
                ct_s[...] = ct
                m_s[...] = jnp.broadcast_to(m_prev, m_s.shape)
                for cg, t0, t0g, inter, m_in in pending:
                    pm = pm_s[cg]
                    mx = jnp.maximum(pm, m_in)
                    r = jnp.exp2(pm - mx)
                    wi = jnp.exp2(m_in - mx)
                    x = x_s[cg]
                    num = r * x[:, 0:ML_DIM] + wi * inter[:, 0:ML_DIM]
                    den = r * x[:, ML_DIM:] + wi * inter[:, ML_DIM:]
                    h = num / jnp.maximum(jnp.abs(den), jnp.exp2(-bc_s[cg] - mx))
                    if d == 0:
                        hf_s[pl.ds(t0g, L), :] = h
                    else:
                        hsum = hf_s[pl.ds(t0g, L), :] + h
                        ms = jnp.mean(hsum * hsum, axis=-1, keepdims=True)
                        hn = hsum * lax.rsqrt(ms + NORM_EPS) * gain_ref[...]
                        sg["out"][pl.ds(t0, L), :] = (sg["mo"][pl.ds(t0, L), :].astype(F32) * hn).astype(BF16)
                return carry

            lax.fori_loop(0, n // group, body, 0)

    for d in range(2):
        state_free_pass(d)
        sequential_pass(d)


def _mlstm_call(ctx_p, lat_p, conv_w, ml_gain, layer, *, emit_ctx):
    bsz = lat_p[0].shape[0]
    in_specs = []
    args = []
    t_tot = 0
    for (mqk, mv, mo, g) in (ctx_p, lat_p):
        t = mqk.shape[1]
        n = t // CHUNK
        t_tot += t
        in_specs += [
            pl.BlockSpec((None, t, ML_DIM), lambda b, h: (b, 0, h)),
            pl.BlockSpec((None, t, ML_DIM), lambda b, h: (b, 0, ML_HEADS + h)),
            pl.BlockSpec((None, t, ML_DIM), lambda b, h: (b, 0, h)),
            pl.BlockSpec((None, t, ML_DIM), lambda b, h: (b, 0, h)),
            pl.BlockSpec((None, n, GATE_ROWS, CHUNK), lambda b, h: (b, 0, h, 0)),
        ]
        args += [mqk, mqk, mv, mo, g]
    in_specs += [
        pl.BlockSpec((None, CONV_WIDTH, ML_DIM), lambda b, h: (layer, 0, h)),
        pl.BlockSpec((None, CONV_WIDTH, ML_DIM), lambda b, h: (layer, 0, ML_HEADS + h)),
        pl.BlockSpec((None, 1, ML_DIM), lambda b, h: (layer, 0, h)),
    ]
    args += [conv_w, conv_w, ml_gain]
    t_c, t_l = ctx_p[0].shape[1], lat_p[0].shape[1]
    out_l_shape = jax.ShapeDtypeStruct((bsz, t_l, ML_WIDTH), BF16)
    out_l_spec = pl.BlockSpec((None, t_l, ML_DIM), lambda b, h: (b, 0, h))
    if emit_ctx:
        out_shape = (jax.ShapeDtypeStruct((bsz, t_c, ML_WIDTH), BF16), out_l_shape)
        out_specs = (pl.BlockSpec((None, t_c, ML_DIM), lambda b, h: (b, 0, h)), out_l_spec)
    else:
        out_shape = out_l_shape
        out_specs = out_l_spec
    n_tot = t_tot // CHUNK
    res = pl.pallas_call(
        functools.partial(_mlstm_kernel, emit_ctx=emit_ctx),
        grid=(bsz, ML_HEADS),
        in_specs=in_specs,
        out_specs=out_specs,
        out_shape=out_shape,
        scratch_shapes=[
            pltpu.VMEM((t_tot, ML_DIM), BF16),
            pltpu.VMEM((n_tot, ML_DIM, CHUNK), BF16),
            pltpu.VMEM((2, n_tot, GATE_ROWS, CHUNK), F32),
            pltpu.VMEM((n_tot, CHUNK, 2 * ML_DIM), F32),
            pltpu.VMEM((n_tot, ML_DIM, 2 * ML_DIM), F32),
            pltpu.VMEM((n_tot, CHUNK, 128), F32),
            pltpu.VMEM((n_tot, CHUNK, 128), F32),
            pltpu.VMEM((n_tot, GATE_ROWS, CHUNK), F32),
            pltpu.VMEM((t_tot, ML_DIM), F32),
            pltpu.VMEM((ML_DIM, 2 * ML_DIM), F32),
            pltpu.VMEM((8, 128), F32),
        ],
        compiler_params=_cparams(2),
        name="mlstm_emit" if emit_ctx else "mlstm_last",
    )(*args)
    if emit_ctx:
        return res
    return None, res


def _outmlp_kernel(*refs, final, ffn_chunk):
    x_ref, att_ref, mem_ref, mod_ref, wo_ref, gain_ref, w1_ref, w2_ref = refs[:8]
    if final:
        gf_ref, o_ref = refs[8:]
    else:
        o_ref = refs[8]
    mod = mod_ref[...]
    y = _dot(att_ref[...], wo_ref[0:ATTN_WIDTH, :]) + _dot(mem_ref[...], wo_ref[ATTN_WIDTH:D_MODEL, :])
    x1 = x_ref[...] + mod[2:3] * y
    h = _rms_modulate(x1, gain_ref[...], mod[3:4], mod[4:5]).astype(BF16)
    acc = jnp.zeros(x1.shape, F32)
    for j in range(FFN_DIM // ffn_chunk):
        f = jnp.maximum(_dot(h, w1_ref[:, ffn_chunk * j:ffn_chunk * (j + 1)]), 0.0)
        acc = acc + _dot((f * f).astype(BF16), w2_ref[ffn_chunk * j:ffn_chunk * (j + 1), :])
    x2 = x1 + mod[5:6] * acc
    if final:
        ms = jnp.mean(x2 * x2, axis=-1, keepdims=True)
        x2 = x2 * lax.rsqrt(ms + NORM_EPS) * gf_ref[...]
    o_ref[...] = x2


def _outmlp_call(x, att, mem, mod, mod_row, wo, gain, w1, w2, layer, gf, *, tm, ffn_chunk=1024):
    bsz, t, d = x.shape
    final = gf is not None
    if mod_row is None:
        mod_map = lambda b, i: (b, 0, 0)
    else:
        mod_map = lambda b, i: (mod_row, 0, 0)
    in_specs = [
        pl.BlockSpec((None, tm, d), lambda b, i: (b, i, 0)),
        pl.BlockSpec((None, tm, ATTN_WIDTH), lambda b, i: (b, i, 0)),
        pl.BlockSpec((None, tm, ML_WIDTH), lambda b, i: (b, i, 0)),
        pl.BlockSpec((None, N_MOD, d), mod_map),
        _const_spec((None, d, d), lambda b, i: (layer, 0, 0)),
        _const_spec((None, 1, d), lambda b, i: (layer, 0, 0)),
        _const_spec((None, d, FFN_DIM), lambda b, i: (layer, 0, 0)),
        _const_spec((None, FFN_DIM, d), lambda b, i: (layer, 0, 0)),
    ]
    args = [x, att, mem, mod, wo, gain, w1, w2]
    if final:
        in_specs.append(_const_spec((1, d), lambda b, i: (0, 0)))
        args.append(gf)
    return pl.pallas_call(
        functools.partial(_outmlp_kernel, final=final, ffn_chunk=ffn_chunk),
        grid=(bsz, t // tm),
        in_specs=in_specs,
        out_specs=pl.BlockSpec((None, tm, d), lambda b, i: (b, i, 0)),
        out_shape=jax.ShapeDtypeStruct((bsz, t, d), F32),
        compiler_params=_cparams(2),
        name="outproj_mlp_final" if final else "outproj_mlp",
    )(*args)


def _deinterleave(a, heads):
    lead = a.shape[:-1]
    return a.reshape(*lead, heads, HEAD_DIM // 2, 2).swapaxes(-1, -2).reshape(*lead, heads * HEAD_DIM)


def _relayout_projection(w_in):
    depth, d, _ = w_in.shape
    w = w_in.astype(BF16)
    a_q, a_k, a_v = 0, ATTN_WIDTH, ATTN_WIDTH + KV_WIDTH
    m_qk = ATTN_WIDTH + 2 * KV_WIDTH
    m_v = m_qk + 2 * ML_WIDTH
    m_o = m_v + ML_WIDTH
    gates = m_o + ML_WIDTH
    g = w[..., gates:gates + N_GATES].reshape(depth, d, 4, ML_HEADS).swapaxes(-1, -2)
    g = jnp.pad(g, ((0, 0), (0, 0), (0, 0), (0, GATE_ROWS - 4))).reshape(depth, d, ML_HEADS * GATE_ROWS)
    g = jnp.pad(g, ((0, 0), (0, 0), (0, 128 - ML_HEADS * GATE_ROWS)))
    return jnp.concatenate([
        w[..., m_qk:m_qk + 2 * ML_WIDTH],
        _deinterleave(w[..., a_q:a_q + ATTN_WIDTH], ATTN_HEADS),
        w[..., m_v:m_v + ML_WIDTH],
        w[..., m_o:m_o + ML_WIDTH],
        _deinterleave(w[..., a_k:a_k + KV_WIDTH], KV_HEADS),
        w[..., a_v:a_v + KV_WIDTH],
        g], axis=-1)


def _rope_tables(t):
    rows = t // GRID_W
    row_idx = jnp.repeat(jnp.arange(rows, dtype=F32), GRID_W)
    col_idx = jnp.tile(jnp.arange(GRID_W, dtype=F32), rows)
    inv_freq = jnp.power(ROPE_THETA, -jnp.arange(0, ROPE_AXIS_DIM, 2, dtype=F32) / ROPE_AXIS_DIM)
    ang = jnp.concatenate([row_idx[:, None] * inv_freq, col_idx[:, None] * inv_freq], axis=-1)
    cos, sin = jnp.cos(ang), jnp.sin(ang)
    cos128 = jnp.tile(cos, (1, 4))
    sin128 = jnp.tile(jnp.concatenate([-sin, sin], axis=-1), (1, 2))
    return cos128, sin128


def _pick_tile(t, pref):
    tm = min(pref, t)
    while t % tm:
        tm //= 2
    return tm


def kernel(x, c, ctx, c_ctx, w_ada, b_ada, norm_mix, norm_mlp, w_in, b_gates, conv_qk,
           q_norm, k_norm, mlstm_norm, w_out, w_mlp_in, w_mlp_out, norm_final):
    bsz, t, d = x.shape
    t_c = ctx.shape[1]
    depth = w_ada.shape[0]
    assert d == D_MODEL and t % CHUNK == 0 and t_c % CHUNK == 0 and t % GRID_W == 0

    rows = -(-(bsz + 1) // 8) * 8
    cvec = jnp.zeros((rows, d), F32).at[:bsz].set(c).at[bsz].set(c_ctx)
    mod = _ada_call(cvec, w_ada, b_ada).reshape(depth, rows, N_MOD, d)

    w_all = _relayout_projection(w_in)
    assert w_all.shape[-1] == W_COLS
    qg = jnp.tile(_deinterleave(q_norm, 1), (1, 2)).reshape(depth, 1, 128)
    kg = jnp.tile(_deinterleave(k_norm, 1), (1, 2)).reshape(depth, 1, 128)
    norm_mix = norm_mix.reshape(depth, 1, d)
    norm_mlp = norm_mlp.reshape(depth, 1, d)
    mlstm_norm = mlstm_norm.reshape(depth, 1, ML_WIDTH)
    bg = jnp.zeros((depth, ML_HEADS, GATE_ROWS), F32).at[:, :, 0:4].set(
        b_gates.reshape(depth, 4, ML_HEADS).transpose(0, 2, 1)).reshape(depth, ML_HEADS * GATE_ROWS, 1)
    wo = w_out.astype(BF16)
    w1 = w_mlp_in.astype(BF16)
    w2 = w_mlp_out.astype(BF16)
    cos128, sin128 = _rope_tables(t)
    gf = norm_final.reshape(1, d)

    tm_l = _pick_tile(t, 512)
    tm_c = _pick_tile(t_c, 512)
    tq_l = _pick_tile(t, 512)
    tq_c = _pick_tile(t_c, 256)

    for layer in range(depth):
        emit_ctx = layer < depth - 1
        mod_l = mod[layer]
        qt_l, k_l, vt_l, mqk_l, mv_l, mo_l, g_l = _inproj_call(
            x, mod_l, None, norm_mix, w_all, layer, qg, kg, bg, cos128, sin128, use_rope=True, tm=tm_l)
        qt_c, k_c, vt_c, mqk_c, mv_c, mo_c, g_c = _inproj_call(
            ctx, mod_l, bsz, norm_mix, w_all, layer, qg, kg, bg, cos128, sin128, use_rope=False, tm=tm_c)
        att_l = _attn_call(qt_l, [k_c, k_l], [vt_c, vt_l], tq=tq_l)
        mem_c, mem_l = _mlstm_call((mqk_c, mv_c, mo_c, g_c), (mqk_l, mv_l, mo_l, g_l),
                                   conv_qk, mlstm_norm, layer, emit_ctx=emit_ctx)
        x = _outmlp_call(x, att_l, mem_l, mod_l, None, wo, norm_mlp, w1, w2, layer,
                         None if emit_ctx else gf, tm=tm_l)
        if emit_ctx:
            att_c = _attn_call(qt_c, [k_c], [vt_c], tq=tq_c)
            ctx = _outmlp_call(ctx, att_c, mem_c, mod_l, bsz, wo, norm_mlp, w1, w2, layer, None, tm=tm_c)
    return x
```

```python
import functools

import numpy as np
import jax
import jax.numpy as jnp
from jax import lax
from jax.experimental import pallas as pl
from jax.experimental.pallas import tpu as pltpu

D_MODEL = 1024
N_MOD = 6
GRID_W = 64
ATTN_WIDTH = 512
ATTN_HEADS = 8
HEAD_DIM = 64
KV_HEADS = 2
ATTN_GROUP = ATTN_HEADS // KV_HEADS
KV_WIDTH = KV_HEADS * HEAD_DIM
ROPE_THETA = 10000.0
ROPE_AXIS_DIM = HEAD_DIM // 2
ML_WIDTH = 512
ML_HEADS = 4
ML_DIM = 128
CHUNK = 128
CONV_WIDTH = 3
N_GATES = 4 * ML_HEADS
GATE_ROWS = 8
FFN_DIM = 4 * D_MODEL
NORM_EPS = 1e-6
LOG2_E = 1.4426950408889634

C_MQK = 0
C_Q = C_MQK + 2 * ML_WIDTH
C_MV = C_Q + ATTN_WIDTH
C_MO = C_MV + ML_WIDTH
C_K = C_MO + ML_WIDTH
C_V = C_K + KV_WIDTH
C_G = C_V + KV_WIDTH
W_COLS = C_G + 128

VMEM_LIMIT_BYTES = 56 * 1024 * 1024

BF16 = jnp.bfloat16
F32 = jnp.float32


def _cparams(n_grid, flags=None):
    return pltpu.CompilerParams(
        dimension_semantics=("arbitrary",) * n_grid,
        vmem_limit_bytes=VMEM_LIMIT_BYTES,
        flags=flags)


def _const_spec(shape, index_map):
    return pl.BlockSpec(shape, index_map, pipeline_mode=pl.Buffered(1))


def _aligned(i, m):
    return i if isinstance(i, int) else pl.multiple_of(i, m)


def _dot(a, b):
    return jnp.dot(a, b, preferred_element_type=F32)


def _sigmoid(x):
    return 1.0 / (1.0 + jnp.exp(-x))


def _log_sigmoid(x):
    return jnp.minimum(x, 0.0) - jnp.log(1.0 + jnp.exp(-jnp.abs(x)))


def _group_ones(n, group):
    r = lax.broadcasted_iota(jnp.int32, (n, n), 0) // group
    c = lax.broadcasted_iota(jnp.int32, (n, n), 1) // group
    return jnp.where(r == c, 1.0, 0.0).astype(BF16)


def _split_dot(a, ones_mat):
    hi = a.astype(BF16)
    lo = (a - hi.astype(F32)).astype(BF16)
    return _dot(hi, ones_mat) + _dot(lo, ones_mat)


def _ada_kernel(c_ref, w_ref, b_ref, o_ref):
    c = c_ref[...]
    sc = (c * _sigmoid(c)).astype(BF16)
    o_ref[0] = _dot(sc, w_ref[0].astype(BF16)) + b_ref[0]


def _ada_call(cvec, w_ada, b_ada):
    depth, d, n = w_ada.shape
    rows = cvec.shape[0]
    tn = 1536
    return pl.pallas_call(
        _ada_kernel,
        grid=(depth, n // tn),
        in_specs=[
            pl.BlockSpec((rows, d), lambda l, j: (0, 0)),
            pl.BlockSpec((1, d, tn), lambda l, j: (l, 0, j)),
            pl.BlockSpec((1, 1, tn), lambda l, j: (l, 0, j)),
        ],
        out_specs=pl.BlockSpec((1, rows, tn), lambda l, j: (l, 0, j)),
        out_shape=jax.ShapeDtypeStruct((depth, rows, n), F32),
        compiler_params=_cparams(2),
        name="ada_mod",
    )(cvec, w_ada, b_ada.reshape(depth, 1, n))


def _rms_modulate(x, gain, shift, scale):
    ms = jnp.mean(x * x, axis=-1, keepdims=True)
    y = x * lax.rsqrt(ms + NORM_EPS) * gain
    return y * (1.0 + scale) + shift


def _swap32(x):
    lane = lax.broadcasted_iota(jnp.int32, x.shape, 1)
    up = pltpu.roll(x, 96, 1)
    down = pltpu.roll(x, 32, 1)
    return jnp.where((lane % HEAD_DIM) < ROPE_AXIS_DIM, up, down)


def _head_norm_rope(t, gain, cos, sin, use_rope):
    ss = _dot((t * t).astype(BF16), _group_ones(128, HEAD_DIM))
    tn = t * lax.rsqrt(ss * (1.0 / HEAD_DIM) + NORM_EPS) * gain
    if use_rope:
        tn = tn * cos + _swap32(tn) * sin
    return tn


def _inproj_kernel(x_ref, mod_ref, gain_ref, w_ref, qg_ref, kg_ref, bg_ref, cos_ref, sin_ref,
                   qt_out, k_out, vt_out, mqk_out, mv_out, mo_out, g_out, *, use_rope):
    x = x_ref[...]
    mod = mod_ref[...]
    h = _rms_modulate(x, gain_ref[...], mod[0:1], mod[1:2]).astype(BF16)
    tm = x.shape[0]
    cos = cos_ref[...] if use_rope else None
    sin = sin_ref[...] if use_rope else None

    q = _dot(h, w_ref[:, C_Q:C_Q + ATTN_WIDTH])
    k = _dot(h, w_ref[:, C_K:C_K + KV_WIDTH])
    g = _dot(h, w_ref[:, C_G:C_G + 128])
    mqk_out[...] = _dot(h, w_ref[:, C_MQK:C_MQK + 2 * ML_WIDTH])

    qscale = HEAD_DIM ** -0.5 * LOG2_E
    q_blocks = [_head_norm_rope(q[:, 128 * j:128 * (j + 1)], qg_ref[...], cos, sin, use_rope)
                for j in range(ATTN_WIDTH // 128)]
    k_normed = _head_norm_rope(k, kg_ref[...], cos, sin, use_rope)

    mv_out[...] = _dot(h, w_ref[:, C_MV:C_MV + ML_WIDTH]).astype(BF16)
    mo_out[...] = _sigmoid(_dot(h, w_ref[:, C_MO:C_MO + ML_WIDTH])).astype(BF16)
    vt = _dot(h, w_ref[:, C_V:C_V + KV_WIDTH]).T
    pad_row = lax.broadcasted_iota(jnp.int32, (HEAD_DIM, tm), 0)
    pad = jnp.where(pad_row == 0, 1.0, 0.0)
    for kvh in range(KV_HEADS):
        vt_out[128 * kvh:128 * kvh + HEAD_DIM, :] = vt[HEAD_DIM * kvh:HEAD_DIM * (kvh + 1)].astype(BF16)
        vt_out[128 * kvh + HEAD_DIM:128 * (kvh + 1), :] = pad.astype(BF16)

    for j, blk in enumerate(q_blocks):
        qt_out[128 * j:128 * (j + 1), :] = (blk * qscale).T.astype(BF16)
    k_out[...] = k_normed.astype(BF16)

    gt = g.T[0:ML_HEADS * GATE_ROWS, :] + bg_ref[...]
    row = lax.broadcasted_iota(jnp.int32, gt.shape, 0) % GATE_ROWS
    gt = jnp.where((row == 1) | (row == 3), _log_sigmoid(gt), gt)
    for j in range(tm // CHUNK):
        g_out[j] = gt[:, CHUNK * j:CHUNK * (j + 1)]


def _inproj_call(x, mod, mod_row, gain, w_all, layer, qg, kg, bg, cos, sin, *, use_rope, tm):
    bsz, t, d = x.shape
    nt = t // tm
    nc = t // CHUNK
    cpt = tm // CHUNK
    if mod_row is None:
        mod_map = lambda b, i: (b, 0, 0)
    else:
        mod_map = lambda b, i: (mod_row, 0, 0)
    tok = lambda w: pl.BlockSpec((None, tm, w), lambda b, i: (b, i, 0))
    out_shapes = (
        jax.ShapeDtypeStruct((bsz, ATTN_WIDTH, t), BF16),
        jax.ShapeDtypeStruct((bsz, t, KV_WIDTH), BF16),
        jax.ShapeDtypeStruct((bsz, KV_HEADS * 128, t), BF16),
        jax.ShapeDtypeStruct((bsz, t, 2 * ML_WIDTH), F32),
        jax.ShapeDtypeStruct((bsz, t, ML_WIDTH), BF16),
        jax.ShapeDtypeStruct((bsz, t, ML_WIDTH), BF16),
        jax.ShapeDtypeStruct((bsz, nc, ML_HEADS * GATE_ROWS, CHUNK), F32),
    )
    out_specs = (
        pl.BlockSpec((None, ATTN_WIDTH, tm), lambda b, i: (b, 0, i)),
        tok(KV_WIDTH),
        pl.BlockSpec((None, KV_HEADS * 128, tm), lambda b, i: (b, 0, i)),
        tok(2 * ML_WIDTH),
        tok(ML_WIDTH),
        tok(ML_WIDTH),
        pl.BlockSpec((None, cpt, ML_HEADS * GATE_ROWS, CHUNK), lambda b, i: (b, i, 0, 0)),
    )
    in_specs = [
        tok(d),
        pl.BlockSpec((None, N_MOD, d), mod_map),
        _const_spec((None, 1, d), lambda b, i: (layer, 0, 0)),
        _const_spec((None, d, W_COLS), lambda b, i: (layer, 0, 0)),
        _const_spec((None, 1, 128), lambda b, i: (layer, 0, 0)),
        _const_spec((None, 1, 128), lambda b, i: (layer, 0, 0)),
        _const_spec((None, ML_HEADS * GATE_ROWS, 1), lambda b, i: (layer, 0, 0)),
        pl.BlockSpec((tm, 128), lambda b, i: (i, 0)),
        pl.BlockSpec((tm, 128), lambda b, i: (i, 0)),
    ]
    return pl.pallas_call(
        functools.partial(_inproj_kernel, use_rope=use_rope),
        grid=(bsz, nt),
        in_specs=in_specs,
        out_specs=out_specs,
        out_shape=out_shapes,
        compiler_params=_cparams(2),
        name="inproj_rope" if use_rope else "inproj_ctx",
    )(x, mod, gain, w_all, qg, kg, bg, cos, sin)


def _attn_kernel(*refs, n_seg, q_block):
    qt_ref = refs[0]
    k_refs = refs[1:1 + n_seg]
    vt_refs = refs[1 + n_seg:1 + 2 * n_seg]
    o_ref = refs[1 + 2 * n_seg]
    k_all, vt_all, st_ref = refs[2 + 2 * n_seg:]
    s_tot = k_all.shape[0]
    tq = qt_ref.shape[1]
    cols = ATTN_GROUP * q_block

    @pl.when(pl.program_id(1) == 0)
    def _():
        base = 0
        for seg in range(n_seg):
            s_len = k_refs[seg].shape[0]
            k_all[base:base + s_len, :] = k_refs[seg][...]
            vt_all[:, base:base + s_len] = vt_refs[seg][...]
            base += s_len

    groups = [(qb, kvh) for qb in range(tq // q_block) for kvh in range(KV_HEADS)]
    zeros = jnp.zeros((HEAD_DIM, q_block), BF16)

    def padded_queries(qb, kvh):
        pieces = []
        for g in range(ATTN_GROUP):
            row0 = HEAD_DIM * (ATTN_GROUP * kvh + g)
            qg = qt_ref[row0:row0 + HEAD_DIM, q_block * qb:q_block * (qb + 1)]
            pieces.append(jnp.concatenate([qg, zeros] if kvh == 0 else [zeros, qg], axis=0))
        return jnp.concatenate(pieces, axis=1)

    def finalize(qb, kvh, acc):
        ot = acc[0:HEAD_DIM] * (1.0 / acc[HEAD_DIM:HEAD_DIM + 1])
        for pair in range(ATTN_GROUP // 2):
            two = jnp.concatenate([ot[:, q_block * (2 * pair):q_block * (2 * pair + 1)],
                                   ot[:, q_block * (2 * pair + 1):q_block * (2 * pair + 2)]], axis=0)
            lane0 = HEAD_DIM * (ATTN_GROUP * kvh + 2 * pair)
            o_ref[q_block * qb:q_block * (qb + 1), lane0:lane0 + 2 * HEAD_DIM] = two.T.astype(BF16)

    m_prev = None
    for j in range(len(groups) + 1):
        qz = padded_queries(*groups[j]) if j < len(groups) else None
        m_next = None
        if qz is not None:
            st = _dot(k_all[...], qz)
            st_ref[j % 2] = st
            m_next = jnp.max(jnp.max(st.reshape(s_tot // 8, 8, cols), axis=0), axis=0, keepdims=True)
        if j >= 1:
            qb, kvh = groups[j - 1]
            p = jnp.exp2(st_ref[(j - 1) % 2] - m_prev).astype(BF16)
            finalize(qb, kvh, _dot(vt_all[128 * kvh:128 * (kvh + 1), :], p))
        m_prev = m_next


def _attn_call(qt, ks, vts, *, tq, q_block=128):
    bsz, _, t = qt.shape
    n_seg = len(ks)
    s_tot = sum(k.shape[1] for k in ks)
    q_block = min(q_block, tq)
    cols = ATTN_GROUP * q_block
    in_specs = [pl.BlockSpec((None, ATTN_WIDTH, tq), lambda b, i: (b, 0, i))]
    for k in ks:
        in_specs.append(pl.BlockSpec((None, k.shape[1], KV_WIDTH), lambda b, i: (b, 0, 0)))
    for vt in vts:
        in_specs.append(pl.BlockSpec((None, KV_HEADS * 128, vt.shape[2]), lambda b, i: (b, 0, 0)))
    return pl.pallas_call(
        functools.partial(_attn_kernel, n_seg=n_seg, q_block=q_block),
        grid=(bsz, t // tq),
        in_specs=in_specs,
        out_specs=pl.BlockSpec((None, tq, ATTN_WIDTH), lambda b, i: (b, i, 0)),
        out_shape=jax.ShapeDtypeStruct((bsz, t, ATTN_WIDTH), BF16),
        scratch_shapes=[
            pltpu.VMEM((s_tot, KV_WIDTH), BF16),
            pltpu.VMEM((KV_HEADS * 128, s_tot), BF16),
            pltpu.VMEM((2, s_tot, cols), F32),
        ],
        compiler_params=_cparams(2),
        name="attention_%dseg" % n_seg,
    )(qt, *ks, *vts)


def _mlstm_kernel(*refs, emit_ctx):
    (mq_c, mk_c, mv_c, mo_c, g_c, mq_l, mk_l, mv_l, mo_l, g_l, cw_q, cw_k, gain_ref) = refs[:13]
    if emit_ctx:
        out_c, out_l = refs[13:15]
        scratch = refs[15:]
    else:
        out_c = None
        out_l = refs[13]
        scratch = refs[14:]
    q_s, kt_s, cum_s, x_s, u_s, pm_s, bc_s, sc_s, hf_s, ct_s, m_s = scratch

    L = CHUNK
    segs = []
    chunk_base = 0
    for (mq, mk, mv, mo, g, out, emit) in ((mq_c, mk_c, mv_c, mo_c, g_c, out_c, emit_ctx),
                                           (mq_l, mk_l, mv_l, mo_l, g_l, out_l, True)):
        n = mq.shape[0] // L
        segs.append(dict(mq=mq, mk=mk, mv=mv, mo=mo, g=g, out=out, emit=emit, n=n, cb=chunk_base))
        chunk_base += n

    ri = lax.broadcasted_iota(jnp.int32, (L, L), 0)
    ci = lax.broadcasted_iota(jnp.int32, (L, L), 1)
    ones_le = jnp.where(ri <= ci, 1.0, 0.0).astype(BF16)
    ones_ge = jnp.where(ri >= ci, 1.0, 0.0).astype(BF16)
    masks = (ci <= ri, ci >= ri)
    row_i = lax.broadcasted_iota(jnp.int32, (L, ML_DIM), 0)
    ones_blk = jnp.ones((L, 128), BF16)
    kscale = ML_DIM ** -0.5

    def unroll_of(n, want):
        while n % want:
            want //= 2
        return want

    for sg in segs:
        n, cb = sg["n"], sg["cb"]
        t_seg = n * L

        def conv_silu(src, cw, c, t_seg=t_seg, n=n):
            t0 = pl.multiple_of(c * L, L)
            xc = src[pl.ds(t0, L), :]
            prev = src[pl.ds(jnp.maximum(t0 - 1, 0), 1), :] * jnp.where(c > 0, 1.0, 0.0)
            nxt = src[pl.ds(jnp.minimum(t0 + L, t_seg - 1), 1), :] * jnp.where(c < n - 1, 1.0, 0.0)
            xm = jnp.where(row_i == 0, prev, pltpu.roll(xc, 1, 0))
            xp = jnp.where(row_i == L - 1, nxt, pltpu.roll(xc, L - 1, 0))
            y = xm * cw[0:1, :] + xc * cw[1:2, :] + xp * cw[2:3, :]
            return y * _sigmoid(y)

        def prep(c, carry, sg=sg, cb=cb, conv_silu=conv_silu):
            t0g = pl.multiple_of((cb + c) * L, L)
            q_s[pl.ds(t0g, L), :] = conv_silu(sg["mq"], cw_q, c).astype(BF16)
            kt_s[cb + c] = (conv_silu(sg["mk"], cw_k, c) * kscale).T.astype(BF16)
            return carry

        lax.fori_loop(0, n, prep, 0, unroll=unroll_of(n, 2))
        g2 = sg["g"][...].reshape(n * GATE_ROWS, L) * LOG2_E
        cum_s[0, cb:cb + n] = _split_dot(g2, ones_le).reshape(n, GATE_ROWS, L)
        cum_s[1, cb:cb + n] = _split_dot(g2, ones_ge).reshape(n, GATE_ROWS, L)

    def gate_rows(d, sg, c):
        gr = sg["g"][c] * LOG2_E
        cum = cum_s[d, sg["cb"] + c]
        if d == 0:
            return gr[0:1], gr[1:2], cum[1:2], cum[1:2, L - 1:L]
        return gr[2:3], gr[3:4], cum[3:4], cum[3:4, 0:1]

    def state_free_pass():
        for sg in segs:
            group = unroll_of(sg["n"], 4)

            def body(i, carry, sg=sg, group=group):
                pending = []
                for j in range(group):
                    c = i * group + j
                    cg = sg["cb"] + c
                    t0 = pl.multiple_of(c * L, L)
                    t0g = pl.multiple_of(cg * L, L)
                    ktc = kt_s[cg]
                    v_aug = jnp.concatenate([sg["mv"][pl.ds(t0, L), :], ones_blk], axis=1)
                    s0 = _dot(q_s[pl.ds(t0g, L), :], ktc) if sg["emit"] else None
                    kws, es = [], []
                    for d in range(2):
                        ig, lf, brow, total = gate_rows(d, sg, c)
                        a = ig - brow
                        amax = jnp.max(a, axis=-1, keepdims=True)
                        kws.append((ktc.astype(F32) * jnp.exp2(a - amax)).astype(BF16))
                        sc_s[d, cg] = jnp.concatenate(
                            [jnp.broadcast_to(amax, (1, L)), jnp.broadcast_to(total, (1, L)),
                             jnp.zeros((GATE_ROWS - 2, L), F32)], axis=0)
                        if sg["emit"]:
                            a_vis = jnp.where(masks[d], a, -jnp.inf)
                            pm = jnp.broadcast_to(jnp.max(a_vis, axis=-1, keepdims=True), (L, L))
                            pm_s[d, cg] = pm
                            bc_s[d, cg] = jnp.broadcast_to(
                                jnp.sum(jnp.where(masks[d], lf, 0.0), axis=-1, keepdims=True), (L, L))
                            es.append(jnp.exp2(a_vis - pm))
                    pending.append((cg, s0, kws, es, v_aug))
                for cg, s0, kws, es, v_aug in pending:
                    lhs = jnp.concatenate(kws + [(s0 * e).astype(BF16) for e in es], axis=0)
                    res = _dot(lhs, v_aug)
                    for d in range(2):
                        u_s[d, cg] = res[L * d:L * (d + 1)]
                        if es:
                            x_s[d, cg] = res[L * (2 + d):L * (3 + d)]
                return carry
            lax.fori_loop(0, sg["n"] // group, body, 0)

    def sequential_pass(d):
        ct_s[...] = jnp.zeros(ct_s.shape, F32)
        m_s[...] = jnp.zeros(m_s.shape, F32)
        for sg in segs:
            n = sg["n"]
            group = unroll_of(n, 4)

            def body(i, carry, sg=sg, n=n, group=group):
                m_prev = m_s[0:1, 0:1]
                ct = ct_s[...]
                pending = []
                for j in range(group):
                    c = i * group + j if d == 0 else n - 1 - (i * group + j)
                    cg = sg["cb"] + c
                    t0 = pl.multiple_of(c * L, L)
                    t0g = pl.multiple_of(cg * L, L)
                    sc = sc_s[d, cg]
                    amax, total = sc[0:1, 0:1], sc[1:2, 0:1]
                    m_last = jnp.maximum(m_prev, amax)
                    if sg["emit"]:
                        inter = _dot(q_s[pl.ds(t0g, L), :], ct.astype(BF16))
                        pending.append((cg, t0, t0g, inter, m_prev))
                    ct = jnp.exp2(m_prev - m_last) * ct + jnp.exp2(amax - m_last) * u_s[d, cg]
                    m_prev = total + m_last
                ct_s[...] = ct
                m_s[...] = jnp.broadcast_to(m_prev, m_s.shape)
                for cg, t0, t0g, inter, m_in in pending:
                    pm = pm_s[d, cg]
                    mx = jnp.maximum(pm, m_in)
                    r = jnp.exp2(pm - mx)
                    wi = jnp.exp2(m_in - mx)
                    x = x_s[d, cg]
                    num = r * x[:, 0:ML_DIM] + wi * inter[:, 0:ML_DIM]
                    den = r * x[:, ML_DIM:] + wi * inter[:, ML_DIM:]
                    h = num / jnp.maximum(jnp.abs(den), jnp.exp2(-bc_s[d, cg] - mx))
                    if d == 0:
                        hf_s[pl.ds(t0g, L), :] = h
                    else:
                        hsum = hf_s[pl.ds(t0g, L), :] + h
                        ms = jnp.mean(hsum * hsum, axis=-1, keepdims=True)
                        hn = hsum * lax.rsqrt(ms + NORM_EPS) * gain_ref[...]
                        sg["out"][pl.ds(t0, L), :] = (sg["mo"][pl.ds(t0, L), :].astype(F32) * hn).astype(BF16)
                return carry

            lax.fori_loop(0, n // group, body, 0)

    state_free_pass()
    for d in range(2):
        sequential_pass(d)


def _mlstm_call(ctx_p, lat_p, conv_w, ml_gain, layer, *, emit_ctx):
    bsz = lat_p[0].shape[0]
    in_specs = []
    args = []
    t_tot = 0
    for (mqk, mv, mo, g) in (ctx_p, lat_p):
        t = mqk.shape[1]
        n = t // CHUNK
        t_tot += t
        in_specs += [
            pl.BlockSpec((None, t, ML_DIM), lambda b, h: (b, 0, h)),
            pl.BlockSpec((None, t, ML_DIM), lambda b, h: (b, 0, ML_HEADS + h)),
            pl.BlockSpec((None, t, ML_DIM), lambda b, h: (b, 0, h)),
            pl.BlockSpec((None, t, ML_DIM), lambda b, h: (b, 0, h)),
            pl.BlockSpec((None, n, GATE_ROWS, CHUNK), lambda b, h: (b, 0, h, 0)),
        ]
        args += [mqk, mqk, mv, mo, g]
    in_specs += [
        pl.BlockSpec((None, CONV_WIDTH, ML_DIM), lambda b, h: (layer, 0, h)),
        pl.BlockSpec((None, CONV_WIDTH, ML_DIM), lambda b, h: (layer, 0, ML_HEADS + h)),
        pl.BlockSpec((None, 1, ML_DIM), lambda b, h: (layer, 0, h)),
    ]
    args += [conv_w, conv_w, ml_gain]
    t_c, t_l = ctx_p[0].shape[1], lat_p[0].shape[1]
    out_l_shape = jax.ShapeDtypeStruct((bsz, t_l, ML_WIDTH), BF16)
    out_l_spec = pl.BlockSpec((None, t_l, ML_DIM), lambda b, h: (b, 0, h))
    if emit_ctx:
        out_shape = (jax.ShapeDtypeStruct((bsz, t_c, ML_WIDTH), BF16), out_l_shape)
        out_specs = (pl.BlockSpec((None, t_c, ML_DIM), lambda b, h: (b, 0, h)), out_l_spec)
    else:
        out_shape = out_l_shape
        out_specs = out_l_spec
    n_tot = t_tot // CHUNK
    res = pl.pallas_call(
        functools.partial(_mlstm_kernel, emit_ctx=emit_ctx),
        grid=(bsz, ML_HEADS),
        in_specs=in_specs,
        out_specs=out_specs,
        out_shape=out_shape,
        scratch_shapes=[
            pltpu.VMEM((t_tot, ML_DIM), BF16),
            pltpu.VMEM((n_tot, ML_DIM, CHUNK), BF16),
            pltpu.VMEM((2, n_tot, GATE_ROWS, CHUNK), F32),
            pltpu.VMEM((2, n_tot, CHUNK, 2 * ML_DIM), F32),
            pltpu.VMEM((2, n_tot, ML_DIM, 2 * ML_DIM), F32),
            pltpu.VMEM((2, n_tot, CHUNK, 128), F32),
            pltpu.VMEM((2, n_tot, CHUNK, 128), F32),
            pltpu.VMEM((2, n_tot, GATE_ROWS, CHUNK), F32),
            pltpu.VMEM((t_tot, ML_DIM), F32),
            pltpu.VMEM((ML_DIM, 2 * ML_DIM), F32),
            pltpu.VMEM((8, 128), F32),
        ],
        compiler_params=_cparams(2),
        name="mlstm_emit" if emit_ctx else "mlstm_last",
    )(*args)
    if emit_ctx:
        return res
    return None, res


def _outmlp_kernel(*refs, final, ffn_chunk):
    x_ref, att_ref, mem_ref, mod_ref, wo_ref, gain_ref, w1_ref, w2_ref = refs[:8]
    if final:
        gf_ref, o_ref = refs[8:]
    else:
        o_ref = refs[8]
    mod = mod_ref[...]
    y = _dot(att_ref[...], wo_ref[0:ATTN_WIDTH, :]) + _dot(mem_ref[...], wo_ref[ATTN_WIDTH:D_MODEL, :])
    x1 = x_ref[...] + mod[2:3] * y
    h = _rms_modulate(x1, gain_ref[...], mod[3:4], mod[4:5]).astype(BF16)
    acc = jnp.zeros(x1.shape, F32)
    for j in range(FFN_DIM // ffn_chunk):
        f = jnp.maximum(_dot(h, w1_ref[:, ffn_chunk * j:ffn_chunk * (j + 1)]), 0.0)
        acc = acc + _dot((f * f).astype(BF16), w2_ref[ffn_chunk * j:ffn_chunk * (j + 1), :])
    x2 = x1 + mod[5:6] * acc
    if final:
        ms = jnp.mean(x2 * x2, axis=-1, keepdims=True)
        x2 = x2 * lax.rsqrt(ms + NORM_EPS) * gf_ref[...]
    o_ref[...] = x2


def _outmlp_call(x, att, mem, mod, mod_row, wo, gain, w1, w2, layer, gf, *, tm, ffn_chunk=1024):
    bsz, t, d = x.shape
    final = gf is not None
    if mod_row is None:
        mod_map = lambda b, i: (b, 0, 0)
    else:
        mod_map = lambda b, i: (mod_row, 0, 0)
    in_specs = [
        pl.BlockSpec((None, tm, d), lambda b, i: (b, i, 0)),
        pl.BlockSpec((None, tm, ATTN_WIDTH), lambda b, i: (b, i, 0)),
        pl.BlockSpec((None, tm, ML_WIDTH), lambda b, i: (b, i, 0)),
        pl.BlockSpec((None, N_MOD, d), mod_map),
        _const_spec((None, d, d), lambda b, i: (layer, 0, 0)),
        _const_spec((None, 1, d), lambda b, i: (layer, 0, 0)),
        _const_spec((None, d, FFN_DIM), lambda b, i: (layer, 0, 0)),
        _const_spec((None, FFN_DIM, d), lambda b, i: (layer, 0, 0)),
    ]
    args = [x, att, mem, mod, wo, gain, w1, w2]
    if final:
        in_specs.append(_const_spec((1, d), lambda b, i: (0, 0)))
        args.append(gf)
    return pl.pallas_call(
        functools.partial(_outmlp_kernel, final=final, ffn_chunk=ffn_chunk),
        grid=(bsz, t // tm),
        in_specs=in_specs,
        out_specs=pl.BlockSpec((None, tm, d), lambda b, i: (b, i, 0)),
        out_shape=jax.ShapeDtypeStruct((bsz, t, d), F32),
        compiler_params=_cparams(2),
        name="outproj_mlp_final" if final else "outproj_mlp",
    )(*args)


def _deinterleave(a, heads):
    lead = a.shape[:-1]
    return a.reshape(*lead, heads, HEAD_DIM // 2, 2).swapaxes(-1, -2).reshape(*lead, heads * HEAD_DIM)


def _relayout_projection(w_in):
    depth, d, _ = w_in.shape
    w = w_in.astype(BF16)
    a_q, a_k, a_v = 0, ATTN_WIDTH, ATTN_WIDTH + KV_WIDTH
    m_qk = ATTN_WIDTH + 2 * KV_WIDTH
    m_v = m_qk + 2 * ML_WIDTH
    m_o = m_v + ML_WIDTH
    gates = m_o + ML_WIDTH
    g = w[..., gates:gates + N_GATES].reshape(depth, d, 4, ML_HEADS).swapaxes(-1, -2)
    g = jnp.pad(g, ((0, 0), (0, 0), (0, 0), (0, GATE_ROWS - 4))).reshape(depth, d, ML_HEADS * GATE_ROWS)
    g = jnp.pad(g, ((0, 0), (0, 0), (0, 128 - ML_HEADS * GATE_ROWS)))
    return jnp.concatenate([
        w[..., m_qk:m_qk + 2 * ML_WIDTH],
        _deinterleave(w[..., a_q:a_q + ATTN_WIDTH], ATTN_HEADS),
        w[..., m_v:m_v + ML_WIDTH],
        w[..., m_o:m_o + ML_WIDTH],
        _deinterleave(w[..., a_k:a_k + KV_WIDTH], KV_HEADS),
        w[..., a_v:a_v + KV_WIDTH],
        g], axis=-1)


def _rope_tables(t):
    rows = t // GRID_W
    row_idx = jnp.repeat(jnp.arange(rows, dtype=F32), GRID_W)
    col_idx = jnp.tile(jnp.arange(GRID_W, dtype=F32), rows)
    inv_freq = jnp.power(ROPE_THETA, -jnp.arange(0, ROPE_AXIS_DIM, 2, dtype=F32) / ROPE_AXIS_DIM)
    ang = jnp.concatenate([row_idx[:, None] * inv_freq, col_idx[:, None] * inv_freq], axis=-1)
    cos, sin = jnp.cos(ang), jnp.sin(ang)
    cos128 = jnp.tile(cos, (1, 4))
    sin128 = jnp.tile(jnp.concatenate([-sin, sin], axis=-1), (1, 2))
    return cos128, sin128


def _pick_tile(t, pref):
    tm = min(pref, t)
    while t % tm:
        tm //= 2
    return tm


def kernel(x, c, ctx, c_ctx, w_ada, b_ada, norm_mix, norm_mlp, w_in, b_gates, conv_qk,
           q_norm, k_norm, mlstm_norm, w_out, w_mlp_in, w_mlp_out, norm_final):
    bsz, t, d = x.shape
    t_c = ctx.shape[1]
    depth = w_ada.shape[0]
    assert d == D_MODEL and t % CHUNK == 0 and t_c % CHUNK == 0 and t % GRID_W == 0

    rows = -(-(bsz + 1) // 8) * 8
    cvec = jnp.zeros((rows, d), F32).at[:bsz].set(c).at[bsz].set(c_ctx)
    mod = _ada_call(cvec, w_ada, b_ada).reshape(depth, rows, N_MOD, d)

    w_all = _relayout_projection(w_in)
    assert w_all.shape[-1] == W_COLS
    qg = jnp.tile(_deinterleave(q_norm, 1), (1, 2)).reshape(depth, 1, 128)
    kg = jnp.tile(_deinterleave(k_norm, 1), (1, 2)).reshape(depth, 1, 128)
    norm_mix = norm_mix.reshape(depth, 1, d)
    norm_mlp = norm_mlp.reshape(depth, 1, d)
    mlstm_norm = mlstm_norm.reshape(depth, 1, ML_WIDTH)
    bg = jnp.zeros((depth, ML_HEADS, GATE_ROWS), F32).at[:, :, 0:4].set(
        b_gates.reshape(depth, 4, ML_HEADS).transpose(0, 2, 1)).reshape(depth, ML_HEADS * GATE_ROWS, 1)
    wo = w_out.astype(BF16)
    w1 = w_mlp_in.astype(BF16)
    w2 = w_mlp_out.astype(BF16)
    cos128, sin128 = _rope_tables(t)
    gf = norm_final.reshape(1, d)

    tm_l = _pick_tile(t, 512)
    tm_c = _pick_tile(t_c, 512)
    tq_l = _pick_tile(t, 512)
    tq_c = _pick_tile(t_c, 256)

    for layer in range(depth):
        emit_ctx = layer < depth - 1
        mod_l = mod[layer]
        qt_l, k_l, vt_l, mqk_l, mv_l, mo_l, g_l = _inproj_call(
            x, mod_l, None, norm_mix, w_all, layer, qg, kg, bg, cos128, sin128, use_rope=True, tm=tm_l)
        qt_c, k_c, vt_c, mqk_c, mv_c, mo_c, g_c = _inproj_call(
            ctx, mod_l, bsz, norm_mix, w_all, layer, qg, kg, bg, cos128, sin128, use_rope=False, tm=tm_c)
        att_l = _attn_call(qt_l, [k_c, k_l], [vt_c, vt_l], tq=tq_l)
        mem_c, mem_l = _mlstm_call((mqk_c, mv_c, mo_c, g_c), (mqk_l, mv_l, mo_l, g_l),
                                   conv_qk, mlstm_norm, layer, emit_ctx=emit_ctx)
        x = _outmlp_call(x, att_l, mem_l, mod_l, None, wo, norm_mlp, w1, w2, layer,
                         None if emit_ctx else gf, tm=tm_l)
        if emit_ctx:
            att_c = _attn_call(qt_c, [k_c], [vt_c], tq=tq_c)
            ctx = _outmlp_call(ctx, att_c, mem_c, mod_l, bsz, wo, norm_mlp, w1, w2, layer, None, tm=tm_c)
    return x
```

```python
import functools

import numpy as np
import jax
import jax.numpy as jnp
from jax import lax
from jax.experimental import pallas as pl
from jax.experimental.pallas import tpu as pltpu

D_MODEL = 1024
N_MOD = 6
GRID_W = 64
ATTN_WIDTH = 512
ATTN_HEADS = 8
HEAD_DIM = 64
KV_HEADS = 2
ATTN_GROUP = ATTN_HEADS // KV_HEADS
KV_WIDTH = KV_HEADS * HEAD_DIM
ROPE_THETA = 10000.0
ROPE_AXIS_DIM = HEAD_DIM // 2
ML_WIDTH = 512
ML_HEADS = 4
ML_DIM = 128
CHUNK = 128
CONV_WIDTH = 3
N_GATES = 4 * ML_HEADS
GATE_ROWS = 8
FFN_DIM = 4 * D_MODEL
NORM_EPS = 1e-6
LOG2_E = 1.4426950408889634

C_MQK = 0
C_Q = C_MQK + 2 * ML_WIDTH
C_MV = C_Q + ATTN_WIDTH
C_MO = C_MV + ML_WIDTH
C_K = C_MO + ML_WIDTH
C_V = C_K + KV_WIDTH
C_G = C_V + KV_WIDTH
W_COLS = C_G + 128

VMEM_LIMIT_BYTES = 56 * 1024 * 1024

BF16 = jnp.bfloat16
F32 = jnp.float32


def _cparams(n_grid, flags=None):
    return pltpu.CompilerParams(
        dimension_semantics=("arbitrary",) * n_grid,
        vmem_limit_bytes=VMEM_LIMIT_BYTES,
        flags=flags)


def _const_spec(shape, index_map):
    return pl.BlockSpec(shape, index_map, pipeline_mode=pl.Buffered(1))


def _aligned(i, m):
    return i if isinstance(i, int) else pl.multiple_of(i, m)


def _dot(a, b):
    return jnp.dot(a, b, preferred_element_type=F32)


def _sigmoid(x):
    return 1.0 / (1.0 + jnp.exp(-x))


def _log_sigmoid(x):
    return jnp.minimum(x, 0.0) - jnp.log(1.0 + jnp.exp(-jnp.abs(x)))


def _group_ones(n, group):
    r = lax.broadcasted_iota(jnp.int32, (n, n), 0) // group
    c = lax.broadcasted_iota(jnp.int32, (n, n), 1) // group
    return jnp.where(r == c, 1.0, 0.0).astype(BF16)


def _split_dot(a, ones_mat):
    hi = a.astype(BF16)
    lo = (a - hi.astype(F32)).astype(BF16)
    return _dot(hi, ones_mat) + _dot(lo, ones_mat)


def _ada_kernel(c_ref, w_ref, b_ref, o_ref):
    c = c_ref[...]
    sc = (c * _sigmoid(c)).astype(BF16)
    o_ref[0] = _dot(sc, w_ref[0].astype(BF16)) + b_ref[0]


def _ada_call(cvec, w_ada, b_ada):
    depth, d, n = w_ada.shape
    rows = cvec.shape[0]
    tn = 1536
    return pl.pallas_call(
        _ada_kernel,
        grid=(depth, n // tn),
        in_specs=[
            pl.BlockSpec((rows, d), lambda l, j: (0, 0)),
            pl.BlockSpec((1, d, tn), lambda l, j: (l, 0, j)),
            pl.BlockSpec((1, 1, tn), lambda l, j: (l, 0, j)),
        ],
        out_specs=pl.BlockSpec((1, rows, tn), lambda l, j: (l, 0, j)),
        out_shape=jax.ShapeDtypeStruct((depth, rows, n), F32),
        compiler_params=_cparams(2),
        name="ada_mod",
    )(cvec, w_ada, b_ada.reshape(depth, 1, n))


def _rms_modulate(x, gain, shift, scale):
    ms = jnp.mean(x * x, axis=-1, keepdims=True)
    y = x * lax.rsqrt(ms + NORM_EPS) * gain
    return y * (1.0 + scale) + shift


def _swap32(x):
    lane = lax.broadcasted_iota(jnp.int32, x.shape, 1)
    up = pltpu.roll(x, 96, 1)
    down = pltpu.roll(x, 32, 1)
    return jnp.where((lane % HEAD_DIM) < ROPE_AXIS_DIM, up, down)


def _head_norm_rope(t, gain, cos, sin, use_rope):
    ss = _dot((t * t).astype(BF16), _group_ones(128, HEAD_DIM))
    tn = t * lax.rsqrt(ss * (1.0 / HEAD_DIM) + NORM_EPS) * gain
    if use_rope:
        tn = tn * cos + _swap32(tn) * sin
    return tn


def _inproj_kernel(x_ref, mod_ref, gain_ref, w_ref, qg_ref, kg_ref, bg_ref, cos_ref, sin_ref,
                   qt_out, k_out, vt_out, mqk_out, mv_out, mo_out, g_out, *, use_rope):
    x = x_ref[...]
    mod = mod_ref[...]
    h = _rms_modulate(x, gain_ref[...], mod[0:1], mod[1:2]).astype(BF16)
    tm = x.shape[0]
    cos = cos_ref[...] if use_rope else None
    sin = sin_ref[...] if use_rope else None

    q = _dot(h, w_ref[:, C_Q:C_Q + ATTN_WIDTH])
    k = _dot(h, w_ref[:, C_K:C_K + KV_WIDTH])
    g = _dot(h, w_ref[:, C_G:C_G + 128])
    mqk_out[...] = _dot(h, w_ref[:, C_MQK:C_MQK + 2 * ML_WIDTH])

    qscale = HEAD_DIM ** -0.5 * LOG2_E
    q_blocks = [_head_norm_rope(q[:, 128 * j:128 * (j + 1)], qg_ref[...], cos, sin, use_rope)
                for j in range(ATTN_WIDTH // 128)]
    k_normed = _head_norm_rope(k, kg_ref[...], cos, sin, use_rope)

    mv_out[...] = _dot(h, w_ref[:, C_MV:C_MV + ML_WIDTH]).astype(BF16)
    mo_out[...] = _sigmoid(_dot(h, w_ref[:, C_MO:C_MO + ML_WIDTH])).astype(BF16)
    vt = _dot(h, w_ref[:, C_V:C_V + KV_WIDTH]).T
    pad_row = lax.broadcasted_iota(jnp.int32, (HEAD_DIM, tm), 0)
    pad = jnp.where(pad_row == 0, 1.0, 0.0)
    for kvh in range(KV_HEADS):
        vt_out[128 * kvh:128 * kvh + HEAD_DIM, :] = vt[HEAD_DIM * kvh:HEAD_DIM * (kvh + 1)].astype(BF16)
        vt_out[128 * kvh + HEAD_DIM:128 * (kvh + 1), :] = pad.astype(BF16)

    for j, blk in enumerate(q_blocks):
        qt_out[128 * j:128 * (j + 1), :] = (blk * qscale).T.astype(BF16)
    k_out[...] = k_normed.astype(BF16)

    gt = g.T[0:ML_HEADS * GATE_ROWS, :] + bg_ref[...]
    row = lax.broadcasted_iota(jnp.int32, gt.shape, 0) % GATE_ROWS
    gt = jnp.where((row == 1) | (row == 3), _log_sigmoid(gt), gt)
    for j in range(tm // CHUNK):
        g_out[j] = gt[:, CHUNK * j:CHUNK * (j + 1)]


def _inproj_call(x, mod, mod_row, gain, w_all, layer, qg, kg, bg, cos, sin, *, use_rope, tm):
    bsz, t, d = x.shape
    nt = t // tm
    nc = t // CHUNK
    cpt = tm // CHUNK
    if mod_row is None:
        mod_map = lambda b, i: (b, 0, 0)
    else:
        mod_map = lambda b, i: (mod_row, 0, 0)
    tok = lambda w: pl.BlockSpec((None, tm, w), lambda b, i: (b, i, 0))
    out_shapes = (
        jax.ShapeDtypeStruct((bsz, ATTN_WIDTH, t), BF16),
        jax.ShapeDtypeStruct((bsz, t, KV_WIDTH), BF16),
        jax.ShapeDtypeStruct((bsz, KV_HEADS * 128, t), BF16),
        jax.ShapeDtypeStruct((bsz, t, 2 * ML_WIDTH), F32),
        jax.ShapeDtypeStruct((bsz, t, ML_WIDTH), BF16),
        jax.ShapeDtypeStruct((bsz, t, ML_WIDTH), BF16),
        jax.ShapeDtypeStruct((bsz, nc, ML_HEADS * GATE_ROWS, CHUNK), F32),
    )
    out_specs = (
        pl.BlockSpec((None, ATTN_WIDTH, tm), lambda b, i: (b, 0, i)),
        tok(KV_WIDTH),
        pl.BlockSpec((None, KV_HEADS * 128, tm), lambda b, i: (b, 0, i)),
        tok(2 * ML_WIDTH),
        tok(ML_WIDTH),
        tok(ML_WIDTH),
        pl.BlockSpec((None, cpt, ML_HEADS * GATE_ROWS, CHUNK), lambda b, i: (b, i, 0, 0)),
    )
    in_specs = [
        tok(d),
        pl.BlockSpec((None, N_MOD, d), mod_map),
        _const_spec((None, 1, d), lambda b, i: (layer, 0, 0)),
        _const_spec((None, d, W_COLS), lambda b, i: (layer, 0, 0)),
        _const_spec((None, 1, 128), lambda b, i: (layer, 0, 0)),
        _const_spec((None, 1, 128), lambda b, i: (layer, 0, 0)),
        _const_spec((None, ML_HEADS * GATE_ROWS, 1), lambda b, i: (layer, 0, 0)),
        pl.BlockSpec((tm, 128), lambda b, i: (i, 0)),
        pl.BlockSpec((tm, 128), lambda b, i: (i, 0)),
    ]
    return pl.pallas_call(
        functools.partial(_inproj_kernel, use_rope=use_rope),
        grid=(bsz, nt),
        in_specs=in_specs,
        out_specs=out_specs,
        out_shape=out_shapes,
        compiler_params=_cparams(2),
        name="inproj_rope" if use_rope else "inproj_ctx",
    )(x, mod, gain, w_all, qg, kg, bg, cos, sin)


def _attn_kernel(*refs, n_seg, q_block):
    qt_ref = refs[0]
    k_refs = refs[1:1 + n_seg]
    vt_refs = refs[1 + n_seg:1 + 2 * n_seg]
    o_ref = refs[1 + 2 * n_seg]
    k_all, vt_all, st_ref = refs[2 + 2 * n_seg:]
    s_tot = k_all.shape[0]
    tq = qt_ref.shape[1]
    cols = ATTN_GROUP * q_block

    @pl.when(pl.program_id(1) == 0)
    def _():
        base = 0
        for seg in range(n_seg):
            s_len = k_refs[seg].shape[0]
            k_all[base:base + s_len, :] = k_refs[seg][...]
            vt_all[:, base:base + s_len] = vt_refs[seg][...]
            base += s_len

    groups = [(qb, kvh) for qb in range(tq // q_block) for kvh in range(KV_HEADS)]
    zeros = jnp.zeros((HEAD_DIM, q_block), BF16)

    def padded_queries(qb, kvh):
        pieces = []
        for g in range(ATTN_GROUP):
            row0 = HEAD_DIM * (ATTN_GROUP * kvh + g)
            qg = qt_ref[row0:row0 + HEAD_DIM, q_block * qb:q_block * (qb + 1)]
            pieces.append(jnp.concatenate([qg, zeros] if kvh == 0 else [zeros, qg], axis=0))
        return jnp.concatenate(pieces, axis=1)

    def finalize(qb, kvh, acc):
        ot = acc[0:HEAD_DIM] * (1.0 / acc[HEAD_DIM:HEAD_DIM + 1])
        for pair in range(ATTN_GROUP // 2):
            two = jnp.concatenate([ot[:, q_block * (2 * pair):q_block * (2 * pair + 1)],
                                   ot[:, q_block * (2 * pair + 1):q_block * (2 * pair + 2)]], axis=0)
            lane0 = HEAD_DIM * (ATTN_GROUP * kvh + 2 * pair)
            o_ref[q_block * qb:q_block * (qb + 1), lane0:lane0 + 2 * HEAD_DIM] = two.T.astype(BF16)

    m_prev = None
    for j in range(len(groups) + 1):
        qz = padded_queries(*groups[j]) if j < len(groups) else None
        m_next = None
        if qz is not None:
            st = _dot(k_all[...], qz)
            st_ref[j % 2] = st
            m_next = jnp.max(jnp.max(st.reshape(s_tot // 8, 8, cols), axis=0), axis=0, keepdims=True)
        if j >= 1:
            qb, kvh = groups[j - 1]
            p = jnp.exp2(st_ref[(j - 1) % 2] - m_prev).astype(BF16)
            finalize(qb, kvh, _dot(vt_all[128 * kvh:128 * (kvh + 1), :], p))
        m_prev = m_next


def _attn_call(qt, ks, vts, *, tq, q_block=128):
    bsz, _, t = qt.shape
    n_seg = len(ks)
    s_tot = sum(k.shape[1] for k in ks)
    q_block = min(q_block, tq)
    cols = ATTN_GROUP * q_block
    in_specs = [pl.BlockSpec((None, ATTN_WIDTH, tq), lambda b, i: (b, 0, i))]
    for k in ks:
        in_specs.append(pl.BlockSpec((None, k.shape[1], KV_WIDTH), lambda b, i: (b, 0, 0)))
    for vt in vts:
        in_specs.append(pl.BlockSpec((None, KV_HEADS * 128, vt.shape[2]), lambda b, i: (b, 0, 0)))
    return pl.pallas_call(
        functools.partial(_attn_kernel, n_seg=n_seg, q_block=q_block),
        grid=(bsz, t // tq),
        in_specs=in_specs,
        out_specs=pl.BlockSpec((None, tq, ATTN_WIDTH), lambda b, i: (b, i, 0)),
        out_shape=jax.ShapeDtypeStruct((bsz, t, ATTN_WIDTH), BF16),
        scratch_shapes=[
            pltpu.VMEM((s_tot, KV_WIDTH), BF16),
            pltpu.VMEM((KV_HEADS * 128, s_tot), BF16),
            pltpu.VMEM((2, s_tot, cols), F32),
        ],
        compiler_params=_cparams(2),
        name="attention_%dseg" % n_seg,
    )(qt, *ks, *vts)


def _mlstm_kernel(*refs, emit_ctx):
    (mq_c, mk_c, mv_c, mo_c, g_c, mq_l, mk_l, mv_l, mo_l, g_l, cw_q, cw_k, gain_ref) = refs[:13]
    if emit_ctx:
        out_c, out_l = refs[13:15]
        scratch = refs[15:]
    else:
        out_c = None
        out_l = refs[13]
        scratch = refs[14:]
    q_s, cum_s, x_s, u_s, pm_s, bc_s, sc_s, hf_s, ct_s, m_s = scratch

    L = CHUNK
    segs = []
    chunk_base = 0
    for (mq, mk, mv, mo, g, out, emit) in ((mq_c, mk_c, mv_c, mo_c, g_c, out_c, emit_ctx),
                                           (mq_l, mk_l, mv_l, mo_l, g_l, out_l, True)):
        n = mq.shape[0] // L
        segs.append(dict(mq=mq, mk=mk, mv=mv, mo=mo, g=g, out=out, emit=emit, n=n, cb=chunk_base))
        chunk_base += n

    ri = lax.broadcasted_iota(jnp.int32, (L, L), 0)
    ci = lax.broadcasted_iota(jnp.int32, (L, L), 1)
    ones_le = jnp.where(ri <= ci, 1.0, 0.0).astype(BF16)
    ones_ge = jnp.where(ri >= ci, 1.0, 0.0).astype(BF16)
    masks = (ci <= ri, ci >= ri)
    row_i = lax.broadcasted_iota(jnp.int32, (L, ML_DIM), 0)
    ones_blk = jnp.ones((L, 128), BF16)
    kscale = ML_DIM ** -0.5

    def unroll_of(n, want):
        while n % want:
            want //= 2
        return want

    def conv_silu(src, cw, c, n):
        t0 = pl.multiple_of(c * L, L)
        xc = src[pl.ds(t0, L), :]
        prev = src[pl.ds(jnp.maximum(t0 - 1, 0), 1), :] * jnp.where(c > 0, 1.0, 0.0)
        nxt = src[pl.ds(jnp.minimum(t0 + L, n * L - 1), 1), :] * jnp.where(c < n - 1, 1.0, 0.0)
        xm = jnp.where(row_i == 0, prev, pltpu.roll(xc, 1, 0))
        xp = jnp.where(row_i == L - 1, nxt, pltpu.roll(xc, L - 1, 0))
        y = xm * cw[0:1, :] + xc * cw[1:2, :] + xp * cw[2:3, :]
        return y * _sigmoid(y)

    for sg in segs:
        n, cb = sg["n"], sg["cb"]
        g2 = sg["g"][...].reshape(n * GATE_ROWS, L) * LOG2_E
        cum_s[0, cb:cb + n] = _split_dot(g2, ones_le).reshape(n, GATE_ROWS, L)
        cum_s[1, cb:cb + n] = _split_dot(g2, ones_ge).reshape(n, GATE_ROWS, L)

    def gate_rows(d, sg, c):
        gr = sg["g"][c] * LOG2_E
        cum = cum_s[d, sg["cb"] + c]
        if d == 0:
            return gr[0:1], gr[1:2], cum[1:2], cum[1:2, L - 1:L]
        return gr[2:3], gr[3:4], cum[3:4], cum[3:4, 0:1]

    def state_free_pass():
        for sg in segs:
            group = unroll_of(sg["n"], 4)

            def body(i, carry, sg=sg, group=group):
                pending = []
                for j in range(group):
                    c = i * group + j
                    cg = sg["cb"] + c
                    t0 = pl.multiple_of(c * L, L)
                    t0g = pl.multiple_of(cg * L, L)
                    qc = conv_silu(sg["mq"], cw_q, c, sg["n"]).astype(BF16)
                    ktc = (conv_silu(sg["mk"], cw_k, c, sg["n"]) * kscale).T.astype(BF16)
                    q_s[pl.ds(t0g, L), :] = qc
                    v_aug = jnp.concatenate([sg["mv"][pl.ds(t0, L), :], ones_blk], axis=1)
                    s0 = _dot(qc, ktc) if sg["emit"] else None
                    kws, es = [], []
                    for d in range(2):
                        ig, lf, brow, total = gate_rows(d, sg, c)
                        a = ig - brow
                        amax = jnp.max(a, axis=-1, keepdims=True)
                        kws.append((ktc.astype(F32) * jnp.exp2(a - amax)).astype(BF16))
                        sc_s[d, cg] = jnp.concatenate(
                            [jnp.broadcast_to(amax, (1, L)), jnp.broadcast_to(total, (1, L)),
                             jnp.zeros((GATE_ROWS - 2, L), F32)], axis=0)
                        if sg["emit"]:
                            a_vis = jnp.where(masks[d], a, -jnp.inf)
                            pm = jnp.broadcast_to(jnp.max(a_vis, axis=-1, keepdims=True), (L, L))
                            pm_s[d, cg] = pm
                            bc_s[d, cg] = jnp.broadcast_to(
                                jnp.sum(jnp.where(masks[d], lf, 0.0), axis=-1, keepdims=True), (L, L))
                            es.append(jnp.exp2(a_vis - pm))
                    pending.append((cg, s0, kws, es, v_aug))
                for cg, s0, kws, es, v_aug in pending:
                    lhs = jnp.concatenate(kws + [(s0 * e).astype(BF16) for e in es], axis=0)
                    res = _dot(lhs, v_aug)
                    for d in range(2):
                        u_s[d, cg] = res[L * d:L * (d + 1)]
                        if es:
                            x_s[d, cg] = res[L * (2 + d):L * (3 + d)]
                return carry
            lax.fori_loop(0, sg["n"] // group, body, 0)

    def sequential_pass(d):
        ct_s[...] = jnp.zeros(ct_s.shape, F32)
        m_s[...] = jnp.zeros(m_s.shape, F32)
        for sg in segs:
            n = sg["n"]
            group = unroll_of(n, 4)

            def body(i, carry, sg=sg, n=n, group=group):
                m_prev = m_s[0:1, 0:1]
                ct = ct_s[...]
                pending = []
                for j in range(group):
                    c = i * group + j if d == 0 else n - 1 - (i * group + j)
                    cg = sg["cb"] + c
                    t0 = pl.multiple_of(c * L, L)
                    t0g = pl.multiple_of(cg * L, L)
                    sc = sc_s[d, cg]
                    amax, total = sc[0:1, 0:1], sc[1:2, 0:1]
                    m_last = jnp.maximum(m_prev, amax)
                    if sg["emit"]:
                        inter = _dot(q_s[pl.ds(t0g, L), :], ct.astype(BF16))
                        pending.append((cg, t0, t0g, inter, m_prev))
                    ct = jnp.exp2(m_prev - m_last) * ct + jnp.exp2(amax - m_last) * u_s[d, cg]
                    m_prev = total + m_last
                ct_s[...] = ct
                m_s[...] = jnp.broadcast_to(m_prev, m_s.shape)
                for cg, t0, t0g, inter, m_in in pending:
                    pm = pm_s[d, cg]
                    mx = jnp.maximum(pm, m_in)
                    r = jnp.exp2(pm - mx)
                    wi = jnp.exp2(m_in - mx)
                    x = x_s[d, cg]
                    num = r * x[:, 0:ML_DIM] + wi * inter[:, 0:ML_DIM]
                    den = r * x[:, ML_DIM:] + wi * inter[:, ML_DIM:]
                    h = num / jnp.maximum(jnp.abs(den), jnp.exp2(-bc_s[d, cg] - mx))
                    if d == 0:
                        hf_s[pl.ds(t0g, L), :] = h
                    else:
                        hsum = hf_s[pl.ds(t0g, L), :] + h
                        ms = jnp.mean(hsum * hsum, axis=-1, keepdims=True)
                        hn = hsum * lax.rsqrt(ms + NORM_EPS) * gain_ref[...]
                        sg["out"][pl.ds(t0, L), :] = (sg["mo"][pl.ds(t0, L), :].astype(F32) * hn).astype(BF16)
                return carry

            lax.fori_loop(0, n // group, body, 0)

    state_free_pass()
    for d in range(2):
        sequential_pass(d)


def _mlstm_call(ctx_p, lat_p, conv_w, ml_gain, layer, *, emit_ctx):
    bsz = lat_p[0].shape[0]
    in_specs = []
    args = []
    t_tot = 0
    for (mqk, mv, mo, g) in (ctx_p, lat_p):
        t = mqk.shape[1]
        n = t // CHUNK
        t_tot += t
        in_specs += [
            pl.BlockSpec((None, t, ML_DIM), lambda b, h: (b, 0, h)),
            pl.BlockSpec((None, t, ML_DIM), lambda b, h: (b, 0, ML_HEADS + h)),
            pl.BlockSpec((None, t, ML_DIM), lambda b, h: (b, 0, h)),
            pl.BlockSpec((None, t, ML_DIM), lambda b, h: (b, 0, h)),
            pl.BlockSpec((None, n, GATE_ROWS, CHUNK), lambda b, h: (b, 0, h, 0)),
        ]
        args += [mqk, mqk, mv, mo, g]
    in_specs += [
        pl.BlockSpec((None, CONV_WIDTH, ML_DIM), lambda b, h: (layer, 0, h)),
        pl.BlockSpec((None, CONV_WIDTH, ML_DIM), lambda b, h: (layer, 0, ML_HEADS + h)),
        pl.BlockSpec((None, 1, ML_DIM), lambda b, h: (layer, 0, h)),
    ]
    args += [conv_w, conv_w, ml_gain]
    t_c, t_l = ctx_p[0].shape[1], lat_p[0].shape[1]
    out_l_shape = jax.ShapeDtypeStruct((bsz, t_l, ML_WIDTH), BF16)
    out_l_spec = pl.BlockSpec((None, t_l, ML_DIM), lambda b, h: (b, 0, h))
    if emit_ctx:
        out_shape = (jax.ShapeDtypeStruct((bsz, t_c, ML_WIDTH), BF16), out_l_shape)
        out_specs = (pl.BlockSpec((None, t_c, ML_DIM), lambda b, h: (b, 0, h)), out_l_spec)
    else:
        out_shape = out_l_shape
        out_specs = out_l_spec
    n_tot = t_tot // CHUNK
    res = pl.pallas_call(
        functools.partial(_mlstm_kernel, emit_ctx=emit_ctx),
        grid=(bsz, ML_HEADS),
        in_specs=in_specs,
        out_specs=out_specs,
        out_shape=out_shape,
        scratch_shapes=[
            pltpu.VMEM((t_tot, ML_DIM), BF16),
            pltpu.VMEM((2, n_tot, GATE_ROWS, CHUNK), F32),
            pltpu.VMEM((2, n_tot, CHUNK, 2 * ML_DIM), F32),
            pltpu.VMEM((2, n_tot, ML_DIM, 2 * ML_DIM), F32),
            pltpu.VMEM((2, n_tot, CHUNK, 128), F32),
            pltpu.VMEM((2, n_tot, CHUNK, 128), F32),
            pltpu.VMEM((2, n_tot, GATE_ROWS, CHUNK), F32),
            pltpu.VMEM((t_tot, ML_DIM), F32),
            pltpu.VMEM((ML_DIM, 2 * ML_DIM), F32),
            pltpu.VMEM((8, 128), F32),
        ],
        compiler_params=_cparams(2),
        name="mlstm_emit" if emit_ctx else "mlstm_last",
    )(*args)
    if emit_ctx:
        return res
    return None, res


def _outmlp_kernel(*refs, final, ffn_chunk):
    x_ref, att_ref, mem_ref, mod_ref, wo_ref, gain_ref, w1_ref, w2_ref = refs[:8]
    if final:
        gf_ref, o_ref = refs[8:]
    else:
        o_ref = refs[8]
    mod = mod_ref[...]
    y = _dot(att_ref[...], wo_ref[0:ATTN_WIDTH, :]) + _dot(mem_ref[...], wo_ref[ATTN_WIDTH:D_MODEL, :])
    x1 = x_ref[...] + mod[2:3] * y
    h = _rms_modulate(x1, gain_ref[...], mod[3:4], mod[4:5]).astype(BF16)
    acc = jnp.zeros(x1.shape, F32)
    for j in range(FFN_DIM // ffn_chunk):
        f = jnp.maximum(_dot(h, w1_ref[:, ffn_chunk * j:ffn_chunk * (j + 1)]), 0.0)
        acc = acc + _dot((f * f).astype(BF16), w2_ref[ffn_chunk * j:ffn_chunk * (j + 1), :])
    x2 = x1 + mod[5:6] * acc
    if final:
        ms = jnp.mean(x2 * x2, axis=-1, keepdims=True)
        x2 = x2 * lax.rsqrt(ms + NORM_EPS) * gf_ref[...]
    o_ref[...] = x2


def _outmlp_call(x, att, mem, mod, mod_row, wo, gain, w1, w2, layer, gf, *, tm, ffn_chunk=1024):
    bsz, t, d = x.shape
    final = gf is not None
    if mod_row is None:
        mod_map = lambda b, i: (b, 0, 0)
    else:
        mod_map = lambda b, i: (mod_row, 0, 0)
    in_specs = [
        pl.BlockSpec((None, tm, d), lambda b, i: (b, i, 0)),
        pl.BlockSpec((None, tm, ATTN_WIDTH), lambda b, i: (b, i, 0)),
        pl.BlockSpec((None, tm, ML_WIDTH), lambda b, i: (b, i, 0)),
        pl.BlockSpec((None, N_MOD, d), mod_map),
        _const_spec((None, d, d), lambda b, i: (layer, 0, 0)),
        _const_spec((None, 1, d), lambda b, i: (layer, 0, 0)),
        _const_spec((None, d, FFN_DIM), lambda b, i: (layer, 0, 0)),
        _const_spec((None, FFN_DIM, d), lambda b, i: (layer, 0, 0)),
    ]
    args = [x, att, mem, mod, wo, gain, w1, w2]
    if final:
        in_specs.append(_const_spec((1, d), lambda b, i: (0, 0)))
        args.append(gf)
    return pl.pallas_call(
        functools.partial(_outmlp_kernel, final=final, ffn_chunk=ffn_chunk),
        grid=(bsz, t // tm),
        in_specs=in_specs,
        out_specs=pl.BlockSpec((None, tm, d), lambda b, i: (b, i, 0)),
        out_shape=jax.ShapeDtypeStruct((bsz, t, d), F32),
        compiler_params=_cparams(2),
        name="outproj_mlp_final" if final else "outproj_mlp",
    )(*args)


def _deinterleave(a, heads):
    lead = a.shape[:-1]
    return a.reshape(*lead, heads, HEAD_DIM // 2, 2).swapaxes(-1, -2).reshape(*lead, heads * HEAD_DIM)


def _relayout_projection(w_in):
    depth, d, _ = w_in.shape
    w = w_in.astype(BF16)
    a_q, a_k, a_v = 0, ATTN_WIDTH, ATTN_WIDTH + KV_WIDTH
    m_qk = ATTN_WIDTH + 2 * KV_WIDTH
    m_v = m_qk + 2 * ML_WIDTH
    m_o = m_v + ML_WIDTH
    gates = m_o + ML_WIDTH
    g = w[..., gates:gates + N_GATES].reshape(depth, d, 4, ML_HEADS).swapaxes(-1, -2)
    g = jnp.pad(g, ((0, 0), (0, 0), (0, 0), (0, GATE_ROWS - 4))).reshape(depth, d, ML_HEADS * GATE_ROWS)
    g = jnp.pad(g, ((0, 0), (0, 0), (0, 128 - ML_HEADS * GATE_ROWS)))
    return jnp.concatenate([
        w[..., m_qk:m_qk + 2 * ML_WIDTH],
        _deinterleave(w[..., a_q:a_q + ATTN_WIDTH], ATTN_HEADS),
        w[..., m_v:m_v + ML_WIDTH],
        w[..., m_o:m_o + ML_WIDTH],
        _deinterleave(w[..., a_k:a_k + KV_WIDTH], KV_HEADS),
        w[..., a_v:a_v + KV_WIDTH],
        g], axis=-1)


def _rope_tables(t):
    rows = t // GRID_W
    row_idx = jnp.repeat(jnp.arange(rows, dtype=F32), GRID_W)
    col_idx = jnp.tile(jnp.arange(GRID_W, dtype=F32), rows)
    inv_freq = jnp.power(ROPE_THETA, -jnp.arange(0, ROPE_AXIS_DIM, 2, dtype=F32) / ROPE_AXIS_DIM)
    ang = jnp.concatenate([row_idx[:, None] * inv_freq, col_idx[:, None] * inv_freq], axis=-1)
    cos, sin = jnp.cos(ang), jnp.sin(ang)
    cos128 = jnp.tile(cos, (1, 4))
    sin128 = jnp.tile(jnp.concatenate([-sin, sin], axis=-1), (1, 2))
    return cos128, sin128


def _pick_tile(t, pref):
    tm = min(pref, t)
    while t % tm:
        tm //= 2
    return tm


def kernel(x, c, ctx, c_ctx, w_ada, b_ada, norm_mix, norm_mlp, w_in, b_gates, conv_qk,
           q_norm, k_norm, mlstm_norm, w_out, w_mlp_in, w_mlp_out, norm_final):
    bsz, t, d = x.shape
    t_c = ctx.shape[1]
    depth = w_ada.shape[0]
    assert d == D_MODEL and t % CHUNK == 0 and t_c % CHUNK == 0 and t % GRID_W == 0

    rows = -(-(bsz + 1) // 8) * 8
    cvec = jnp.zeros((rows, d), F32).at[:bsz].set(c).at[bsz].set(c_ctx)
    mod = _ada_call(cvec, w_ada, b_ada).reshape(depth, rows, N_MOD, d)

    w_all = _relayout_projection(w_in)
    assert w_all.shape[-1] == W_COLS
    qg = jnp.tile(_deinterleave(q_norm, 1), (1, 2)).reshape(depth, 1, 128)
    kg = jnp.tile(_deinterleave(k_norm, 1), (1, 2)).reshape(depth, 1, 128)
    norm_mix = norm_mix.reshape(depth, 1, d)
    norm_mlp = norm_mlp.reshape(depth, 1, d)
    mlstm_norm = mlstm_norm.reshape(depth, 1, ML_WIDTH)
    bg = jnp.zeros((depth, ML_HEADS, GATE_ROWS), F32).at[:, :, 0:4].set(
        b_gates.reshape(depth, 4, ML_HEADS).transpose(0, 2, 1)).reshape(depth, ML_HEADS * GATE_ROWS, 1)
    wo = w_out.astype(BF16)
    w1 = w_mlp_in.astype(BF16)
    w2 = w_mlp_out.astype(BF16)
    cos128, sin128 = _rope_tables(t)
    gf = norm_final.reshape(1, d)

    tm_l = _pick_tile(t, 512)
    tm_c = _pick_tile(t_c, 512)
    tq_l = _pick_tile(t, 512)
    tq_c = _pick_tile(t_c, 256)

    for layer in range(depth):
        emit_ctx = layer < depth - 1
        mod_l = mod[layer]
        qt_l, k_l, vt_l, mqk_l, mv_l, mo_l, g_l = _inproj_call(
            x, mod_l, None, norm_mix, w_all, layer, qg, kg, bg, cos128, sin128, use_rope=True, tm=tm_l)
        qt_c, k_c, vt_c, mqk_c, mv_c, mo_c, g_c = _inproj_call(
            ctx, mod_l, bsz, norm_mix, w_all, layer, qg, kg, bg, cos128, sin128, use_rope=False, tm=tm_c)
        att_l = _attn_call(qt_l, [k_c, k_l], [vt_c, vt_l], tq=tq_l)
        mem_c, mem_l = _mlstm_call((mqk_c, mv_c, mo_c, g_c), (mqk_l, mv_l, mo_l, g_l),
                                   conv_qk, mlstm_norm, layer, emit_ctx=emit_ctx)
        x = _outmlp_call(x, att_l, mem_l, mod_l, None, wo, norm_mlp, w1, w2, layer,
                         None if emit_ctx else gf, tm=tm_l)
        if emit_ctx:
            att_c = _attn_call(qt_c, [k_c], [vt_c], tq=tq_c)
            ctx = _outmlp_call(ctx, att_c, mem_c, mod_l, bsz, wo, norm_mlp, w1, w2, layer, None, tm=tm_c)
    return x
```

```python
import functools

import jax
import jax.numpy as jnp
from jax import lax
from jax.experimental import pallas as pl
from jax.experimental.pallas import tpu as pltpu

D_MODEL = 1024
N_MOD = 6
GRID_W = 64
ATTN_WIDTH = 512
ATTN_HEADS = 8
HEAD_DIM = 64
KV_HEADS = 2
ATTN_GROUP = ATTN_HEADS // KV_HEADS
KV_WIDTH = KV_HEADS * HEAD_DIM
ROPE_THETA = 10000.0
ROPE_AXIS_DIM = HEAD_DIM // 2
ML_WIDTH = 512
ML_HEADS = 4
ML_DIM = 128
CHUNK = 128
CONV_WIDTH = 3
N_GATES = 4 * ML_HEADS
GATE_ROWS = 8
FFN_DIM = 4 * D_MODEL
NORM_EPS = 1e-6
LOG2_E = 1.4426950408889634

C_MQK = 0
C_Q = C_MQK + 2 * ML_WIDTH
C_MV = C_Q + ATTN_WIDTH
C_MO = C_MV + ML_WIDTH
C_K = C_MO + ML_WIDTH
C_V = C_K + KV_WIDTH
C_G = C_V + KV_WIDTH
W_COLS = C_G + 128

VMEM_LIMIT_BYTES = 56 * 1024 * 1024

BF16 = jnp.bfloat16
F32 = jnp.float32


def _cparams(n_grid):
    return pltpu.CompilerParams(
        dimension_semantics=("arbitrary",) * n_grid,
        vmem_limit_bytes=VMEM_LIMIT_BYTES)


def _const_spec(shape, index_map):
    return pl.BlockSpec(shape, index_map, pipeline_mode=pl.Buffered(1))


def _dot(a, b):
    return jnp.dot(a, b, preferred_element_type=F32)


def _sigmoid(x):
    return 1.0 / (1.0 + jnp.exp(-x))


def _log_sigmoid(x):
    return jnp.minimum(x, 0.0) - jnp.log(1.0 + jnp.exp(-jnp.abs(x)))


def _group_ones(n, group):
    r = lax.broadcasted_iota(jnp.int32, (n, n), 0) // group
    c = lax.broadcasted_iota(jnp.int32, (n, n), 1) // group
    return jnp.where(r == c, 1.0, 0.0).astype(BF16)


def _split_dot(a, ones_mat):
    hi = a.astype(BF16)
    lo = (a - hi.astype(F32)).astype(BF16)
    return _dot(hi, ones_mat) + _dot(lo, ones_mat)


def _ada_kernel(c_ref, w_ref, b_ref, o_ref):
    c = c_ref[...]
    sc = (c * _sigmoid(c)).astype(BF16)
    o_ref[0] = _dot(sc, w_ref[0].astype(BF16)) + b_ref[0]


def _ada_call(cvec, w_ada, b_ada):
    depth, d, n = w_ada.shape
    rows = cvec.shape[0]
    tn = 1536
    return pl.pallas_call(
        _ada_kernel,
        grid=(depth, n // tn),
        in_specs=[
            pl.BlockSpec((rows, d), lambda l, j: (0, 0)),
            pl.BlockSpec((1, d, tn), lambda l, j: (l, 0, j)),
            pl.BlockSpec((1, 1, tn), lambda l, j: (l, 0, j)),
        ],
        out_specs=pl.BlockSpec((1, rows, tn), lambda l, j: (l, 0, j)),
        out_shape=jax.ShapeDtypeStruct((depth, rows, n), F32),
        compiler_params=_cparams(2),
        name="ada_mod",
    )(cvec, w_ada, b_ada.reshape(depth, 1, n))


def _rms_modulate(x, gain, shift, scale):
    ms = jnp.mean(x * x, axis=-1, keepdims=True)
    y = x * lax.rsqrt(ms + NORM_EPS) * gain
    return y * (1.0 + scale) + shift


def _swap32(x):
    lane = lax.broadcasted_iota(jnp.int32, x.shape, 1)
    up = pltpu.roll(x, 96, 1)
    down = pltpu.roll(x, 32, 1)
    return jnp.where((lane % HEAD_DIM) < ROPE_AXIS_DIM, up, down)


def _head_norm_rope(t, gain, cos, sin, use_rope):
    ss = _dot((t * t).astype(BF16), _group_ones(128, HEAD_DIM))
    tn = t * lax.rsqrt(ss * (1.0 / HEAD_DIM) + NORM_EPS) * gain
    if use_rope:
        tn = tn * cos + _swap32(tn) * sin
    return tn


def _inproj_kernel(x_ref, mod_ref, gain_ref, w_ref, qg_ref, kg_ref, bg_ref, cos_ref, sin_ref,
                   qt_out, k_out, vt_out, mqk_out, mv_out, mo_out, g_out, *, use_rope):
    x = x_ref[...]
    mod = mod_ref[...]
    h = _rms_modulate(x, gain_ref[...], mod[0:1], mod[1:2]).astype(BF16)
    tm = x.shape[0]
    cos = cos_ref[...] if use_rope else None
    sin = sin_ref[...] if use_rope else None

    q = _dot(h, w_ref[:, C_Q:C_Q + ATTN_WIDTH])
    k = _dot(h, w_ref[:, C_K:C_K + KV_WIDTH])
    g = _dot(h, w_ref[:, C_G:C_G + 128])
    mqk_out[...] = _dot(h, w_ref[:, C_MQK:C_MQK + 2 * ML_WIDTH])

    qscale = HEAD_DIM ** -0.5 * LOG2_E
    q_blocks = [_head_norm_rope(q[:, 128 * j:128 * (j + 1)], qg_ref[...], cos, sin, use_rope)
                for j in range(ATTN_WIDTH // 128)]
    k_normed = _head_norm_rope(k, kg_ref[...], cos, sin, use_rope)

    mv_out[...] = _dot(h, w_ref[:, C_MV:C_MV + ML_WIDTH]).astype(BF16)
    mo_out[...] = _sigmoid(_dot(h, w_ref[:, C_MO:C_MO + ML_WIDTH])).astype(BF16)
    vt = _dot(h, w_ref[:, C_V:C_V + KV_WIDTH]).T
    pad_row = lax.broadcasted_iota(jnp.int32, (HEAD_DIM, tm), 0)
    pad = jnp.where(pad_row == 0, 1.0, 0.0)
    for kvh in range(KV_HEADS):
        vt_out[128 * kvh:128 * kvh + HEAD_DIM, :] = vt[HEAD_DIM * kvh:HEAD_DIM * (kvh + 1)].astype(BF16)
        vt_out[128 * kvh + HEAD_DIM:128 * (kvh + 1), :] = pad.astype(BF16)

    for j, blk in enumerate(q_blocks):
        qt_out[128 * j:128 * (j + 1), :] = (blk * qscale).T.astype(BF16)
    k_out[...] = k_normed.astype(BF16)

    gt = g.T[0:ML_HEADS * GATE_ROWS, :] + bg_ref[...]
    row = lax.broadcasted_iota(jnp.int32, gt.shape, 0) % GATE_ROWS
    gt = jnp.where((row == 1) | (row == 3), _log_sigmoid(gt), gt)
    for j in range(tm // CHUNK):
        g_out[j] = gt[:, CHUNK * j:CHUNK * (j + 1)]


def _inproj_call(x, mod, mod_row, gain, w_all, layer, qg, kg, bg, cos, sin, *, use_rope, tm):
    bsz, t, d = x.shape
    nt = t // tm
    nc = t // CHUNK
    cpt = tm // CHUNK
    if mod_row is None:
        mod_map = lambda b, i: (b, 0, 0)
    else:
        mod_map = lambda b, i: (mod_row, 0, 0)
    tok = lambda w: pl.BlockSpec((None, tm, w), lambda b, i: (b, i, 0))
    out_shapes = (
        jax.ShapeDtypeStruct((bsz, ATTN_WIDTH, t), BF16),
        jax.ShapeDtypeStruct((bsz, t, KV_WIDTH), BF16),
        jax.ShapeDtypeStruct((bsz, KV_HEADS * 128, t), BF16),
        jax.ShapeDtypeStruct((bsz, t, 2 * ML_WIDTH), F32),
        jax.ShapeDtypeStruct((bsz, t, ML_WIDTH), BF16),
        jax.ShapeDtypeStruct((bsz, t, ML_WIDTH), BF16),
        jax.ShapeDtypeStruct((bsz, nc, ML_HEADS * GATE_ROWS, CHUNK), F32),
    )
    out_specs = (
        pl.BlockSpec((None, ATTN_WIDTH, tm), lambda b, i: (b, 0, i)),
        tok(KV_WIDTH),
        pl.BlockSpec((None, KV_HEADS * 128, tm), lambda b, i: (b, 0, i)),
        tok(2 * ML_WIDTH),
        tok(ML_WIDTH),
        tok(ML_WIDTH),
        pl.BlockSpec((None, cpt, ML_HEADS * GATE_ROWS, CHUNK), lambda b, i: (b, i, 0, 0)),
    )
    in_specs = [
        tok(d),
        pl.BlockSpec((None, N_MOD, d), mod_map),
        _const_spec((None, 1, d), lambda b, i: (layer, 0, 0)),
        _const_spec((None, d, W_COLS), lambda b, i: (layer, 0, 0)),
        _const_spec((None, 1, 128), lambda b, i: (layer, 0, 0)),
        _const_spec((None, 1, 128), lambda b, i: (layer, 0, 0)),
        _const_spec((None, ML_HEADS * GATE_ROWS, 1), lambda b, i: (layer, 0, 0)),
        pl.BlockSpec((tm, 128), lambda b, i: (i, 0)),
        pl.BlockSpec((tm, 128), lambda b, i: (i, 0)),
    ]
    return pl.pallas_call(
        functools.partial(_inproj_kernel, use_rope=use_rope),
        grid=(bsz, nt),
        in_specs=in_specs,
        out_specs=out_specs,
        out_shape=out_shapes,
        compiler_params=_cparams(2),
        name="inproj_rope" if use_rope else "inproj_ctx",
    )(x, mod, gain, w_all, qg, kg, bg, cos, sin)


def _attn_kernel(*refs, n_seg, q_block):
    qt_ref = refs[0]
    k_refs = refs[1:1 + n_seg]
    vt_refs = refs[1 + n_seg:1 + 2 * n_seg]
    o_ref = refs[1 + 2 * n_seg]
    k_all, vt_all, st_ref = refs[2 + 2 * n_seg:]
    s_tot = k_all.shape[0]
    tq = qt_ref.shape[1]
    cols = ATTN_GROUP * q_block

    @pl.when(pl.program_id(1) == 0)
    def _():
        base = 0
        for seg in range(n_seg):
            s_len = k_refs[seg].shape[0]
            k_all[base:base + s_len, :] = k_refs[seg][...]
            vt_all[:, base:base + s_len] = vt_refs[seg][...]
            base += s_len

    groups = [(qb, kvh) for qb in range(tq // q_block) for kvh in range(KV_HEADS)]
    zeros = jnp.zeros((HEAD_DIM, q_block), BF16)

    def padded_queries(qb, kvh):
        pieces = []
        for g in range(ATTN_GROUP):
            row0 = HEAD_DIM * (ATTN_GROUP * kvh + g)
            qg = qt_ref[row0:row0 + HEAD_DIM, q_block * qb:q_block * (qb + 1)]
            pieces.append(jnp.concatenate([qg, zeros] if kvh == 0 else [zeros, qg], axis=0))
        return jnp.concatenate(pieces, axis=1)

    def finalize(qb, kvh, acc):
        ot = acc[0:HEAD_DIM] * (1.0 / acc[HEAD_DIM:HEAD_DIM + 1])
        for pair in range(ATTN_GROUP // 2):
            two = jnp.concatenate([ot[:, q_block * (2 * pair):q_block * (2 * pair + 1)],
                                   ot[:, q_block * (2 * pair + 1):q_block * (2 * pair + 2)]], axis=0)
            lane0 = HEAD_DIM * (ATTN_GROUP * kvh + 2 * pair)
            o_ref[q_block * qb:q_block * (qb + 1), lane0:lane0 + 2 * HEAD_DIM] = two.T.astype(BF16)

    m_prev = None
    for j in range(len(groups) + 1):
        qz = padded_queries(*groups[j]) if j < len(groups) else None
        m_next = None
        if qz is not None:
            st = _dot(k_all[...], qz)
            st_ref[j % 2] = st
            m_next = jnp.max(jnp.max(st.reshape(s_tot // 8, 8, cols), axis=0), axis=0, keepdims=True)
        if j >= 1:
            qb, kvh = groups[j - 1]
            p = jnp.exp2(st_ref[(j - 1) % 2] - m_prev).astype(BF16)
            finalize(qb, kvh, _dot(vt_all[128 * kvh:128 * (kvh + 1), :], p))
        m_prev = m_next


def _attn_call(qt, ks, vts, *, tq, q_block=128):
    bsz, _, t = qt.shape
    n_seg = len(ks)
    s_tot = sum(k.shape[1] for k in ks)
    q_block = min(q_block, tq)
    cols = ATTN_GROUP * q_block
    in_specs = [pl.BlockSpec((None, ATTN_WIDTH, tq), lambda b, i: (b, 0, i))]
    for k in ks:
        in_specs.append(pl.BlockSpec((None, k.shape[1], KV_WIDTH), lambda b, i: (b, 0, 0)))
    for vt in vts:
        in_specs.append(pl.BlockSpec((None, KV_HEADS * 128, vt.shape[2]), lambda b, i: (b, 0, 0)))
    return pl.pallas_call(
        functools.partial(_attn_kernel, n_seg=n_seg, q_block=q_block),
        grid=(bsz, t // tq),
        in_specs=in_specs,
        out_specs=pl.BlockSpec((None, tq, ATTN_WIDTH), lambda b, i: (b, i, 0)),
        out_shape=jax.ShapeDtypeStruct((bsz, t, ATTN_WIDTH), BF16),
        scratch_shapes=[
            pltpu.VMEM((s_tot, KV_WIDTH), BF16),
            pltpu.VMEM((KV_HEADS * 128, s_tot), BF16),
            pltpu.VMEM((2, s_tot, cols), F32),
        ],
        compiler_params=_cparams(2),
        name="attention_%dseg" % n_seg,
    )(qt, *ks, *vts)


def _mlstm_kernel(*refs, emit_ctx):
    (mq_c, mk_c, mv_c, mo_c, g_c, mq_l, mk_l, mv_l, mo_l, g_l, cw_q, cw_k, gain_ref) = refs[:13]
    if emit_ctx:
        out_c, out_l = refs[13:15]
        scratch = refs[15:]
    else:
        out_c = None
        out_l = refs[13]
        scratch = refs[14:]
    q_s, cum_s, x_s, u_s, pm_s, nb_s, sc_s, hf_s, ct_s, m_s = scratch

    L = CHUNK
    segs = []
    chunk_base = 0
    for (mq, mk, mv, mo, g, out, emit) in ((mq_c, mk_c, mv_c, mo_c, g_c, out_c, emit_ctx),
                                           (mq_l, mk_l, mv_l, mo_l, g_l, out_l, True)):
        n = mq.shape[0] // L
        segs.append(dict(mq=mq, mk=mk, mv=mv, mo=mo, g=g, out=out, emit=emit, n=n, cb=chunk_base))
        chunk_base += n

    ri = lax.broadcasted_iota(jnp.int32, (L, L), 0)
    ci = lax.broadcasted_iota(jnp.int32, (L, L), 1)
    ones_le = jnp.where(ri <= ci, 1.0, 0.0).astype(BF16)
    ones_ge = jnp.where(ri >= ci, 1.0, 0.0).astype(BF16)
    masks = (ci <= ri, ci >= ri)
    row_i = lax.broadcasted_iota(jnp.int32, (L, ML_DIM), 0)
    ones_blk = jnp.ones((L, 128), BF16)
    kscale = ML_DIM ** -0.5

    def unroll_of(n, want):
        while n % want:
            want //= 2
        return want

    def conv_silu(src, cw, c, n):
        t0 = pl.multiple_of(c * L, L)
        xc = src[pl.ds(t0, L), :]
        prev = src[pl.ds(jnp.maximum(t0 - 1, 0), 1), :] * jnp.where(c > 0, 1.0, 0.0)
        nxt = src[pl.ds(jnp.minimum(t0 + L, n * L - 1), 1), :] * jnp.where(c < n - 1, 1.0, 0.0)
        xm = jnp.where(row_i == 0, prev, pltpu.roll(xc, 1, 0))
        xp = jnp.where(row_i == L - 1, nxt, pltpu.roll(xc, L - 1, 0))
        y = xm * cw[0:1, :] + xc * cw[1:2, :] + xp * cw[2:3, :]
        return y * _sigmoid(y)

    for sg in segs:
        n, cb = sg["n"], sg["cb"]
        g2 = sg["g"][...].reshape(n * GATE_ROWS, L) * LOG2_E
        cum_s[0, cb:cb + n] = _split_dot(g2, ones_le).reshape(n, GATE_ROWS, L)
        cum_s[1, cb:cb + n] = _split_dot(g2, ones_ge).reshape(n, GATE_ROWS, L)

    def gate_rows(d, sg, c):
        gr = sg["g"][c] * LOG2_E
        cum = cum_s[d, sg["cb"] + c]
        if d == 0:
            return gr[0:1], gr[1:2], cum[1:2], cum[1:2, L - 1:L]
        return gr[2:3], gr[3:4], cum[3:4], cum[3:4, 0:1]

    def state_free_pass():
        for sg in segs:
            group = unroll_of(sg["n"], 4)

            def body(i, carry, sg=sg, group=group):
                pending = []
                for j in range(group):
                    c = i * group + j
                    cg = sg["cb"] + c
                    t0 = pl.multiple_of(c * L, L)
                    t0g = pl.multiple_of(cg * L, L)
                    qc = conv_silu(sg["mq"], cw_q, c, sg["n"]).astype(BF16)
                    ktc = (conv_silu(sg["mk"], cw_k, c, sg["n"]) * kscale).T.astype(BF16)
                    q_s[pl.ds(t0g, L), :] = qc
                    v_aug = jnp.concatenate([sg["mv"][pl.ds(t0, L), :], ones_blk], axis=1)
                    s0 = _dot(qc, ktc) if sg["emit"] else None
                    kws, es = [], []
                    for d in range(2):
                        ig, lf, brow, total = gate_rows(d, sg, c)
                        a = ig - brow
                        amax = jnp.max(a, axis=-1, keepdims=True)
                        kws.append((ktc.astype(F32) * jnp.exp2(a - amax)).astype(BF16))
                        sc_s[d, cg] = jnp.concatenate(
                            [jnp.broadcast_to(amax, (1, L)), jnp.broadcast_to(total, (1, L)),
                             jnp.zeros((GATE_ROWS - 2, L), F32)], axis=0)
                        if sg["emit"]:
                            a_vis = jnp.where(masks[d], a, -jnp.inf)
                            pm = jnp.broadcast_to(jnp.max(a_vis, axis=-1, keepdims=True), (L, L))
                            pm_s[d, cg] = pm
                            bcol = jnp.sum(jnp.where(masks[d], lf, 0.0), axis=-1, keepdims=True)
                            nb_s[d, cg] = -jnp.broadcast_to(bcol, (L, L)) - pm
                            es.append(jnp.exp2(a_vis - pm))
                    pending.append((cg, s0, kws, es, v_aug))
                for cg, s0, kws, es, v_aug in pending:
                    lhs = jnp.concatenate(kws + [(s0 * e).astype(BF16) for e in es], axis=0)
                    res = _dot(lhs, v_aug)
                    for d in range(2):
                        u_s[d, cg] = res[L * d:L * (d + 1)]
                        if es:
                            x_s[d, cg] = res[L * (2 + d):L * (3 + d)]
                return carry
            lax.fori_loop(0, sg["n"] // group, body, 0)

    def sequential_pass(d):
        ct_s[...] = jnp.zeros(ct_s.shape, F32)
        m_s[...] = jnp.zeros(m_s.shape, F32)
        for sg in segs:
            n = sg["n"]
            group = unroll_of(n, 8)

            def body(i, carry, sg=sg, n=n, group=group):
                m_prev = m_s[0:1, 0:1]
                ct = ct_s[...]
                pending = []
                for j in range(group):
                    c = i * group + j if d == 0 else n - 1 - (i * group + j)
                    cg = sg["cb"] + c
                    t0 = pl.multiple_of(c * L, L)
                    t0g = pl.multiple_of(cg * L, L)
                    sc = sc_s[d, cg]
                    amax, total = sc[0:1, 0:1], sc[1:2, 0:1]
                    m_last = jnp.maximum(m_prev, amax)
                    if sg["emit"]:
                        inter = _dot(q_s[pl.ds(t0g, L), :], ct.astype(BF16))
                        pending.append((cg, t0, t0g, inter, m_prev))
                    ct = jnp.exp2(m_prev - m_last) * ct + jnp.exp2(amax - m_last) * u_s[d, cg]
                    m_prev = total + m_last
                ct_s[...] = ct
                m_s[...] = jnp.broadcast_to(m_prev, m_s.shape)
                for cg, t0, t0g, inter, m_in in pending:
                    pm = pm_s[d, cg]
                    mx = jnp.maximum(pm, m_in)
                    dpm = pm - mx
                    r = jnp.exp2(dpm)
                    wi = jnp.exp2(m_in - mx)
                    x = x_s[d, cg]
                    num = r * x[:, 0:ML_DIM] + wi * inter[:, 0:ML_DIM]
                    den = r * x[:, ML_DIM:] + wi * inter[:, ML_DIM:]
                    h = num / jnp.maximum(jnp.abs(den), jnp.exp2(nb_s[d, cg] + dpm))
                    if d == 0:
                        hf_s[pl.ds(t0g, L), :] = h
                    else:
                        hsum = hf_s[pl.ds(t0g, L), :] + h
                        ms = jnp.mean(hsum * hsum, axis=-1, keepdims=True)
                        hn = hsum * lax.rsqrt(ms + NORM_EPS) * gain_ref[...]
                        sg["out"][pl.ds(t0, L), :] = (sg["mo"][pl.ds(t0, L), :].astype(F32) * hn).astype(BF16)
                return carry

            lax.fori_loop(0, n // group, body, 0)

    state_free_pass()
    for d in range(2):
        sequential_pass(d)


def _mlstm_call(ctx_p, lat_p, conv_w, ml_gain, layer, *, emit_ctx):
    bsz = lat_p[0].shape[0]
    in_specs = []
    args = []
    t_tot = 0
    for (mqk, mv, mo, g) in (ctx_p, lat_p):
        t = mqk.shape[1]
        n = t // CHUNK
        t_tot += t
        in_specs += [
            pl.BlockSpec((None, t, ML_DIM), lambda b, h: (b, 0, h)),
            pl.BlockSpec((None, t, ML_DIM), lambda b, h: (b, 0, ML_HEADS + h)),
            pl.BlockSpec((None, t, ML_DIM), lambda b, h: (b, 0, h)),
            pl.BlockSpec((None, t, ML_DIM), lambda b, h: (b, 0, h)),
            pl.BlockSpec((None, n, GATE_ROWS, CHUNK), lambda b, h: (b, 0, h, 0)),
        ]
        args += [mqk, mqk, mv, mo, g]
    in_specs += [
        pl.BlockSpec((None, CONV_WIDTH, ML_DIM), lambda b, h: (layer, 0, h)),
        pl.BlockSpec((None, CONV_WIDTH, ML_DIM), lambda b, h: (layer, 0, ML_HEADS + h)),
        pl.BlockSpec((None, 1, ML_DIM), lambda b, h: (layer, 0, h)),
    ]
    args += [conv_w, conv_w, ml_gain]
    t_c, t_l = ctx_p[0].shape[1], lat_p[0].shape[1]
    out_l_shape = jax.ShapeDtypeStruct((bsz, t_l, ML_WIDTH), BF16)
    out_l_spec = pl.BlockSpec((None, t_l, ML_DIM), lambda b, h: (b, 0, h))
    if emit_ctx:
        out_shape = (jax.ShapeDtypeStruct((bsz, t_c, ML_WIDTH), BF16), out_l_shape)
        out_specs = (pl.BlockSpec((None, t_c, ML_DIM), lambda b, h: (b, 0, h)), out_l_spec)
    else:
        out_shape = out_l_shape
        out_specs = out_l_spec
    n_tot = t_tot // CHUNK
    res = pl.pallas_call(
        functools.partial(_mlstm_kernel, emit_ctx=emit_ctx),
        grid=(bsz, ML_HEADS),
        in_specs=in_specs,
        out_specs=out_specs,
        out_shape=out_shape,
        scratch_shapes=[
            pltpu.VMEM((t_tot, ML_DIM), BF16),
            pltpu.VMEM((2, n_tot, GATE_ROWS, CHUNK), F32),
            pltpu.VMEM((2, n_tot, CHUNK, 2 * ML_DIM), F32),
            pltpu.VMEM((2, n_tot, ML_DIM, 2 * ML_DIM), F32),
            pltpu.VMEM((2, n_tot, CHUNK, 128), F32),
            pltpu.VMEM((2, n_tot, CHUNK, 128), F32),
            pltpu.VMEM((2, n_tot, GATE_ROWS, CHUNK), F32),
            pltpu.VMEM((t_tot, ML_DIM), F32),
            pltpu.VMEM((ML_DIM, 2 * ML_DIM), F32),
            pltpu.VMEM((8, 128), F32),
        ],
        compiler_params=_cparams(2),
        name="mlstm_emit" if emit_ctx else "mlstm_last",
    )(*args)
    if emit_ctx:
        return res
    return None, res


def _outmlp_kernel(*refs, final, ffn_chunk):
    x_ref, att_ref, mem_ref, mod_ref, wo_ref, gain_ref, w1_ref, w2_ref = refs[:8]
    if final:
        gf_ref, o_ref = refs[8:]
    else:
        o_ref = refs[8]
    mod = mod_ref[...]
    y = _dot(att_ref[...], wo_ref[0:ATTN_WIDTH, :]) + _dot(mem_ref[...], wo_ref[ATTN_WIDTH:D_MODEL, :])
    x1 = x_ref[...] + mod[2:3] * y
    h = _rms_modulate(x1, gain_ref[...], mod[3:4], mod[4:5]).astype(BF16)
    acc = jnp.zeros(x1.shape, F32)
    for j in range(FFN_DIM // ffn_chunk):
        f = jnp.maximum(_dot(h, w1_ref[:, ffn_chunk * j:ffn_chunk * (j + 1)]), 0.0)
        acc = acc + _dot((f * f).astype(BF16), w2_ref[ffn_chunk * j:ffn_chunk * (j + 1), :])
    x2 = x1 + mod[5:6] * acc
    if final:
        ms = jnp.mean(x2 * x2, axis=-1, keepdims=True)
        x2 = x2 * lax.rsqrt(ms + NORM_EPS) * gf_ref[...]
    o_ref[...] = x2


def _outmlp_call(x, att, mem, mod, mod_row, wo, gain, w1, w2, layer, gf, *, tm, ffn_chunk=1024):
    bsz, t, d = x.shape
    final = gf is not None
    if mod_row is None:
        mod_map = lambda b, i: (b, 0, 0)
    else:
        mod_map = lambda b, i: (mod_row, 0, 0)
    in_specs = [
        pl.BlockSpec((None, tm, d), lambda b, i: (b, i, 0)),
        pl.BlockSpec((None, tm, ATTN_WIDTH), lambda b, i: (b, i, 0)),
        pl.BlockSpec((None, tm, ML_WIDTH), lambda b, i: (b, i, 0)),
        pl.BlockSpec((None, N_MOD, d), mod_map),
        _const_spec((None, d, d), lambda b, i: (layer, 0, 0)),
        _const_spec((None, 1, d), lambda b, i: (layer, 0, 0)),
        _const_spec((None, d, FFN_DIM), lambda b, i: (layer, 0, 0)),
        _const_spec((None, FFN_DIM, d), lambda b, i: (layer, 0, 0)),
    ]
    args = [x, att, mem, mod, wo, gain, w1, w2]
    if final:
        in_specs.append(_const_spec((1, d), lambda b, i: (0, 0)))
        args.append(gf)
    return pl.pallas_call(
        functools.partial(_outmlp_kernel, final=final, ffn_chunk=ffn_chunk),
        grid=(bsz, t // tm),
        in_specs=in_specs,
        out_specs=pl.BlockSpec((None, tm, d), lambda b, i: (b, i, 0)),
        out_shape=jax.ShapeDtypeStruct((bsz, t, d), F32),
        compiler_params=_cparams(2),
        name="outproj_mlp_final" if final else "outproj_mlp",
    )(*args)


def _deinterleave(a, heads):
    lead = a.shape[:-1]
    return a.reshape(*lead, heads, HEAD_DIM // 2, 2).swapaxes(-1, -2).reshape(*lead, heads * HEAD_DIM)


def _relayout_projection(w_in):
    depth, d, _ = w_in.shape
    w = w_in.astype(BF16)
    a_q, a_k, a_v = 0, ATTN_WIDTH, ATTN_WIDTH + KV_WIDTH
    m_qk = ATTN_WIDTH + 2 * KV_WIDTH
    m_v = m_qk + 2 * ML_WIDTH
    m_o = m_v + ML_WIDTH
    gates = m_o + ML_WIDTH
    g = w[..., gates:gates + N_GATES].reshape(depth, d, 4, ML_HEADS).swapaxes(-1, -2)
    g = jnp.pad(g, ((0, 0), (0, 0), (0, 0), (0, GATE_ROWS - 4))).reshape(depth, d, ML_HEADS * GATE_ROWS)
    g = jnp.pad(g, ((0, 0), (0, 0), (0, 128 - ML_HEADS * GATE_ROWS)))
    return jnp.concatenate([
        w[..., m_qk:m_qk + 2 * ML_WIDTH],
        _deinterleave(w[..., a_q:a_q + ATTN_WIDTH], ATTN_HEADS),
        w[..., m_v:m_v + ML_WIDTH],
        w[..., m_o:m_o + ML_WIDTH],
        _deinterleave(w[..., a_k:a_k + KV_WIDTH], KV_HEADS),
        w[..., a_v:a_v + KV_WIDTH],
        g], axis=-1)


def _rope_tables(t):
    rows = t // GRID_W
    row_idx = jnp.repeat(jnp.arange(rows, dtype=F32), GRID_W)
    col_idx = jnp.tile(jnp.arange(GRID_W, dtype=F32), rows)
    inv_freq = jnp.power(ROPE_THETA, -jnp.arange(0, ROPE_AXIS_DIM, 2, dtype=F32) / ROPE_AXIS_DIM)
    ang = jnp.concatenate([row_idx[:, None] * inv_freq, col_idx[:, None] * inv_freq], axis=-1)
    cos, sin = jnp.cos(ang), jnp.sin(ang)
    cos128 = jnp.tile(cos, (1, 4))
    sin128 = jnp.tile(jnp.concatenate([-sin, sin], axis=-1), (1, 2))
    return cos128, sin128


def _pick_tile(t, pref):
    tm = min(pref, t)
    while t % tm:
        tm //= 2
    return tm


def kernel(x, c, ctx, c_ctx, w_ada, b_ada, norm_mix, norm_mlp, w_in, b_gates, conv_qk,
           q_norm, k_norm, mlstm_norm, w_out, w_mlp_in, w_mlp_out, norm_final):
    bsz, t, d = x.shape
    t_c = ctx.shape[1]
    depth = w_ada.shape[0]
    assert d == D_MODEL and t % CHUNK == 0 and t_c % CHUNK == 0 and t % GRID_W == 0

    rows = -(-(bsz + 1) // 8) * 8
    cvec = jnp.zeros((rows, d), F32).at[:bsz].set(c).at[bsz].set(c_ctx)
    mod = _ada_call(cvec, w_ada, b_ada).reshape(depth, rows, N_MOD, d)

    w_all = _relayout_projection(w_in)
    assert w_all.shape[-1] == W_COLS
    qg = jnp.tile(_deinterleave(q_norm, 1), (1, 2)).reshape(depth, 1, 128)
    kg = jnp.tile(_deinterleave(k_norm, 1), (1, 2)).reshape(depth, 1, 128)
    norm_mix = norm_mix.reshape(depth, 1, d)
    norm_mlp = norm_mlp.reshape(depth, 1, d)
    mlstm_norm = mlstm_norm.reshape(depth, 1, ML_WIDTH)
    bg = jnp.zeros((depth, ML_HEADS, GATE_ROWS), F32).at[:, :, 0:4].set(
        b_gates.reshape(depth, 4, ML_HEADS).transpose(0, 2, 1)).reshape(depth, ML_HEADS * GATE_ROWS, 1)
    wo = w_out.astype(BF16)
    w1 = w_mlp_in.astype(BF16)
    w2 = w_mlp_out.astype(BF16)
    cos128, sin128 = _rope_tables(t)
    gf = norm_final.reshape(1, d)

    tm_l = _pick_tile(t, 512)
    tm_c = _pick_tile(t_c, 512)
    tq_l = _pick_tile(t, 512)
    tq_c = _pick_tile(t_c, 256)

    for layer in range(depth):
        emit_ctx = layer < depth - 1
        mod_l = mod[layer]
        qt_l, k_l, vt_l, mqk_l, mv_l, mo_l, g_l = _inproj_call(
            x, mod_l, None, norm_mix, w_all, layer, qg, kg, bg, cos128, sin128, use_rope=True, tm=tm_l)
        qt_c, k_c, vt_c, mqk_c, mv_c, mo_c, g_c = _inproj_call(
            ctx, mod_l, bsz, norm_mix, w_all, layer, qg, kg, bg, cos128, sin128, use_rope=False, tm=tm_c)
        att_l = _attn_call(qt_l, [k_c, k_l], [vt_c, vt_l], tq=tq_l)
        mem_c, mem_l = _mlstm_call((mqk_c, mv_c, mo_c, g_c), (mqk_l, mv_l, mo_l, g_l),
                                   conv_qk, mlstm_norm, layer, emit_ctx=emit_ctx)
        x = _outmlp_call(x, att_l, mem_l, mod_l, None, wo, norm_mlp, w1, w2, layer,
                         None if emit_ctx else gf, tm=tm_l)
        if emit_ctx:
            att_c = _attn_call(qt_c, [k_c], [vt_c], tq=tq_c)
            ctx = _outmlp_call(ctx, att_c, mem_c, mod_l, bsz, wo, norm_mlp, w1, w2, layer, None, tm=tm_c)
    return x
```

```python
import functools

import jax
import jax.numpy as jnp
from jax import lax
from jax.experimental import pallas as pl
from jax.experimental.pallas import tpu as pltpu

D_MODEL = 1024
N_MOD = 6
GRID_W = 64
ATTN_WIDTH = 512
ATTN_HEADS = 8
HEAD_DIM = 64
KV_HEADS = 2
ATTN_GROUP = ATTN_HEADS // KV_HEADS
KV_WIDTH = KV_HEADS * HEAD_DIM
ROPE_THETA = 10000.0
ROPE_AXIS_DIM = HEAD_DIM // 2
ML_WIDTH = 512
ML_HEADS = 4
ML_DIM = 128
CHUNK = 128
CONV_WIDTH = 3
N_GATES = 4 * ML_HEADS
GATE_ROWS = 8
FFN_DIM = 4 * D_MODEL
NORM_EPS = 1e-6
LOG2_E = 1.4426950408889634

C_MQK = 0
C_Q = C_MQK + 2 * ML_WIDTH
C_MV = C_Q + ATTN_WIDTH
C_MO = C_MV + ML_WIDTH
C_K = C_MO + ML_WIDTH
C_V = C_K + KV_WIDTH
C_G = C_V + KV_WIDTH
W_COLS = C_G + 128

VMEM_LIMIT_BYTES = 56 * 1024 * 1024

BF16 = jnp.bfloat16
F32 = jnp.float32


def _cparams(n_grid):
    return pltpu.CompilerParams(
        dimension_semantics=("arbitrary",) * n_grid,
        vmem_limit_bytes=VMEM_LIMIT_BYTES)


def _const_spec(shape, index_map):
    return pl.BlockSpec(shape, index_map, pipeline_mode=pl.Buffered(1))


def _dot(a, b):
    return jnp.dot(a, b, preferred_element_type=F32)


def _sigmoid(x):
    return 1.0 / (1.0 + jnp.exp(-x))


def _log_sigmoid(x):
    return jnp.minimum(x, 0.0) - jnp.log(1.0 + jnp.exp(-jnp.abs(x)))


def _group_ones(n, group):
    r = lax.broadcasted_iota(jnp.int32, (n, n), 0) // group
    c = lax.broadcasted_iota(jnp.int32, (n, n), 1) // group
    return jnp.where(r == c, 1.0, 0.0).astype(BF16)


def _split_dot(a, ones_mat):
    hi = a.astype(BF16)
    lo = (a - hi.astype(F32)).astype(BF16)
    return _dot(hi, ones_mat) + _dot(lo, ones_mat)


def _ada_kernel(c_ref, w_ref, b_ref, o_ref):
    c = c_ref[...]
    sc = (c * _sigmoid(c)).astype(BF16)
    o_ref[0] = _dot(sc, w_ref[0].astype(BF16)) + b_ref[0]


def _ada_call(cvec, w_ada, b_ada):
    depth, d, n = w_ada.shape
    rows = cvec.shape[0]
    tn = n // 4
    return pl.pallas_call(
        _ada_kernel,
        grid=(depth, n // tn),
        in_specs=[
            pl.BlockSpec((rows, d), lambda l, j: (0, 0)),
            pl.BlockSpec((1, d, tn), lambda l, j: (l, 0, j)),
            pl.BlockSpec((1, 1, tn), lambda l, j: (l, 0, j)),
        ],
        out_specs=pl.BlockSpec((1, rows, tn), lambda l, j: (l, 0, j)),
        out_shape=jax.ShapeDtypeStruct((depth, rows, n), F32),
        compiler_params=_cparams(2),
        name="ada_mod",
    )(cvec, w_ada, b_ada.reshape(depth, 1, n))


def _rms_modulate(x, gain, shift, scale):
    ms = jnp.mean(x * x, axis=-1, keepdims=True)
    y = x * lax.rsqrt(ms + NORM_EPS) * gain
    return y * (1.0 + scale) + shift


def _swap32(x):
    lane = lax.broadcasted_iota(jnp.int32, x.shape, 1)
    up = pltpu.roll(x, 96, 1)
    down = pltpu.roll(x, 32, 1)
    return jnp.where((lane % HEAD_DIM) < ROPE_AXIS_DIM, up, down)


def _head_sumsq(t):
    return _dot((t * t).astype(BF16), _group_ones(t.shape[1], HEAD_DIM))


def _head_norm_rope(t, ss, gain, cos, sin, use_rope):
    tn = t * lax.rsqrt(ss * (1.0 / HEAD_DIM) + NORM_EPS) * gain
    if use_rope:
        tn = tn * cos + _swap32(tn) * sin
    return tn


def _inproj_kernel(x_ref, mod_ref, gain_ref, w_ref, qg_ref, kg_ref, bg_ref, cos_ref, sin_ref,
                   qt_out, k_out, vt_out, mqk_out, mv_out, mo_out, g_out, *, use_rope):
    x = x_ref[...]
    mod = mod_ref[...]
    h = _rms_modulate(x, gain_ref[...], mod[0:1], mod[1:2]).astype(BF16)
    tm = x.shape[0]
    cos = cos_ref[...] if use_rope else None
    sin = sin_ref[...] if use_rope else None

    q = _dot(h, w_ref[:, C_Q:C_Q + ATTN_WIDTH])
    kvg = _dot(h, w_ref[:, C_K:W_COLS])
    k, v, g = kvg[:, 0:KV_WIDTH], kvg[:, KV_WIDTH:2 * KV_WIDTH], kvg[:, 2 * KV_WIDTH:]
    mqk_out[...] = _dot(h, w_ref[:, C_MQK:C_MQK + 2 * ML_WIDTH])

    qscale = HEAD_DIM ** -0.5 * LOG2_E
    q_ss = [_head_sumsq(q[:, 256 * j:256 * (j + 1)]) for j in range(ATTN_WIDTH // 256)]
    k_ss = _head_sumsq(k)
    q_blocks = [_head_norm_rope(q[:, 128 * j:128 * (j + 1)], q_ss[j // 2][:, 128 * (j % 2):128 * (j % 2 + 1)],
                                qg_ref[...], cos, sin, use_rope)
                for j in range(ATTN_WIDTH // 128)]
    k_normed = _head_norm_rope(k, k_ss, kg_ref[...], cos, sin, use_rope)

    mv_out[...] = _dot(h, w_ref[:, C_MV:C_MV + ML_WIDTH]).astype(BF16)
    mo_out[...] = _sigmoid(_dot(h, w_ref[:, C_MO:C_MO + ML_WIDTH])).astype(BF16)
    vt = v.T
    pad_row = lax.broadcasted_iota(jnp.int32, (HEAD_DIM, tm), 0)
    pad = jnp.where(pad_row == 0, 1.0, 0.0)
    for kvh in range(KV_HEADS):
        vt_out[128 * kvh:128 * kvh + HEAD_DIM, :] = vt[HEAD_DIM * kvh:HEAD_DIM * (kvh + 1)].astype(BF16)
        vt_out[128 * kvh + HEAD_DIM:128 * (kvh + 1), :] = pad.astype(BF16)

    for j, blk in enumerate(q_blocks):
        qt_out[128 * j:128 * (j + 1), :] = (blk * qscale).T.astype(BF16)
    k_out[...] = k_normed.astype(BF16)

    gt = g.T[0:ML_HEADS * GATE_ROWS, :] + bg_ref[...]
    row = lax.broadcasted_iota(jnp.int32, gt.shape, 0) % GATE_ROWS
    gt = jnp.where((row == 1) | (row == 3), _log_sigmoid(gt), gt)
    for j in range(tm // CHUNK):
        g_out[j] = gt[:, CHUNK * j:CHUNK * (j + 1)]


def _inproj_call(x, mod, mod_row, gain, w_all, layer, qg, kg, bg, cos, sin, *, use_rope, tm):
    bsz, t, d = x.shape
    nt = t // tm
    nc = t // CHUNK
    cpt = tm // CHUNK
    if mod_row is None:
        mod_map = lambda b, i: (b, 0, 0)
    else:
        mod_map = lambda b, i: (mod_row, 0, 0)
    tok = lambda w: pl.BlockSpec((None, tm, w), lambda b, i: (b, i, 0))
    out_shapes = (
        jax.ShapeDtypeStruct((bsz, ATTN_WIDTH, t), BF16),
        jax.ShapeDtypeStruct((bsz, t, KV_WIDTH), BF16),
        jax.ShapeDtypeStruct((bsz, KV_HEADS * 128, t), BF16),
        jax.ShapeDtypeStruct((bsz, t, 2 * ML_WIDTH), F32),
        jax.ShapeDtypeStruct((bsz, t, ML_WIDTH), BF16),
        jax.ShapeDtypeStruct((bsz, t, ML_WIDTH), BF16),
        jax.ShapeDtypeStruct((bsz, nc, ML_HEADS * GATE_ROWS, CHUNK), F32),
    )
    out_specs = (
        pl.BlockSpec((None, ATTN_WIDTH, tm), lambda b, i: (b, 0, i)),
        tok(KV_WIDTH),
        pl.BlockSpec((None, KV_HEADS * 128, tm), lambda b, i: (b, 0, i)),
        tok(2 * ML_WIDTH),
        tok(ML_WIDTH),
        tok(ML_WIDTH),
        pl.BlockSpec((None, cpt, ML_HEADS * GATE_ROWS, CHUNK), lambda b, i: (b, i, 0, 0)),
    )
    in_specs = [
        tok(d),
        pl.BlockSpec((None, N_MOD, d), mod_map),
        _const_spec((None, 1, d), lambda b, i: (layer, 0, 0)),
        _const_spec((None, d, W_COLS), lambda b, i: (layer, 0, 0)),
        _const_spec((None, 1, 128), lambda b, i: (layer, 0, 0)),
        _const_spec((None, 1, 128), lambda b, i: (layer, 0, 0)),
        _const_spec((None, ML_HEADS * GATE_ROWS, 1), lambda b, i: (layer, 0, 0)),
        pl.BlockSpec((tm, 128), lambda b, i: (i, 0)),
        pl.BlockSpec((tm, 128), lambda b, i: (i, 0)),
    ]
    return pl.pallas_call(
        functools.partial(_inproj_kernel, use_rope=use_rope),
        grid=(bsz, nt),
        in_specs=in_specs,
        out_specs=out_specs,
        out_shape=out_shapes,
        compiler_params=_cparams(2),
        name="inproj_rope" if use_rope else "inproj_ctx",
    )(x, mod, gain, w_all, qg, kg, bg, cos, sin)


def _attn_kernel(*refs, n_seg, q_block):
    qt_ref = refs[0]
    k_refs = refs[1:1 + n_seg]
    vt_refs = refs[1 + n_seg:1 + 2 * n_seg]
    o_ref = refs[1 + 2 * n_seg]
    k_all, vt_all, st_ref = refs[2 + 2 * n_seg:]
    s_tot = k_all.shape[0]
    tq = qt_ref.shape[1]
    cols = ATTN_GROUP * q_block

    @pl.when(pl.program_id(1) == 0)
    def _():
        base = 0
        for seg in range(n_seg):
            s_len = k_refs[seg].shape[0]
            k_all[base:base + s_len, :] = k_refs[seg][...]
            vt_all[:, base:base + s_len] = vt_refs[seg][...]
            base += s_len

    groups = [(qb, kvh) for qb in range(tq // q_block) for kvh in range(KV_HEADS)]
    zeros = jnp.zeros((HEAD_DIM, q_block), BF16)

    def padded_queries(qb, kvh):
        pieces = []
        for g in range(ATTN_GROUP):
            row0 = HEAD_DIM * (ATTN_GROUP * kvh + g)
            qg = qt_ref[row0:row0 + HEAD_DIM, q_block * qb:q_block * (qb + 1)]
            pieces.append(jnp.concatenate([qg, zeros] if kvh == 0 else [zeros, qg], axis=0))
        return jnp.concatenate(pieces, axis=1)

    def finalize(qb, kvh, acc):
        ot = acc[0:HEAD_DIM] * (1.0 / acc[HEAD_DIM:HEAD_DIM + 1])
        for pair in range(ATTN_GROUP // 2):
            two = jnp.concatenate([ot[:, q_block * (2 * pair):q_block * (2 * pair + 1)],
                                   ot[:, q_block * (2 * pair + 1):q_block * (2 * pair + 2)]], axis=0)
            lane0 = HEAD_DIM * (ATTN_GROUP * kvh + 2 * pair)
            o_ref[q_block * qb:q_block * (qb + 1), lane0:lane0 + 2 * HEAD_DIM] = two.T.astype(BF16)

    m_prev = None
    for j in range(len(groups) + 1):
        qz = padded_queries(*groups[j]) if j < len(groups) else None
        m_next = None
        if qz is not None:
            st = _dot(k_all[...], qz)
            st_ref[j % 2] = st
            m_next = jnp.max(jnp.max(st.reshape(s_tot // 8, 8, cols), axis=0), axis=0, keepdims=True)
        if j >= 1:
            qb, kvh = groups[j - 1]
            p = jnp.exp2(st_ref[(j - 1) % 2] - m_prev).astype(BF16)
            finalize(qb, kvh, _dot(vt_all[128 * kvh:128 * (kvh + 1), :], p))
        m_prev = m_next


def _attn_call(qt, ks, vts, *, tq, q_block=128):
    bsz, _, t = qt.shape
    n_seg = len(ks)
    s_tot = sum(k.shape[1] for k in ks)
    q_block = min(q_block, tq)
    cols = ATTN_GROUP * q_block
    in_specs = [pl.BlockSpec((None, ATTN_WIDTH, tq), lambda b, i: (b, 0, i))]
    for k in ks:
        in_specs.append(pl.BlockSpec((None, k.shape[1], KV_WIDTH), lambda b, i: (b, 0, 0)))
    for vt in vts:
        in_specs.append(pl.BlockSpec((None, KV_HEADS * 128, vt.shape[2]), lambda b, i: (b, 0, 0)))
    return pl.pallas_call(
        functools.partial(_attn_kernel, n_seg=n_seg, q_block=q_block),
        grid=(bsz, t // tq),
        in_specs=in_specs,
        out_specs=pl.BlockSpec((None, tq, ATTN_WIDTH), lambda b, i: (b, i, 0)),
        out_shape=jax.ShapeDtypeStruct((bsz, t, ATTN_WIDTH), BF16),
        scratch_shapes=[
            pltpu.VMEM((s_tot, KV_WIDTH), BF16),
            pltpu.VMEM((KV_HEADS * 128, s_tot), BF16),
            pltpu.VMEM((2, s_tot, cols), F32),
        ],
        compiler_params=_cparams(2),
        name="attention_%dseg" % n_seg,
    )(qt, *ks, *vts)


def _mlstm_kernel(*refs, emit_ctx):
    (mq_c, mk_c, mv_c, mo_c, g_c, mq_l, mk_l, mv_l, mo_l, g_l, cw_q, cw_k, gain_ref) = refs[:13]
    if emit_ctx:
        out_c, out_l = refs[13:15]
        scratch = refs[15:]
    else:
        out_c = None
        out_l = refs[13]
        scratch = refs[14:]
    q_s, cum_s, x_s, u_s, pm_s, nb_s, sc_s, hf_s, ct_s, m_s = scratch

    L = CHUNK
    segs = []
    chunk_base = 0
    for (mq, mk, mv, mo, g, out, emit) in ((mq_c, mk_c, mv_c, mo_c, g_c, out_c, emit_ctx),
                                           (mq_l, mk_l, mv_l, mo_l, g_l, out_l, True)):
        n = mq.shape[0] // L
        segs.append(dict(mq=mq, mk=mk, mv=mv, mo=mo, g=g, out=out, emit=emit, n=n, cb=chunk_base))
        chunk_base += n

    ri = lax.broadcasted_iota(jnp.int32, (L, L), 0)
    ci = lax.broadcasted_iota(jnp.int32, (L, L), 1)
    ones_le = jnp.where(ri <= ci, 1.0, 0.0).astype(BF16)
    ones_ge = jnp.where(ri >= ci, 1.0, 0.0).astype(BF16)
    masks = (ci <= ri, ci >= ri)
    row_i = lax.broadcasted_iota(jnp.int32, (L, ML_DIM), 0)
    ones_blk = jnp.ones((L, 128), BF16)
    kscale = ML_DIM ** -0.5

    def unroll_of(n, want):
        while n % want:
            want //= 2
        return want

    def conv_silu(src, cw, c, n):
        t0 = pl.multiple_of(c * L, L)
        xc = src[pl.ds(t0, L), :]
        prev = src[pl.ds(jnp.maximum(t0 - 1, 0), 1), :] * jnp.where(c > 0, 1.0, 0.0)
        nxt = src[pl.ds(jnp.minimum(t0 + L, n * L - 1), 1), :] * jnp.where(c < n - 1, 1.0, 0.0)
        xm = jnp.where(row_i == 0, prev, pltpu.roll(xc, 1, 0))
        xp = jnp.where(row_i == L - 1, nxt, pltpu.roll(xc, L - 1, 0))
        y = xm * cw[0:1, :] + xc * cw[1:2, :] + xp * cw[2:3, :]
        return y * _sigmoid(y)

    for sg in segs:
        n, cb = sg["n"], sg["cb"]
        g2 = sg["g"][...].reshape(n * GATE_ROWS, L) * LOG2_E
        cum_s[0, cb:cb + n] = _split_dot(g2, ones_le).reshape(n, GATE_ROWS, L)
        cum_s[1, cb:cb + n] = _split_dot(g2, ones_ge).reshape(n, GATE_ROWS, L)

    def gate_rows(d, sg, c):
        gr = sg["g"][c] * LOG2_E
        cum = cum_s[d, sg["cb"] + c]
        if d == 0:
            return gr[0:1], gr[1:2], cum[1:2], cum[1:2, L - 1:L]
        return gr[2:3], gr[3:4], cum[3:4], cum[3:4, 0:1]

    def state_free_pass():
        for sg in segs:
            group = unroll_of(sg["n"], 4)

            def body(i, carry, sg=sg, group=group):
                pending = []
                for j in range(group):
                    c = i * group + j
                    cg = sg["cb"] + c
                    t0 = pl.multiple_of(c * L, L)
                    t0g = pl.multiple_of(cg * L, L)
                    qc = conv_silu(sg["mq"], cw_q, c, sg["n"]).astype(BF16)
                    ktc = (conv_silu(sg["mk"], cw_k, c, sg["n"]) * kscale).T.astype(BF16)
                    q_s[pl.ds(t0g, L), :] = qc
                    v_aug = jnp.concatenate([sg["mv"][pl.ds(t0, L), :], ones_blk], axis=1)
                    s0 = _dot(qc, ktc) if sg["emit"] else None
                    kws, es = [], []
                    for d in range(2):
                        ig, lf, brow, total = gate_rows(d, sg, c)
                        a = ig - brow
                        amax = jnp.max(a, axis=-1, keepdims=True)
                        kws.append((ktc.astype(F32) * jnp.exp2(a - amax)).astype(BF16))
                        sc_s[d, cg] = jnp.concatenate(
                            [jnp.broadcast_to(amax, (1, L)), jnp.broadcast_to(total, (1, L)),
                             jnp.zeros((GATE_ROWS - 2, L), F32)], axis=0)
                        if sg["emit"]:
                            a_vis = jnp.where(masks[d], a, -jnp.inf)
                            pm = jnp.broadcast_to(jnp.max(a_vis, axis=-1, keepdims=True), (L, L))
                            pm_s[d, cg] = pm
                            bcol = jnp.sum(jnp.where(masks[d], lf, 0.0), axis=-1, keepdims=True)
                            nb_s[d, cg] = -jnp.broadcast_to(bcol, (L, L)) - pm
                            es.append(jnp.exp2(a_vis - pm))
                    pending.append((cg, s0, kws, es, v_aug))
                for cg, s0, kws, es, v_aug in pending:
                    lhs = jnp.concatenate(kws + [(s0 * e).astype(BF16) for e in es], axis=0)
                    res = _dot(lhs, v_aug)
                    for d in range(2):
                        u_s[d, cg] = res[L * d:L * (d + 1)]
                        if es:
                            x_s[d, cg] = res[L * (2 + d):L * (3 + d)]
                return carry
            lax.fori_loop(0, sg["n"] // group, body, 0)

    def sequential_pass(d):
        ct_s[...] = jnp.zeros(ct_s.shape, F32)
        m_s[...] = jnp.zeros(m_s.shape, F32)
        for sg in segs:
            n = sg["n"]
            group = unroll_of(n, 8)

            def body(i, carry, sg=sg, n=n, group=group):
                m_prev = m_s[0:1, 0:1]
                ct = ct_s[...]
                pending = []
                for j in range(group):
                    c = i * group + j if d == 0 else n - 1 - (i * group + j)
                    cg = sg["cb"] + c
                    t0 = pl.multiple_of(c * L, L)
                    t0g = pl.multiple_of(cg * L, L)
                    sc = sc_s[d, cg]
                    amax, total = sc[0:1, 0:1], sc[1:2, 0:1]
                    m_last = jnp.maximum(m_prev, amax)
                    if sg["emit"]:
                        inter = _dot(q_s[pl.ds(t0g, L), :], ct.astype(BF16))
                        pending.append((cg, t0, t0g, inter, m_prev))
                    ct = jnp.exp2(m_prev - m_last) * ct + jnp.exp2(amax - m_last) * u_s[d, cg]
                    m_prev = total + m_last
                ct_s[...] = ct
                m_s[...] = jnp.broadcast_to(m_prev, m_s.shape)
                for cg, t0, t0g, inter, m_in in pending:
                    pm = pm_s[d, cg]
                    mx = jnp.maximum(pm, m_in)
                    dpm = pm - mx
                    r = jnp.exp2(dpm)
                    wi = jnp.exp2(m_in - mx)
                    x = x_s[d, cg]
                    num = r * x[:, 0:ML_DIM] + wi * inter[:, 0:ML_DIM]
                    den = r * x[:, ML_DIM:] + wi * inter[:, ML_DIM:]
                    h = num / jnp.maximum(jnp.abs(den), jnp.exp2(nb_s[d, cg] + dpm))
                    if d == 0:
                        hf_s[pl.ds(t0g, L), :] = h
                    else:
                        hsum = hf_s[pl.ds(t0g, L), :] + h
                        ms = jnp.mean(hsum * hsum, axis=-1, keepdims=True)
                        hn = hsum * lax.rsqrt(ms + NORM_EPS) * gain_ref[...]
                        sg["out"][pl.ds(t0, L), :] = (sg["mo"][pl.ds(t0, L), :].astype(F32) * hn).astype(BF16)
                return carry

            lax.fori_loop(0, n // group, body, 0)

    state_free_pass()
    for d in range(2):
        sequential_pass(d)


def _mlstm_call(ctx_p, lat_p, conv_w, ml_gain, layer, *, emit_ctx):
    bsz = lat_p[0].shape[0]
    in_specs = []
    args = []
    t_tot = 0
    for (mqk, mv, mo, g) in (ctx_p, lat_p):
        t = mqk.shape[1]
        n = t // CHUNK
        t_tot += t
        in_specs += [
            pl.BlockSpec((None, t, ML_DIM), lambda b, h: (b, 0, h)),
            pl.BlockSpec((None, t, ML_DIM), lambda b, h: (b, 0, ML_HEADS + h)),
            pl.BlockSpec((None, t, ML_DIM), lambda b, h: (b, 0, h)),
            pl.BlockSpec((None, t, ML_DIM), lambda b, h: (b, 0, h)),
            pl.BlockSpec((None, n, GATE_ROWS, CHUNK), lambda b, h: (b, 0, h, 0)),
        ]
        args += [mqk, mqk, mv, mo, g]
    in_specs += [
        pl.BlockSpec((None, CONV_WIDTH, ML_DIM), lambda b, h: (layer, 0, h)),
        pl.BlockSpec((None, CONV_WIDTH, ML_DIM), lambda b, h: (layer, 0, ML_HEADS + h)),
        pl.BlockSpec((None, 1, ML_DIM), lambda b, h: (layer, 0, h)),
    ]
    args += [conv_w, conv_w, ml_gain]
    t_c, t_l = ctx_p[0].shape[1], lat_p[0].shape[1]
    out_l_shape = jax.ShapeDtypeStruct((bsz, t_l, ML_WIDTH), BF16)
    out_l_spec = pl.BlockSpec((None, t_l, ML_DIM), lambda b, h: (b, 0, h))
    if emit_ctx:
        out_shape = (jax.ShapeDtypeStruct((bsz, t_c, ML_WIDTH), BF16), out_l_shape)
        out_specs = (pl.BlockSpec((None, t_c, ML_DIM), lambda b, h: (b, 0, h)), out_l_spec)
    else:
        out_shape = out_l_shape
        out_specs = out_l_spec
    n_tot = t_tot // CHUNK
    res = pl.pallas_call(
        functools.partial(_mlstm_kernel, emit_ctx=emit_ctx),
        grid=(bsz, ML_HEADS),
        in_specs=in_specs,
        out_specs=out_specs,
        out_shape=out_shape,
        scratch_shapes=[
            pltpu.VMEM((t_tot, ML_DIM), BF16),
            pltpu.VMEM((2, n_tot, GATE_ROWS, CHUNK), F32),
            pltpu.VMEM((2, n_tot, CHUNK, 2 * ML_DIM), F32),
            pltpu.VMEM((2, n_tot, ML_DIM, 2 * ML_DIM), F32),
            pltpu.VMEM((2, n_tot, CHUNK, 128), F32),
            pltpu.VMEM((2, n_tot, CHUNK, 128), F32),
            pltpu.VMEM((2, n_tot, GATE_ROWS, CHUNK), F32),
            pltpu.VMEM((t_tot, ML_DIM), F32),
            pltpu.VMEM((ML_DIM, 2 * ML_DIM), F32),
            pltpu.VMEM((8, 128), F32),
        ],
        compiler_params=_cparams(2),
        name="mlstm_emit" if emit_ctx else "mlstm_last",
    )(*args)
    if emit_ctx:
        return res
    return None, res


def _outmlp_kernel(*refs, final, ffn_chunk):
    x_ref, att_ref, mem_ref, mod_ref, wo_ref, gain_ref, w1_ref, w2_ref = refs[:8]
    if final:
        gf_ref, o_ref = refs[8:]
    else:
        o_ref = refs[8]
    mod = mod_ref[...]
    y = _dot(att_ref[...], wo_ref[0:ATTN_WIDTH, :]) + _dot(mem_ref[...], wo_ref[ATTN_WIDTH:D_MODEL, :])
    x1 = x_ref[...] + mod[2:3] * y
    h = _rms_modulate(x1, gain_ref[...], mod[3:4], mod[4:5]).astype(BF16)
    def hidden(j):
        f = jnp.maximum(_dot(h, w1_ref[:, ffn_chunk * j:ffn_chunk * (j + 1)]), 0.0)
        return (f * f).astype(BF16)

    n_ffn = FFN_DIM // ffn_chunk
    acc = jnp.zeros(x1.shape, F32)
    f_next = hidden(0)
    for j in range(n_ffn):
        f_cur = f_next
        if j + 1 < n_ffn:
            f_next = hidden(j + 1)
        acc = acc + _dot(f_cur, w2_ref[ffn_chunk * j:ffn_chunk * (j + 1), :])
    x2 = x1 + mod[5:6] * acc
    if final:
        ms = jnp.mean(x2 * x2, axis=-1, keepdims=True)
        x2 = x2 * lax.rsqrt(ms + NORM_EPS) * gf_ref[...]
    o_ref[...] = x2


def _outmlp_call(x, att, mem, mod, mod_row, wo, gain, w1, w2, layer, gf, *, tm, ffn_chunk=1024):
    bsz, t, d = x.shape
    final = gf is not None
    if mod_row is None:
        mod_map = lambda b, i: (b, 0, 0)
    else:
        mod_map = lambda b, i: (mod_row, 0, 0)
    in_specs = [
        pl.BlockSpec((None, tm, d), lambda b, i: (b, i, 0)),
        pl.BlockSpec((None, tm, ATTN_WIDTH), lambda b, i: (b, i, 0)),
        pl.BlockSpec((None, tm, ML_WIDTH), lambda b, i: (b, i, 0)),
        pl.BlockSpec((None, N_MOD, d), mod_map),
        _const_spec((None, d, d), lambda b, i: (layer, 0, 0)),
        _const_spec((None, 1, d), lambda b, i: (layer, 0, 0)),
        _const_spec((None, d, FFN_DIM), lambda b, i: (layer, 0, 0)),
        _const_spec((None, FFN_DIM, d), lambda b, i: (layer, 0, 0)),
    ]
    args = [x, att, mem, mod, wo, gain, w1, w2]
    if final:
        in_specs.append(_const_spec((1, d), lambda b, i: (0, 0)))
        args.append(gf)
    return pl.pallas_call(
        functools.partial(_outmlp_kernel, final=final, ffn_chunk=ffn_chunk),
        grid=(bsz, t // tm),
        in_specs=in_specs,
        out_specs=pl.BlockSpec((None, tm, d), lambda b, i: (b, i, 0)),
        out_shape=jax.ShapeDtypeStruct((bsz, t, d), F32),
        compiler_params=_cparams(2),
        name="outproj_mlp_final" if final else "outproj_mlp",
    )(*args)


def _deinterleave(a, heads):
    lead = a.shape[:-1]
    return a.reshape(*lead, heads, HEAD_DIM // 2, 2).swapaxes(-1, -2).reshape(*lead, heads * HEAD_DIM)


def _relayout_projection(w_in):
    depth, d, _ = w_in.shape
    w = w_in.astype(BF16)
    a_q, a_k, a_v = 0, ATTN_WIDTH, ATTN_WIDTH + KV_WIDTH
    m_qk = ATTN_WIDTH + 2 * KV_WIDTH
    m_v = m_qk + 2 * ML_WIDTH
    m_o = m_v + ML_WIDTH
    gates = m_o + ML_WIDTH
    g = w[..., gates:gates + N_GATES].reshape(depth, d, 4, ML_HEADS).swapaxes(-1, -2)
    g = jnp.pad(g, ((0, 0), (0, 0), (0, 0), (0, GATE_ROWS - 4))).reshape(depth, d, ML_HEADS * GATE_ROWS)
    g = jnp.pad(g, ((0, 0), (0, 0), (0, 128 - ML_HEADS * GATE_ROWS)))
    return jnp.concatenate([
        w[..., m_qk:m_qk + 2 * ML_WIDTH],
        _deinterleave(w[..., a_q:a_q + ATTN_WIDTH], ATTN_HEADS),
        w[..., m_v:m_v + ML_WIDTH],
        w[..., m_o:m_o + ML_WIDTH],
        _deinterleave(w[..., a_k:a_k + KV_WIDTH], KV_HEADS),
        w[..., a_v:a_v + KV_WIDTH],
        g], axis=-1)


def _rope_tables(t):
    rows = t // GRID_W
    row_idx = jnp.repeat(jnp.arange(rows, dtype=F32), GRID_W)
    col_idx = jnp.tile(jnp.arange(GRID_W, dtype=F32), rows)
    inv_freq = jnp.power(ROPE_THETA, -jnp.arange(0, ROPE_AXIS_DIM, 2, dtype=F32) / ROPE_AXIS_DIM)
    ang = jnp.concatenate([row_idx[:, None] * inv_freq, col_idx[:, None] * inv_freq], axis=-1)
    cos, sin = jnp.cos(ang), jnp.sin(ang)
    cos128 = jnp.tile(cos, (1, 4))
    sin128 = jnp.tile(jnp.concatenate([-sin, sin], axis=-1), (1, 2))
    return cos128, sin128


def _pick_tile(t, pref):
    tm = min(pref, t)
    while t % tm:
        tm //= 2
    return tm


def kernel(x, c, ctx, c_ctx, w_ada, b_ada, norm_mix, norm_mlp, w_in, b_gates, conv_qk,
           q_norm, k_norm, mlstm_norm, w_out, w_mlp_in, w_mlp_out, norm_final):
    bsz, t, d = x.shape
    t_c = ctx.shape[1]
    depth = w_ada.shape[0]
    assert d == D_MODEL and t % CHUNK == 0 and t_c % CHUNK == 0 and t % GRID_W == 0

    rows = -(-(bsz + 1) // 8) * 8
    cvec = jnp.zeros((rows, d), F32).at[:bsz].set(c).at[bsz].set(c_ctx)
    mod = _ada_call(cvec, w_ada, b_ada).reshape(depth, rows, N_MOD, d)

    w_all = _relayout_projection(w_in)
    assert w_all.shape[-1] == W_COLS
    qg = jnp.tile(_deinterleave(q_norm, 1), (1, 2)).reshape(depth, 1, 128)
    kg = jnp.tile(_deinterleave(k_norm, 1), (1, 2)).reshape(depth, 1, 128)
    norm_mix = norm_mix.reshape(depth, 1, d)
    norm_mlp = norm_mlp.reshape(depth, 1, d)
    mlstm_norm = mlstm_norm.reshape(depth, 1, ML_WIDTH)
    bg = jnp.zeros((depth, ML_HEADS, GATE_ROWS), F32).at[:, :, 0:4].set(
        b_gates.reshape(depth, 4, ML_HEADS).transpose(0, 2, 1)).reshape(depth, ML_HEADS * GATE_ROWS, 1)
    wo = w_out.astype(BF16)
    w1 = w_mlp_in.astype(BF16)
    w2 = w_mlp_out.astype(BF16)
    cos128, sin128 = _rope_tables(t)
    gf = norm_final.reshape(1, d)

    tm_l = _pick_tile(t, 512)
    tm_c = _pick_tile(t_c, 512)
    tq_l = _pick_tile(t, 512)
    tq_c = _pick_tile(t_c, 256)

    for layer in range(depth):
        emit_ctx = layer < depth - 1
        mod_l = mod[layer]
        qt_l, k_l, vt_l, mqk_l, mv_l, mo_l, g_l = _inproj_call(
            x, mod_l, None, norm_mix, w_all, layer, qg, kg, bg, cos128, sin128, use_rope=True, tm=tm_l)
        qt_c, k_c, vt_c, mqk_c, mv_c, mo_c, g_c = _inproj_call(
            ctx, mod_l, bsz, norm_mix, w_all, layer, qg, kg, bg, cos128, sin128, use_rope=False, tm=tm_c)
        att_l = _attn_call(qt_l, [k_c, k_l], [vt_c, vt_l], tq=tq_l)
        mem_c, mem_l = _mlstm_call((mqk_c, mv_c, mo_c, g_c), (mqk_l, mv_l, mo_l, g_l),
                                   conv_qk, mlstm_norm, layer, emit_ctx=emit_ctx)
        x = _outmlp_call(x, att_l, mem_l, mod_l, None, wo, norm_mlp, w1, w2, layer,
                         None if emit_ctx else gf, tm=tm_l)
        if emit_ctx:
            att_c = _attn_call(qt_c, [k_c], [vt_c], tq=tq_c)
            ctx = _outmlp_call(ctx, att_c, mem_c, mod_l, bsz, wo, norm_mlp, w1, w2, layer, None, tm=tm_c)
    return x
```

```python
import functools

import jax
import jax.numpy as jnp
from jax import lax
from jax.experimental import pallas as pl
from jax.experimental.pallas import tpu as pltpu

D_MODEL = 1024
N_MOD = 6
GRID_W = 64
ATTN_WIDTH = 512
ATTN_HEADS = 8
HEAD_DIM = 64
KV_HEADS = 2
ATTN_GROUP = ATTN_HEADS // KV_HEADS
KV_WIDTH = KV_HEADS * HEAD_DIM
ROPE_THETA = 10000.0
ROPE_AXIS_DIM = HEAD_DIM // 2
ML_WIDTH = 512
ML_HEADS = 4
ML_DIM = 128
CHUNK = 128
CONV_WIDTH = 3
N_GATES = 4 * ML_HEADS
GATE_ROWS = 8
FFN_DIM = 4 * D_MODEL
NORM_EPS = 1e-6
LOG2_E = 1.4426950408889634

C_MQK = 0
C_Q = C_MQK + 2 * ML_WIDTH
C_MV = C_Q + ATTN_WIDTH
C_MO = C_MV + ML_WIDTH
C_K = C_MO + ML_WIDTH
C_V = C_K + KV_WIDTH
C_G = C_V + KV_WIDTH
W_COLS = C_G + 128

VMEM_LIMIT_BYTES = 56 * 1024 * 1024

BF16 = jnp.bfloat16
F32 = jnp.float32


def _cparams(n_grid):
    return pltpu.CompilerParams(
        dimension_semantics=("arbitrary",) * n_grid,
        vmem_limit_bytes=VMEM_LIMIT_BYTES)


def _const_spec(shape, index_map):
    return pl.BlockSpec(shape, index_map, pipeline_mode=pl.Buffered(1))


def _dot(a, b):
    return jnp.dot(a, b, preferred_element_type=F32)


def _sigmoid(x):
    return 1.0 / (1.0 + jnp.exp(-x))


def _log_sigmoid(x):
    return jnp.minimum(x, 0.0) - jnp.log(1.0 + jnp.exp(-jnp.abs(x)))


def _group_ones(n, group):
    r = lax.broadcasted_iota(jnp.int32, (n, n), 0) // group
    c = lax.broadcasted_iota(jnp.int32, (n, n), 1) // group
    return jnp.where(r == c, 1.0, 0.0).astype(BF16)


def _split_dot(a, ones_mat):
    hi = a.astype(BF16)
    lo = (a - hi.astype(F32)).astype(BF16)
    return _dot(hi, ones_mat) + _dot(lo, ones_mat)


def _ada_kernel(c_ref, w_ref, b_ref, o_ref):
    c = c_ref[...]
    sc = (c * _sigmoid(c)).astype(BF16)
    o_ref[0] = _dot(sc, w_ref[0].astype(BF16)) + b_ref[0]


def _ada_call(cvec, w_ada, b_ada):
    depth, d, n = w_ada.shape
    rows = cvec.shape[0]
    tn = n // 4
    return pl.pallas_call(
        _ada_kernel,
        grid=(depth, n // tn),
        in_specs=[
            pl.BlockSpec((rows, d), lambda l, j: (0, 0)),
            pl.BlockSpec((1, d, tn), lambda l, j: (l, 0, j)),
            pl.BlockSpec((1, 1, tn), lambda l, j: (l, 0, j)),
        ],
        out_specs=pl.BlockSpec((1, rows, tn), lambda l, j: (l, 0, j)),
        out_shape=jax.ShapeDtypeStruct((depth, rows, n), F32),
        compiler_params=_cparams(2),
        name="ada_mod",
    )(cvec, w_ada, b_ada.reshape(depth, 1, n))


def _rms_modulate(x, gain, shift, scale):
    ms = jnp.mean(x * x, axis=-1, keepdims=True)
    y = x * lax.rsqrt(ms + NORM_EPS) * gain
    return y * (1.0 + scale) + shift


def _swap32(x):
    lane = lax.broadcasted_iota(jnp.int32, x.shape, 1)
    up = pltpu.roll(x, 96, 1)
    down = pltpu.roll(x, 32, 1)
    return jnp.where((lane % HEAD_DIM) < ROPE_AXIS_DIM, up, down)


def _head_sumsq(t):
    return _dot((t * t).astype(BF16), _group_ones(t.shape[1], HEAD_DIM))


def _head_norm_rope(t, ss, gain, cos, sin, use_rope):
    tn = t * lax.rsqrt(ss * (1.0 / HEAD_DIM) + NORM_EPS) * gain
    if use_rope:
        tn = tn * cos + _swap32(tn) * sin
    return tn


def _inproj_kernel(x_ref, mod_ref, gain_ref, w_ref, qg_ref, kg_ref, bg_ref, cos_ref, sin_ref,
                   qt_out, k_out, vt_out, mqk_out, mv_out, mo_out, g_out, *, use_rope):
    x = x_ref[...]
    mod = mod_ref[...]
    h = _rms_modulate(x, gain_ref[...], mod[0:1], mod[1:2]).astype(BF16)
    tm = x.shape[0]
    cos = cos_ref[...] if use_rope else None
    sin = sin_ref[...] if use_rope else None

    q = _dot(h, w_ref[:, C_Q:C_Q + ATTN_WIDTH])
    kvg = _dot(h, w_ref[:, C_K:W_COLS])
    k, v, g = kvg[:, 0:KV_WIDTH], kvg[:, KV_WIDTH:2 * KV_WIDTH], kvg[:, 2 * KV_WIDTH:]
    mqk_out[...] = _dot(h, w_ref[:, C_MQK:C_MQK + 2 * ML_WIDTH])

    qscale = HEAD_DIM ** -0.5 * LOG2_E
    q_ss = [_head_sumsq(q[:, 256 * j:256 * (j + 1)]) for j in range(ATTN_WIDTH // 256)]
    k_ss = _head_sumsq(k)
    q_blocks = [_head_norm_rope(q[:, 128 * j:128 * (j + 1)], q_ss[j // 2][:, 128 * (j % 2):128 * (j % 2 + 1)],
                                qg_ref[...], cos, sin, use_rope)
                for j in range(ATTN_WIDTH // 128)]
    k_normed = _head_norm_rope(k, k_ss, kg_ref[...], cos, sin, use_rope)

    mv_out[...] = _dot(h, w_ref[:, C_MV:C_MV + ML_WIDTH]).astype(BF16)
    mo_out[...] = _sigmoid(_dot(h, w_ref[:, C_MO:C_MO + ML_WIDTH])).astype(BF16)
    vt = v.T
    pad_row = lax.broadcasted_iota(jnp.int32, (HEAD_DIM, tm), 0)
    pad = jnp.where(pad_row == 0, 1.0, 0.0)
    for kvh in range(KV_HEADS):
        vt_out[128 * kvh:128 * kvh + HEAD_DIM, :] = vt[HEAD_DIM * kvh:HEAD_DIM * (kvh + 1)].astype(BF16)
        vt_out[128 * kvh + HEAD_DIM:128 * (kvh + 1), :] = pad.astype(BF16)

    for j, blk in enumerate(q_blocks):
        qt_out[128 * j:128 * (j + 1), :] = (blk * qscale).T.astype(BF16)
    k_out[...] = k_normed.astype(BF16)

    gt = g.T[0:ML_HEADS * GATE_ROWS, :] + bg_ref[...]
    row = lax.broadcasted_iota(jnp.int32, gt.shape, 0) % GATE_ROWS
    gt = jnp.where((row == 1) | (row == 3), _log_sigmoid(gt), gt)
    for j in range(tm // CHUNK):
        g_out[j] = gt[:, CHUNK * j:CHUNK * (j + 1)]


def _inproj_call(x, mod, mod_row, gain, w_all, layer, qg, kg, bg, cos, sin, *, use_rope, tm):
    bsz, t, d = x.shape
    nt = t // tm
    nc = t // CHUNK
    cpt = tm // CHUNK
    if mod_row is None:
        mod_map = lambda b, i: (b, 0, 0)
    else:
        mod_map = lambda b, i: (mod_row, 0, 0)
    tok = lambda w: pl.BlockSpec((None, tm, w), lambda b, i: (b, i, 0))
    out_shapes = (
        jax.ShapeDtypeStruct((bsz, ATTN_WIDTH, t), BF16),
        jax.ShapeDtypeStruct((bsz, t, KV_WIDTH), BF16),
        jax.ShapeDtypeStruct((bsz, KV_HEADS * 128, t), BF16),
        jax.ShapeDtypeStruct((bsz, t, 2 * ML_WIDTH), F32),
        jax.ShapeDtypeStruct((bsz, t, ML_WIDTH), BF16),
        jax.ShapeDtypeStruct((bsz, t, ML_WIDTH), BF16),
        jax.ShapeDtypeStruct((bsz, nc, ML_HEADS * GATE_ROWS, CHUNK), F32),
    )
    out_specs = (
        pl.BlockSpec((None, ATTN_WIDTH, tm), lambda b, i: (b, 0, i)),
        tok(KV_WIDTH),
        pl.BlockSpec((None, KV_HEADS * 128, tm), lambda b, i: (b, 0, i)),
        tok(2 * ML_WIDTH),
        tok(ML_WIDTH),
        tok(ML_WIDTH),
        pl.BlockSpec((None, cpt, ML_HEADS * GATE_ROWS, CHUNK), lambda b, i: (b, i, 0, 0)),
    )
    in_specs = [
        tok(d),
        pl.BlockSpec((None, N_MOD, d), mod_map),
        _const_spec((None, 1, d), lambda b, i: (layer, 0, 0)),
        _const_spec((None, d, W_COLS), lambda b, i: (layer, 0, 0)),
        _const_spec((None, 1, 128), lambda b, i: (layer, 0, 0)),
        _const_spec((None, 1, 128), lambda b, i: (layer, 0, 0)),
        _const_spec((None, ML_HEADS * GATE_ROWS, 1), lambda b, i: (layer, 0, 0)),
        pl.BlockSpec((tm, 128), lambda b, i: (i, 0)),
        pl.BlockSpec((tm, 128), lambda b, i: (i, 0)),
    ]
    return pl.pallas_call(
        functools.partial(_inproj_kernel, use_rope=use_rope),
        grid=(bsz, nt),
        in_specs=in_specs,
        out_specs=out_specs,
        out_shape=out_shapes,
        compiler_params=_cparams(2),
        name="inproj_rope" if use_rope else "inproj_ctx",
    )(x, mod, gain, w_all, qg, kg, bg, cos, sin)


def _attn_kernel(*refs, n_seg, q_block):
    qt_ref = refs[0]
    k_refs = refs[1:1 + n_seg]
    vt_refs = refs[1 + n_seg:1 + 2 * n_seg]
    o_ref = refs[1 + 2 * n_seg]
    k_all, vt_all, st_ref = refs[2 + 2 * n_seg:]
    s_tot = k_all.shape[0]
    tq = qt_ref.shape[1]
    cols = ATTN_GROUP * q_block

    @pl.when(pl.program_id(1) == 0)
    def _():
        base = 0
        for seg in range(n_seg):
            s_len = k_refs[seg].shape[0]
            k_all[base:base + s_len, :] = k_refs[seg][...]
            vt_all[:, base:base + s_len] = vt_refs[seg][...]
            base += s_len

    groups = [(qb, kvh) for qb in range(tq // q_block) for kvh in range(KV_HEADS)]
    zeros = jnp.zeros((HEAD_DIM, q_block), BF16)

    def padded_queries(qb, kvh):
        pieces = []
        for g in range(ATTN_GROUP):
            row0 = HEAD_DIM * (ATTN_GROUP * kvh + g)
            qg = qt_ref[row0:row0 + HEAD_DIM, q_block * qb:q_block * (qb + 1)]
            pieces.append(jnp.concatenate([qg, zeros] if kvh == 0 else [zeros, qg], axis=0))
        return jnp.concatenate(pieces, axis=1)

    def finalize(qb, kvh, acc):
        ot = acc[0:HEAD_DIM] * (1.0 / acc[HEAD_DIM:HEAD_DIM + 1])
        for pair in range(ATTN_GROUP // 2):
            two = jnp.concatenate([ot[:, q_block * (2 * pair):q_block * (2 * pair + 1)],
                                   ot[:, q_block * (2 * pair + 1):q_block * (2 * pair + 2)]], axis=0)
            lane0 = HEAD_DIM * (ATTN_GROUP * kvh + 2 * pair)
            o_ref[q_block * qb:q_block * (qb + 1), lane0:lane0 + 2 * HEAD_DIM] = two.T.astype(BF16)

    m_prev = None
    for j in range(len(groups) + 1):
        qz = padded_queries(*groups[j]) if j < len(groups) else None
        m_next = None
        if qz is not None:
            st = _dot(k_all[...], qz)
            st_ref[j % 2] = st
            m_next = jnp.max(jnp.max(st.reshape(s_tot // 8, 8, cols), axis=0), axis=0, keepdims=True)
        if j >= 1:
            qb, kvh = groups[j - 1]
            p = jnp.exp2(st_ref[(j - 1) % 2] - m_prev).astype(BF16)
            finalize(qb, kvh, _dot(vt_all[128 * kvh:128 * (kvh + 1), :], p))
        m_prev = m_next


def _attn_call(qt, ks, vts, *, tq, q_block=128):
    bsz, _, t = qt.shape
    n_seg = len(ks)
    s_tot = sum(k.shape[1] for k in ks)
    q_block = min(q_block, tq)
    cols = ATTN_GROUP * q_block
    in_specs = [pl.BlockSpec((None, ATTN_WIDTH, tq), lambda b, i: (b, 0, i))]
    for k in ks:
        in_specs.append(pl.BlockSpec((None, k.shape[1], KV_WIDTH), lambda b, i: (b, 0, 0)))
    for vt in vts:
        in_specs.append(pl.BlockSpec((None, KV_HEADS * 128, vt.shape[2]), lambda b, i: (b, 0, 0)))
    return pl.pallas_call(
        functools.partial(_attn_kernel, n_seg=n_seg, q_block=q_block),
        grid=(bsz, t // tq),
        in_specs=in_specs,
        out_specs=pl.BlockSpec((None, tq, ATTN_WIDTH), lambda b, i: (b, i, 0)),
        out_shape=jax.ShapeDtypeStruct((bsz, t, ATTN_WIDTH), BF16),
        scratch_shapes=[
            pltpu.VMEM((s_tot, KV_WIDTH), BF16),
            pltpu.VMEM((KV_HEADS * 128, s_tot), BF16),
            pltpu.VMEM((2, s_tot, cols), F32),
        ],
        compiler_params=_cparams(2),
        name="attention_%dseg" % n_seg,
    )(qt, *ks, *vts)


def _mlstm_kernel(*refs, emit_ctx):
    (mq_c, mk_c, mv_c, mo_c, g_c, mq_l, mk_l, mv_l, mo_l, g_l, cw_q, cw_k, gain_ref) = refs[:13]
    if emit_ctx:
        out_c, out_l = refs[13:15]
        scratch = refs[15:]
    else:
        out_c = None
        out_l = refs[13]
        scratch = refs[14:]
    q_s, cum_s, x_s, u_s, pm_s, nb_s, sc_s, hf_s, ct_s, m_s = scratch

    L = CHUNK
    segs = []
    chunk_base = 0
    for (mq, mk, mv, mo, g, out, emit) in ((mq_c, mk_c, mv_c, mo_c, g_c, out_c, emit_ctx),
                                           (mq_l, mk_l, mv_l, mo_l, g_l, out_l, True)):
        n = mq.shape[0] // L
        segs.append(dict(mq=mq, mk=mk, mv=mv, mo=mo, g=g, out=out, emit=emit, n=n, cb=chunk_base))
        chunk_base += n

    ri = lax.broadcasted_iota(jnp.int32, (L, L), 0)
    ci = lax.broadcasted_iota(jnp.int32, (L, L), 1)
    ones_le = jnp.where(ri <= ci, 1.0, 0.0).astype(BF16)
    ones_ge = jnp.where(ri >= ci, 1.0, 0.0).astype(BF16)
    masks = (ci <= ri, ci >= ri)
    row_i = lax.broadcasted_iota(jnp.int32, (L, ML_DIM), 0)
    ones_blk = jnp.ones((L, 128), BF16)
    kscale = ML_DIM ** -0.5

    def unroll_of(n, want):
        while n % want:
            want //= 2
        return want

    def conv_silu(src, cw, c, n):
        t0 = pl.multiple_of(c * L, L)
        xc = src[pl.ds(t0, L), :]
        prev = src[pl.ds(jnp.maximum(t0 - 1, 0), 1), :] * jnp.where(c > 0, 1.0, 0.0)
        nxt = src[pl.ds(jnp.minimum(t0 + L, n * L - 1), 1), :] * jnp.where(c < n - 1, 1.0, 0.0)
        xm = jnp.where(row_i == 0, prev, pltpu.roll(xc, 1, 0))
        xp = jnp.where(row_i == L - 1, nxt, pltpu.roll(xc, L - 1, 0))
        y = xm * cw[0:1, :] + xc * cw[1:2, :] + xp * cw[2:3, :]
        return y * _sigmoid(y)

    for sg in segs:
        n, cb = sg["n"], sg["cb"]
        g2 = sg["g"][...].reshape(n * GATE_ROWS, L) * LOG2_E
        cum_s[0, cb:cb + n] = _split_dot(g2, ones_le).reshape(n, GATE_ROWS, L)
        cum_s[1, cb:cb + n] = _split_dot(g2, ones_ge).reshape(n, GATE_ROWS, L)

    def gate_rows(d, sg, c):
        gr = sg["g"][c] * LOG2_E
        cum = cum_s[d, sg["cb"] + c]
        if d == 0:
            return gr[0:1], gr[1:2], cum[1:2], cum[1:2, L - 1:L]
        return gr[2:3], gr[3:4], cum[3:4], cum[3:4, 0:1]

    def state_free_pass():
        for sg in segs:
            group = unroll_of(sg["n"], 4)

            def body(i, carry, sg=sg, group=group):
                pending = []
                for j in range(group):
                    c = i * group + j
                    cg = sg["cb"] + c
                    t0 = pl.multiple_of(c * L, L)
                    t0g = pl.multiple_of(cg * L, L)
                    qc = conv_silu(sg["mq"], cw_q, c, sg["n"]).astype(BF16)
                    ktc = (conv_silu(sg["mk"], cw_k, c, sg["n"]) * kscale).T.astype(BF16)
                    q_s[pl.ds(t0g, L), :] = qc
                    v_aug = jnp.concatenate([sg["mv"][pl.ds(t0, L), :], ones_blk], axis=1)
                    s0 = _dot(qc, ktc) if sg["emit"] else None
                    kws, es = [], []
                    for d in range(2):
                        ig, lf, brow, total = gate_rows(d, sg, c)
                        a = ig - brow
                        amax = jnp.max(a, axis=-1, keepdims=True)
                        kws.append((ktc.astype(F32) * jnp.exp2(a - amax)).astype(BF16))
                        sc_s[d, cg] = jnp.concatenate(
                            [jnp.broadcast_to(amax, (1, L)), jnp.broadcast_to(total, (1, L)),
                             jnp.zeros((GATE_ROWS - 2, L), F32)], axis=0)
                        if sg["emit"]:
                            a_vis = jnp.where(masks[d], a, -jnp.inf)
                            pm = jnp.broadcast_to(jnp.max(a_vis, axis=-1, keepdims=True), (L, L))
                            pm_s[d, cg] = pm
                            bcol = jnp.sum(jnp.where(masks[d], lf, 0.0), axis=-1, keepdims=True)
                            nb_s[d, cg] = -jnp.broadcast_to(bcol, (L, L)) - pm
                            es.append(jnp.exp2(a_vis - pm))
                    pending.append((cg, s0, kws, es, v_aug))
                for cg, s0, kws, es, v_aug in pending:
                    lhs = jnp.concatenate(kws + [(s0 * e).astype(BF16) for e in es], axis=0)
                    res = _dot(lhs, v_aug)
                    for d in range(2):
                        u_s[d, cg] = res[L * d:L * (d + 1)]
                        if es:
                            x_s[d, cg] = res[L * (2 + d):L * (3 + d)]
                return carry
            lax.fori_loop(0, sg["n"] // group, body, 0)

    def sequential_pass(d):
        ct_s[...] = jnp.zeros(ct_s.shape, F32)
        m_s[...] = jnp.zeros(m_s.shape, F32)
        for sg in segs:
            n = sg["n"]
            group = unroll_of(n, 8)

            def body(i, carry, sg=sg, n=n, group=group):
                m_prev = m_s[0:1, 0:1]
                ct = ct_s[...]
                pending = []
                for j in range(group):
                    c = i * group + j if d == 0 else n - 1 - (i * group + j)
                    cg = sg["cb"] + c
                    t0 = pl.multiple_of(c * L, L)
                    t0g = pl.multiple_of(cg * L, L)
                    sc = sc_s[d, cg]
                    amax, total = sc[0:1, 0:1], sc[1:2, 0:1]
                    m_last = jnp.maximum(m_prev, amax)
                    if sg["emit"]:
                        inter = _dot(q_s[pl.ds(t0g, L), :], ct.astype(BF16))
                        pending.append((cg, t0, t0g, inter, m_prev))
                    ct = jnp.exp2(m_prev - m_last) * ct + jnp.exp2(amax - m_last) * u_s[d, cg]
                    m_prev = total + m_last
                ct_s[...] = ct
                m_s[...] = jnp.broadcast_to(m_prev, m_s.shape)
                for cg, t0, t0g, inter, m_in in pending:
                    pm = pm_s[d, cg]
                    mx = jnp.maximum(pm, m_in)
                    dpm = pm - mx
                    r = jnp.exp2(dpm)
                    wi = jnp.exp2(m_in - mx)
                    x = x_s[d, cg]
                    num = r * x[:, 0:ML_DIM] + wi * inter[:, 0:ML_DIM]
                    den = r * x[:, ML_DIM:] + wi * inter[:, ML_DIM:]
                    h = num / jnp.maximum(jnp.abs(den), jnp.exp2(nb_s[d, cg] + dpm))
                    if d == 0:
                        hf_s[pl.ds(t0g, L), :] = h
                    else:
                        hsum = hf_s[pl.ds(t0g, L), :] + h
                        ms = jnp.mean(hsum * hsum, axis=-1, keepdims=True)
                        hn = hsum * lax.rsqrt(ms + NORM_EPS) * gain_ref[...]
                        sg["out"][pl.ds(t0, L), :] = (sg["mo"][pl.ds(t0, L), :].astype(F32) * hn).astype(BF16)
                return carry

            lax.fori_loop(0, n // group, body, 0)

    state_free_pass()
    for d in range(2):
        sequential_pass(d)


def _mlstm_call(ctx_p, lat_p, conv_w, ml_gain, layer, *, emit_ctx):
    bsz = lat_p[0].shape[0]
    in_specs = []
    args = []
    t_tot = 0
    for (mqk, mv, mo, g) in (ctx_p, lat_p):
        t = mqk.shape[1]
        n = t // CHUNK
        t_tot += t
        in_specs += [
            pl.BlockSpec((None, t, ML_DIM), lambda b, h: (b, 0, h)),
            pl.BlockSpec((None, t, ML_DIM), lambda b, h: (b, 0, ML_HEADS + h)),
            pl.BlockSpec((None, t, ML_DIM), lambda b, h: (b, 0, h)),
            pl.BlockSpec((None, t, ML_DIM), lambda b, h: (b, 0, h)),
            pl.BlockSpec((None, n, GATE_ROWS, CHUNK), lambda b, h: (b, 0, h, 0)),
        ]
        args += [mqk, mqk, mv, mo, g]
    in_specs += [
        pl.BlockSpec((None, CONV_WIDTH, ML_DIM), lambda b, h: (layer, 0, h)),
        pl.BlockSpec((None, CONV_WIDTH, ML_DIM), lambda b, h: (layer, 0, ML_HEADS + h)),
        pl.BlockSpec((None, 1, ML_DIM), lambda b, h: (layer, 0, h)),
    ]
    args += [conv_w, conv_w, ml_gain]
    t_c, t_l = ctx_p[0].shape[1], lat_p[0].shape[1]
    out_l_shape = jax.ShapeDtypeStruct((bsz, t_l, ML_WIDTH), BF16)
    out_l_spec = pl.BlockSpec((None, t_l, ML_DIM), lambda b, h: (b, 0, h))
    if emit_ctx:
        out_shape = (jax.ShapeDtypeStruct((bsz, t_c, ML_WIDTH), BF16), out_l_shape)
        out_specs = (pl.BlockSpec((None, t_c, ML_DIM), lambda b, h: (b, 0, h)), out_l_spec)
    else:
        out_shape = out_l_shape
        out_specs = out_l_spec
    n_tot = t_tot // CHUNK
    res = pl.pallas_call(
        functools.partial(_mlstm_kernel, emit_ctx=emit_ctx),
        grid=(bsz, ML_HEADS),
        in_specs=in_specs,
        out_specs=out_specs,
        out_shape=out_shape,
        scratch_shapes=[
            pltpu.VMEM((t_tot, ML_DIM), BF16),
            pltpu.VMEM((2, n_tot, GATE_ROWS, CHUNK), F32),
            pltpu.VMEM((2, n_tot, CHUNK, 2 * ML_DIM), F32),
            pltpu.VMEM((2, n_tot, ML_DIM, 2 * ML_DIM), F32),
            pltpu.VMEM((2, n_tot, CHUNK, 128), F32),
            pltpu.VMEM((2, n_tot, CHUNK, 128), F32),
            pltpu.VMEM((2, n_tot, GATE_ROWS, CHUNK), F32),
            pltpu.VMEM((t_tot, ML_DIM), F32),
            pltpu.VMEM((ML_DIM, 2 * ML_DIM), F32),
            pltpu.VMEM((8, 128), F32),
        ],
        compiler_params=_cparams(2),
        name="mlstm_emit" if emit_ctx else "mlstm_last",
    )(*args)
    if emit_ctx:
        return res
    return None, res


def _outmlp_kernel(*refs, final, ffn_chunk):
    x_ref, att_ref, mem_ref, mod_ref, wo_ref, gain_ref, w1_ref, w2_ref = refs[:8]
    if final:
        gf_ref, o_ref = refs[8:]
    else:
        o_ref = refs[8]
    mod = mod_ref[...]
    y = _dot(att_ref[...], wo_ref[0:ATTN_WIDTH, :]) + _dot(mem_ref[...], wo_ref[ATTN_WIDTH:D_MODEL, :])
    x1 = x_ref[...] + mod[2:3] * y
    h = _rms_modulate(x1, gain_ref[...], mod[3:4], mod[4:5]).astype(BF16)
    def hidden(j):
        f = jnp.maximum(_dot(h, w1_ref[:, ffn_chunk * j:ffn_chunk * (j + 1)]), 0.0)
        return (f * f).astype(BF16)

    n_ffn = FFN_DIM // ffn_chunk
    acc = jnp.zeros(x1.shape, F32)
    f_next = hidden(0)
    for j in range(n_ffn):
        f_cur = f_next
        if j + 1 < n_ffn:
            f_next = hidden(j + 1)
        acc = acc + _dot(f_cur, w2_ref[ffn_chunk * j:ffn_chunk * (j + 1), :])
    x2 = x1 + mod[5:6] * acc
    if final:
        ms = jnp.mean(x2 * x2, axis=-1, keepdims=True)
        x2 = x2 * lax.rsqrt(ms + NORM_EPS) * gf_ref[...]
    o_ref[...] = x2


def _outmlp_call(x, att, mem, mod, mod_row, wo, gain, w1, w2, layer, gf, *, tm, ffn_chunk=1024):
    bsz, t, d = x.shape
    final = gf is not None
    if mod_row is None:
        mod_map = lambda b, i: (b, 0, 0)
    else:
        mod_map = lambda b, i: (mod_row, 0, 0)
    in_specs = [
        pl.BlockSpec((None, tm, d), lambda b, i: (b, i, 0)),
        pl.BlockSpec((None, tm, ATTN_WIDTH), lambda b, i: (b, i, 0)),
        pl.BlockSpec((None, tm, ML_WIDTH), lambda b, i: (b, i, 0)),
        pl.BlockSpec((None, N_MOD, d), mod_map),
        _const_spec((None, d, d), lambda b, i: (layer, 0, 0)),
        _const_spec((None, 1, d), lambda b, i: (layer, 0, 0)),
        _const_spec((None, d, FFN_DIM), lambda b, i: (layer, 0, 0)),
        _const_spec((None, FFN_DIM, d), lambda b, i: (layer, 0, 0)),
    ]
    args = [x, att, mem, mod, wo, gain, w1, w2]
    if final:
        in_specs.append(_const_spec((1, d), lambda b, i: (0, 0)))
        args.append(gf)
    return pl.pallas_call(
        functools.partial(_outmlp_kernel, final=final, ffn_chunk=ffn_chunk),
        grid=(bsz, t // tm),
        in_specs=in_specs,
        out_specs=pl.BlockSpec((None, tm, d), lambda b, i: (b, i, 0)),
        out_shape=jax.ShapeDtypeStruct((bsz, t, d), F32),
        compiler_params=_cparams(2),
        name="outproj_mlp_final" if final else "outproj_mlp",
    )(*args)


def _deinterleave(a, heads):
    lead = a.shape[:-1]
    return a.reshape(*lead, heads, HEAD_DIM // 2, 2).swapaxes(-1, -2).reshape(*lead, heads * HEAD_DIM)


def _relayout_projection(w_in):
    depth, d, _ = w_in.shape
    w = w_in.astype(BF16)
    a_q, a_k, a_v = 0, ATTN_WIDTH, ATTN_WIDTH + KV_WIDTH
    m_qk = ATTN_WIDTH + 2 * KV_WIDTH
    m_v = m_qk + 2 * ML_WIDTH
    m_o = m_v + ML_WIDTH
    gates = m_o + ML_WIDTH
    g = w[..., gates:gates + N_GATES].reshape(depth, d, 4, ML_HEADS).swapaxes(-1, -2)
    g = jnp.pad(g, ((0, 0), (0, 0), (0, 0), (0, GATE_ROWS - 4))).reshape(depth, d, ML_HEADS * GATE_ROWS)
    g = jnp.pad(g, ((0, 0), (0, 0), (0, 128 - ML_HEADS * GATE_ROWS)))
    return jnp.concatenate([
        w[..., m_qk:m_qk + 2 * ML_WIDTH],
        _deinterleave(w[..., a_q:a_q + ATTN_WIDTH], ATTN_HEADS),
        w[..., m_v:m_v + ML_WIDTH],
        w[..., m_o:m_o + ML_WIDTH],
        _deinterleave(w[..., a_k:a_k + KV_WIDTH], KV_HEADS),
        w[..., a_v:a_v + KV_WIDTH],
        g], axis=-1)


def _rope_tables(t):
    rows = t // GRID_W
    row_idx = jnp.repeat(jnp.arange(rows, dtype=F32), GRID_W)
    col_idx = jnp.tile(jnp.arange(GRID_W, dtype=F32), rows)
    inv_freq = jnp.power(ROPE_THETA, -jnp.arange(0, ROPE_AXIS_DIM, 2, dtype=F32) / ROPE_AXIS_DIM)
    ang = jnp.concatenate([row_idx[:, None] * inv_freq, col_idx[:, None] * inv_freq], axis=-1)
    cos, sin = jnp.cos(ang), jnp.sin(ang)
    cos128 = jnp.tile(cos, (1, 4))
    sin128 = jnp.tile(jnp.concatenate([-sin, sin], axis=-1), (1, 2))
    return cos128, sin128


def _pick_tile(t, pref):
    tm = min(pref, t)
    while t % tm:
        tm //= 2
    return tm


def kernel(x, c, ctx, c_ctx, w_ada, b_ada, norm_mix, norm_mlp, w_in, b_gates, conv_qk,
           q_norm, k_norm, mlstm_norm, w_out, w_mlp_in, w_mlp_out, norm_final):
    bsz, t, d = x.shape
    t_c = ctx.shape[1]
    depth = w_ada.shape[0]
    assert d == D_MODEL and t % CHUNK == 0 and t_c % CHUNK == 0 and t % GRID_W == 0

    rows = -(-(bsz + 1) // 8) * 8
    cvec = jnp.zeros((rows, d), F32).at[:bsz].set(c).at[bsz].set(c_ctx)
    mod = _ada_call(cvec, w_ada, b_ada).reshape(depth, rows, N_MOD, d)

    w_all = _relayout_projection(w_in)
    assert w_all.shape[-1] == W_COLS
    qg = jnp.tile(_deinterleave(q_norm, 1), (1, 2)).reshape(depth, 1, 128)
    kg = jnp.tile(_deinterleave(k_norm, 1), (1, 2)).reshape(depth, 1, 128)
    norm_mix = norm_mix.reshape(depth, 1, d)
    norm_mlp = norm_mlp.reshape(depth, 1, d)
    mlstm_norm = mlstm_norm.reshape(depth, 1, ML_WIDTH)
    bg = jnp.zeros((depth, ML_HEADS, GATE_ROWS), F32).at[:, :, 0:4].set(
        b_gates.reshape(depth, 4, ML_HEADS).transpose(0, 2, 1)).reshape(depth, ML_HEADS * GATE_ROWS, 1)
    wo = w_out.astype(BF16)
    w1 = w_mlp_in.astype(BF16)
    w2 = w_mlp_out.astype(BF16)
    cos128, sin128 = _rope_tables(t)
    gf = norm_final.reshape(1, d)

    tm_l = _pick_tile(t, 512)
    tm_c = _pick_tile(t_c, 512)
    tq_l = _pick_tile(t, 1024)
    tq_c = _pick_tile(t_c, 256)

    for layer in range(depth):
        emit_ctx = layer < depth - 1
        mod_l = mod[layer]
        qt_l, k_l, vt_l, mqk_l, mv_l, mo_l, g_l = _inproj_call(
            x, mod_l, None, norm_mix, w_all, layer, qg, kg, bg, cos128, sin128, use_rope=True, tm=tm_l)
        qt_c, k_c, vt_c, mqk_c, mv_c, mo_c, g_c = _inproj_call(
            ctx, mod_l, bsz, norm_mix, w_all, layer, qg, kg, bg, cos128, sin128, use_rope=False, tm=tm_c)
        att_l = _attn_call(qt_l, [k_c, k_l], [vt_c, vt_l], tq=tq_l)
        mem_c, mem_l = _mlstm_call((mqk_c, mv_c, mo_c, g_c), (mqk_l, mv_l, mo_l, g_l),
                                   conv_qk, mlstm_norm, layer, emit_ctx=emit_ctx)
        x = _outmlp_call(x, att_l, mem_l, mod_l, None, wo, norm_mlp, w1, w2, layer,
                         None if emit_ctx else gf, tm=tm_l)
        if emit_ctx:
            att_c = _attn_call(qt_c, [k_c], [vt_c], tq=tq_c)
            ctx = _outmlp_call(ctx, att_c, mem_c, mod_l, bsz, wo, norm_mlp, w1, w2, layer, None, tm=tm_c)
    return x
```

```python
import functools

import jax
import jax.numpy as jnp
from jax import lax
from jax.experimental import pallas as pl
from jax.experimental.pallas import tpu as pltpu

D_MODEL = 1024
N_MOD = 6
GRID_W = 64
ATTN_WIDTH = 512
ATTN_HEADS = 8
HEAD_DIM = 64
KV_HEADS = 2
ATTN_GROUP = ATTN_HEADS // KV_HEADS
KV_WIDTH = KV_HEADS * HEAD_DIM
ROPE_THETA = 10000.0
ROPE_AXIS_DIM = HEAD_DIM // 2
ML_WIDTH = 512
ML_HEADS = 4
ML_DIM = 128
CHUNK = 128
CONV_WIDTH = 3
N_GATES = 4 * ML_HEADS
GATE_ROWS = 8
FFN_DIM = 4 * D_MODEL
NORM_EPS = 1e-6
LOG2_E = 1.4426950408889634

C_MQK = 0
C_Q = C_MQK + 2 * ML_WIDTH
C_MV = C_Q + ATTN_WIDTH
C_MO = C_MV + ML_WIDTH
C_K = C_MO + ML_WIDTH
C_V = C_K + KV_WIDTH
C_G = C_V + KV_WIDTH
W_COLS = C_G + 128

VMEM_LIMIT_BYTES = 56 * 1024 * 1024

BF16 = jnp.bfloat16
F32 = jnp.float32


def _cparams(n_grid):
    return pltpu.CompilerParams(
        dimension_semantics=("arbitrary",) * n_grid,
        vmem_limit_bytes=VMEM_LIMIT_BYTES)


def _const_spec(shape, index_map):
    return pl.BlockSpec(shape, index_map, pipeline_mode=pl.Buffered(1))


def _dot(a, b):
    return jnp.dot(a, b, preferred_element_type=F32)


def _sigmoid(x):
    return 1.0 / (1.0 + jnp.exp(-x))


def _log_sigmoid(x):
    return jnp.minimum(x, 0.0) - jnp.log(1.0 + jnp.exp(-jnp.abs(x)))


def _group_ones(n, group):
    r = lax.broadcasted_iota(jnp.int32, (n, n), 0) // group
    c = lax.broadcasted_iota(jnp.int32, (n, n), 1) // group
    return jnp.where(r == c, 1.0, 0.0).astype(BF16)


def _split_dot(a, ones_mat):
    hi = a.astype(BF16)
    lo = (a - hi.astype(F32)).astype(BF16)
    return _dot(hi, ones_mat) + _dot(lo, ones_mat)


def _ada_kernel(c_ref, w_ref, b_ref, o_ref):
    c = c_ref[...]
    sc = (c * _sigmoid(c)).astype(BF16)
    o_ref[0] = _dot(sc, w_ref[0].astype(BF16)) + b_ref[0]


def _ada_call(cvec, w_ada, b_ada):
    depth, d, n = w_ada.shape
    rows = cvec.shape[0]
    tn = n // 4
    return pl.pallas_call(
        _ada_kernel,
        grid=(depth, n // tn),
        in_specs=[
            pl.BlockSpec((rows, d), lambda l, j: (0, 0)),
            pl.BlockSpec((1, d, tn), lambda l, j: (l, 0, j)),
            pl.BlockSpec((1, 1, tn), lambda l, j: (l, 0, j)),
        ],
        out_specs=pl.BlockSpec((1, rows, tn), lambda l, j: (l, 0, j)),
        out_shape=jax.ShapeDtypeStruct((depth, rows, n), F32),
        compiler_params=_cparams(2),
        name="ada_mod",
    )(cvec, w_ada, b_ada.reshape(depth, 1, n))


def _rms_modulate(x, gain, shift, scale):
    ms = jnp.mean(x * x, axis=-1, keepdims=True)
    y = x * lax.rsqrt(ms + NORM_EPS) * gain
    return y * (1.0 + scale) + shift


def _swap32(x):
    lane = lax.broadcasted_iota(jnp.int32, x.shape, 1)
    up = pltpu.roll(x, 96, 1)
    down = pltpu.roll(x, 32, 1)
    return jnp.where((lane % HEAD_DIM) < ROPE_AXIS_DIM, up, down)


def _head_sumsq(t):
    return _dot((t * t).astype(BF16), _group_ones(t.shape[1], HEAD_DIM))


def _head_norm_rope(t, ss, gain, cos, sin, use_rope):
    tn = t * lax.rsqrt(ss * (1.0 / HEAD_DIM) + NORM_EPS) * gain
    if use_rope:
        tn = tn * cos + _swap32(tn) * sin
    return tn


def _inproj_kernel(x_ref, mod_ref, gain_ref, w_ref, qg_ref, kg_ref, bg_ref, cos_ref, sin_ref,
                   qt_out, k_out, vt_out, mqk_out, mv_out, mo_out, g_out, *, use_rope):
    x = x_ref[...]
    mod = mod_ref[...]
    h = _rms_modulate(x, gain_ref[...], mod[0:1], mod[1:2]).astype(BF16)
    tm = x.shape[0]
    cos = cos_ref[...] if use_rope else None
    sin = sin_ref[...] if use_rope else None

    q = _dot(h, w_ref[:, C_Q:C_Q + ATTN_WIDTH])
    kvg = _dot(h, w_ref[:, C_K:W_COLS])
    k, v, g = kvg[:, 0:KV_WIDTH], kvg[:, KV_WIDTH:2 * KV_WIDTH], kvg[:, 2 * KV_WIDTH:]
    mqk_out[...] = _dot(h, w_ref[:, C_MQK:C_MQK + 2 * ML_WIDTH])

    qscale = HEAD_DIM ** -0.5 * LOG2_E
    q_ss = [_head_sumsq(q[:, 256 * j:256 * (j + 1)]) for j in range(ATTN_WIDTH // 256)]
    k_ss = _head_sumsq(k)
    q_blocks = [_head_norm_rope(q[:, 128 * j:128 * (j + 1)], q_ss[j // 2][:, 128 * (j % 2):128 * (j % 2 + 1)],
                                qg_ref[...], cos, sin, use_rope)
                for j in range(ATTN_WIDTH // 128)]
    k_normed = _head_norm_rope(k, k_ss, kg_ref[...], cos, sin, use_rope)

    mv_out[...] = _dot(h, w_ref[:, C_MV:C_MV + ML_WIDTH]).astype(BF16)
    mo_out[...] = _sigmoid(_dot(h, w_ref[:, C_MO:C_MO + ML_WIDTH])).astype(BF16)
    vt = v.T
    pad_row = lax.broadcasted_iota(jnp.int32, (HEAD_DIM, tm), 0)
    pad = jnp.where(pad_row == 0, 1.0, 0.0)
    for kvh in range(KV_HEADS):
        vt_out[128 * kvh:128 * kvh + HEAD_DIM, :] = vt[HEAD_DIM * kvh:HEAD_DIM * (kvh + 1)].astype(BF16)
        vt_out[128 * kvh + HEAD_DIM:128 * (kvh + 1), :] = pad.astype(BF16)

    for j, blk in enumerate(q_blocks):
        qt_out[128 * j:128 * (j + 1), :] = (blk * qscale).T.astype(BF16)
    k_out[...] = k_normed.astype(BF16)

    gt = g.T[0:ML_HEADS * GATE_ROWS, :] + bg_ref[...]
    row = lax.broadcasted_iota(jnp.int32, gt.shape, 0) % GATE_ROWS
    gt = jnp.where((row == 1) | (row == 3), _log_sigmoid(gt), gt)
    for j in range(tm // CHUNK):
        g_out[j] = gt[:, CHUNK * j:CHUNK * (j + 1)]


def _inproj_call(x, mod, mod_row, gain, w_all, layer, qg, kg, bg, cos, sin, *, use_rope, tm):
    bsz, t, d = x.shape
    nt = t // tm
    nc = t // CHUNK
    cpt = tm // CHUNK
    if mod_row is None:
        mod_map = lambda b, i: (b, 0, 0)
    else:
        mod_map = lambda b, i: (mod_row, 0, 0)
    tok = lambda w: pl.BlockSpec((None, tm, w), lambda b, i: (b, i, 0))
    out_shapes = (
        jax.ShapeDtypeStruct((bsz, ATTN_WIDTH, t), BF16),
        jax.ShapeDtypeStruct((bsz, t, KV_WIDTH), BF16),
        jax.ShapeDtypeStruct((bsz, KV_HEADS * 128, t), BF16),
        jax.ShapeDtypeStruct((bsz, t, 2 * ML_WIDTH), F32),
        jax.ShapeDtypeStruct((bsz, t, ML_WIDTH), BF16),
        jax.ShapeDtypeStruct((bsz, t, ML_WIDTH), BF16),
        jax.ShapeDtypeStruct((bsz, nc, ML_HEADS * GATE_ROWS, CHUNK), F32),
    )
    out_specs = (
        pl.BlockSpec((None, ATTN_WIDTH, tm), lambda b, i: (b, 0, i)),
        tok(KV_WIDTH),
        pl.BlockSpec((None, KV_HEADS * 128, tm), lambda b, i: (b, 0, i)),
        tok(2 * ML_WIDTH),
        tok(ML_WIDTH),
        tok(ML_WIDTH),
        pl.BlockSpec((None, cpt, ML_HEADS * GATE_ROWS, CHUNK), lambda b, i: (b, i, 0, 0)),
    )
    in_specs = [
        tok(d),
        pl.BlockSpec((None, N_MOD, d), mod_map),
        _const_spec((None, 1, d), lambda b, i: (layer, 0, 0)),
        _const_spec((None, d, W_COLS), lambda b, i: (layer, 0, 0)),
        _const_spec((None, 1, 128), lambda b, i: (layer, 0, 0)),
        _const_spec((None, 1, 128), lambda b, i: (layer, 0, 0)),
        _const_spec((None, ML_HEADS * GATE_ROWS, 1), lambda b, i: (layer, 0, 0)),
        pl.BlockSpec((tm, 128), lambda b, i: (i, 0)),
        pl.BlockSpec((tm, 128), lambda b, i: (i, 0)),
    ]
    return pl.pallas_call(
        functools.partial(_inproj_kernel, use_rope=use_rope),
        grid=(bsz, nt),
        in_specs=in_specs,
        out_specs=out_specs,
        out_shape=out_shapes,
        compiler_params=_cparams(2),
        name="inproj_rope" if use_rope else "inproj_ctx",
    )(x, mod, gain, w_all, qg, kg, bg, cos, sin)


def _attn_kernel(*refs, n_seg, q_block):
    qt_ref = refs[0]
    k_refs = refs[1:1 + n_seg]
    vt_refs = refs[1 + n_seg:1 + 2 * n_seg]
    o_ref = refs[1 + 2 * n_seg]
    k_all, vt_all, st_ref = refs[2 + 2 * n_seg:]
    s_tot = k_all.shape[0]
    tq = qt_ref.shape[1]
    cols = ATTN_GROUP * q_block

    @pl.when(pl.program_id(1) == 0)
    def _():
        base = 0
        for seg in range(n_seg):
            s_len = k_refs[seg].shape[0]
            k_all[base:base + s_len, :] = k_refs[seg][...]
            vt_all[:, base:base + s_len] = vt_refs[seg][...]
            base += s_len

    groups = [(qb, kvh) for qb in range(tq // q_block) for kvh in range(KV_HEADS)]
    zeros = jnp.zeros((HEAD_DIM, q_block), BF16)

    def padded_queries(qb, kvh):
        pieces = []
        for g in range(ATTN_GROUP):
            row0 = HEAD_DIM * (ATTN_GROUP * kvh + g)
            qg = qt_ref[row0:row0 + HEAD_DIM, q_block * qb:q_block * (qb + 1)]
            pieces.append(jnp.concatenate([qg, zeros] if kvh == 0 else [zeros, qg], axis=0))
        return jnp.concatenate(pieces, axis=1)

    def finalize(qb, kvh, acc):
        ot = acc[0:HEAD_DIM] * (1.0 / acc[HEAD_DIM:HEAD_DIM + 1])
        for pair in range(ATTN_GROUP // 2):
            two = jnp.concatenate([ot[:, q_block * (2 * pair):q_block * (2 * pair + 1)],
                                   ot[:, q_block * (2 * pair + 1):q_block * (2 * pair + 2)]], axis=0)
            lane0 = HEAD_DIM * (ATTN_GROUP * kvh + 2 * pair)
            o_ref[q_block * qb:q_block * (qb + 1), lane0:lane0 + 2 * HEAD_DIM] = two.T.astype(BF16)

    m_prev = None
    for j in range(len(groups) + 1):
        qz = padded_queries(*groups[j]) if j < len(groups) else None
        m_next = None
        if qz is not None:
            st = _dot(k_all[...], qz)
            st_ref[j % 2] = st
            m_next = jnp.max(jnp.max(st.reshape(s_tot // 8, 8, cols), axis=0), axis=0, keepdims=True)
        if j >= 1:
            qb, kvh = groups[j - 1]
            p = jnp.exp2(st_ref[(j - 1) % 2] - m_prev).astype(BF16)
            finalize(qb, kvh, _dot(vt_all[128 * kvh:128 * (kvh + 1), :], p))
        m_prev = m_next


def _attn_call(qt, ks, vts, *, tq, q_block=128):
    bsz, _, t = qt.shape
    n_seg = len(ks)
    s_tot = sum(k.shape[1] for k in ks)
    q_block = min(q_block, tq)
    cols = ATTN_GROUP * q_block
    in_specs = [pl.BlockSpec((None, ATTN_WIDTH, tq), lambda b, i: (b, 0, i))]
    for k in ks:
        in_specs.append(pl.BlockSpec((None, k.shape[1], KV_WIDTH), lambda b, i: (b, 0, 0)))
    for vt in vts:
        in_specs.append(pl.BlockSpec((None, KV_HEADS * 128, vt.shape[2]), lambda b, i: (b, 0, 0)))
    return pl.pallas_call(
        functools.partial(_attn_kernel, n_seg=n_seg, q_block=q_block),
        grid=(bsz, t // tq),
        in_specs=in_specs,
        out_specs=pl.BlockSpec((None, tq, ATTN_WIDTH), lambda b, i: (b, i, 0)),
        out_shape=jax.ShapeDtypeStruct((bsz, t, ATTN_WIDTH), BF16),
        scratch_shapes=[
            pltpu.VMEM((s_tot, KV_WIDTH), BF16),
            pltpu.VMEM((KV_HEADS * 128, s_tot), BF16),
            pltpu.VMEM((2, s_tot, cols), F32),
        ],
        compiler_params=_cparams(2),
        name="attention_%dseg" % n_seg,
    )(qt, *ks, *vts)


def _mlstm_kernel(*refs, emit_ctx):
    (mq_c, mk_c, mv_c, mo_c, g_c, mq_l, mk_l, mv_l, mo_l, g_l, cw_q, cw_k, gain_ref) = refs[:13]
    if emit_ctx:
        out_c, out_l = refs[13:15]
        scratch = refs[15:]
    else:
        out_c = None
        out_l = refs[13]
        scratch = refs[14:]
    q_s, cum_s, x_s, u_s, pm_s, nb_s, sc_s, hf_s, ct_s, m_s = scratch

    L = CHUNK
    segs = []
    chunk_base = 0
    for (mq, mk, mv, mo, g, out, emit) in ((mq_c, mk_c, mv_c, mo_c, g_c, out_c, emit_ctx),
                                           (mq_l, mk_l, mv_l, mo_l, g_l, out_l, True)):
        n = mq.shape[0] // L
        segs.append(dict(mq=mq, mk=mk, mv=mv, mo=mo, g=g, out=out, emit=emit, n=n, cb=chunk_base))
        chunk_base += n

    ri = lax.broadcasted_iota(jnp.int32, (L, L), 0)
    ci = lax.broadcasted_iota(jnp.int32, (L, L), 1)
    ones_le = jnp.where(ri <= ci, 1.0, 0.0).astype(BF16)
    ones_ge = jnp.where(ri >= ci, 1.0, 0.0).astype(BF16)
    masks = (ci <= ri, ci >= ri)
    row_i = lax.broadcasted_iota(jnp.int32, (L, ML_DIM), 0)
    ones_blk = jnp.ones((L, 128), BF16)
    kscale = ML_DIM ** -0.5

    def unroll_of(n, want):
        while n % want:
            want //= 2
        return want

    def conv_silu(src, cw, c, n):
        t0 = pl.multiple_of(c * L, L)
        xc = src[pl.ds(t0, L), :]
        prev = src[pl.ds(jnp.maximum(t0 - 1, 0), 1), :] * jnp.where(c > 0, 1.0, 0.0)
        nxt = src[pl.ds(jnp.minimum(t0 + L, n * L - 1), 1), :] * jnp.where(c < n - 1, 1.0, 0.0)
        xm = jnp.where(row_i == 0, prev, pltpu.roll(xc, 1, 0))
        xp = jnp.where(row_i == L - 1, nxt, pltpu.roll(xc, L - 1, 0))
        y = xm * cw[0:1, :] + xc * cw[1:2, :] + xp * cw[2:3, :]
        return y * _sigmoid(y)

    for sg in segs:
        n, cb = sg["n"], sg["cb"]
        g2 = sg["g"][...].reshape(n * GATE_ROWS, L) * LOG2_E
        cum_s[0, cb:cb + n] = _split_dot(g2, ones_le).reshape(n, GATE_ROWS, L)
        cum_s[1, cb:cb + n] = _split_dot(g2, ones_ge).reshape(n, GATE_ROWS, L)

    def gate_rows(d, sg, c):
        gr = sg["g"][c] * LOG2_E
        cum = cum_s[d, sg["cb"] + c]
        if d == 0:
            return gr[0:1], gr[1:2], cum[1:2], cum[1:2, L - 1:L]
        return gr[2:3], gr[3:4], cum[3:4], cum[3:4, 0:1]

    def state_free_pass():
        for sg in segs:
            group = unroll_of(sg["n"], 8)

            def body(i, carry, sg=sg, group=group):
                pending = []
                for j in range(group):
                    c = i * group + j
                    cg = sg["cb"] + c
                    t0 = pl.multiple_of(c * L, L)
                    t0g = pl.multiple_of(cg * L, L)
                    qc = conv_silu(sg["mq"], cw_q, c, sg["n"]).astype(BF16)
                    ktc = (conv_silu(sg["mk"], cw_k, c, sg["n"]) * kscale).T.astype(BF16)
                    q_s[pl.ds(t0g, L), :] = qc
                    v_aug = jnp.concatenate([sg["mv"][pl.ds(t0, L), :], ones_blk], axis=1)
                    s0 = _dot(qc, ktc) if sg["emit"] else None
                    kws, es = [], []
                    for d in range(2):
                        ig, lf, brow, total = gate_rows(d, sg, c)
                        a = ig - brow
                        amax = jnp.max(a, axis=-1, keepdims=True)
                        kws.append((ktc.astype(F32) * jnp.exp2(a - amax)).astype(BF16))
                        sc_s[d, cg] = jnp.concatenate(
                            [jnp.broadcast_to(amax, (1, L)), jnp.broadcast_to(total, (1, L)),
                             jnp.zeros((GATE_ROWS - 2, L), F32)], axis=0)
                        if sg["emit"]:
                            a_vis = jnp.where(masks[d], a, -jnp.inf)
                            pm = jnp.broadcast_to(jnp.max(a_vis, axis=-1, keepdims=True), (L, L))
                            pm_s[d, cg] = pm
                            bcol = jnp.sum(jnp.where(masks[d], lf, 0.0), axis=-1, keepdims=True)
                            nb_s[d, cg] = -jnp.broadcast_to(bcol, (L, L)) - pm
                            es.append(jnp.exp2(a_vis - pm))
                    pending.append((cg, s0, kws, es, v_aug))
                for cg, s0, kws, es, v_aug in pending:
                    lhs = jnp.concatenate(kws + [(s0 * e).astype(BF16) for e in es], axis=0)
                    res = _dot(lhs, v_aug)
                    for d in range(2):
                        u_s[d, cg] = res[L * d:L * (d + 1)]
                        if es:
                            x_s[d, cg] = res[L * (2 + d):L * (3 + d)]
                return carry
            lax.fori_loop(0, sg["n"] // group, body, 0)

    def sequential_pass(d):
        ct_s[...] = jnp.zeros(ct_s.shape, F32)
        m_s[...] = jnp.zeros(m_s.shape, F32)
        for sg in segs:
            n = sg["n"]
            group = unroll_of(n, 8)

            def body(i, carry, sg=sg, n=n, group=group):
                m_prev = m_s[0:1, 0:1]
                ct = ct_s[...]
                pending = []
                for j in range(group):
                    c = i * group + j if d == 0 else n - 1 - (i * group + j)
                    cg = sg["cb"] + c
                    t0 = pl.multiple_of(c * L, L)
                    t0g = pl.multiple_of(cg * L, L)
                    sc = sc_s[d, cg]
                    amax, total = sc[0:1, 0:1], sc[1:2, 0:1]
                    m_last = jnp.maximum(m_prev, amax)
                    if sg["emit"]:
                        inter = _dot(q_s[pl.ds(t0g, L), :], ct.astype(BF16))
                        pending.append((cg, t0, t0g, inter, m_prev))
                    ct = jnp.exp2(m_prev - m_last) * ct + jnp.exp2(amax - m_last) * u_s[d, cg]
                    m_prev = total + m_last
                ct_s[...] = ct
                m_s[...] = jnp.broadcast_to(m_prev, m_s.shape)
                for cg, t0, t0g, inter, m_in in pending:
                    pm = pm_s[d, cg]
                    mx = jnp.maximum(pm, m_in)
                    dpm = pm - mx
                    r = jnp.exp2(dpm)
                    wi = jnp.exp2(m_in - mx)
                    x = x_s[d, cg]
                    num = r * x[:, 0:ML_DIM] + wi * inter[:, 0:ML_DIM]
                    den = r * x[:, ML_DIM:] + wi * inter[:, ML_DIM:]
                    h = num / jnp.maximum(jnp.abs(den), jnp.exp2(nb_s[d, cg] + dpm))
                    if d == 0:
                        hf_s[pl.ds(t0g, L), :] = h
                    else:
                        hsum = hf_s[pl.ds(t0g, L), :] + h
                        ms = jnp.mean(hsum * hsum, axis=-1, keepdims=True)
                        hn = hsum * lax.rsqrt(ms + NORM_EPS) * gain_ref[...]
                        sg["out"][pl.ds(t0, L), :] = (sg["mo"][pl.ds(t0, L), :].astype(F32) * hn).astype(BF16)
                return carry

            lax.fori_loop(0, n // group, body, 0)

    state_free_pass()
    for d in range(2):
        sequential_pass(d)


def _mlstm_call(ctx_p, lat_p, conv_w, ml_gain, layer, *, emit_ctx):
    bsz = lat_p[0].shape[0]
    in_specs = []
    args = []
    t_tot = 0
    for (mqk, mv, mo, g) in (ctx_p, lat_p):
        t = mqk.shape[1]
        n = t // CHUNK
        t_tot += t
        in_specs += [
            pl.BlockSpec((None, t, ML_DIM), lambda b, h: (b, 0, h)),
            pl.BlockSpec((None, t, ML_DIM), lambda b, h: (b, 0, ML_HEADS + h)),
            pl.BlockSpec((None, t, ML_DIM), lambda b, h: (b, 0, h)),
            pl.BlockSpec((None, t, ML_DIM), lambda b, h: (b, 0, h)),
            pl.BlockSpec((None, n, GATE_ROWS, CHUNK), lambda b, h: (b, 0, h, 0)),
        ]
        args += [mqk, mqk, mv, mo, g]
    in_specs += [
        pl.BlockSpec((None, CONV_WIDTH, ML_DIM), lambda b, h: (layer, 0, h)),
        pl.BlockSpec((None, CONV_WIDTH, ML_DIM), lambda b, h: (layer, 0, ML_HEADS + h)),
        pl.BlockSpec((None, 1, ML_DIM), lambda b, h: (layer, 0, h)),
    ]
    args += [conv_w, conv_w, ml_gain]
    t_c, t_l = ctx_p[0].shape[1], lat_p[0].shape[1]
    out_l_shape = jax.ShapeDtypeStruct((bsz, t_l, ML_WIDTH), BF16)
    out_l_spec = pl.BlockSpec((None, t_l, ML_DIM), lambda b, h: (b, 0, h))
    if emit_ctx:
        out_shape = (jax.ShapeDtypeStruct((bsz, t_c, ML_WIDTH), BF16), out_l_shape)
        out_specs = (pl.BlockSpec((None, t_c, ML_DIM), lambda b, h: (b, 0, h)), out_l_spec)
    else:
        out_shape = out_l_shape
        out_specs = out_l_spec
    n_tot = t_tot // CHUNK
    res = pl.pallas_call(
        functools.partial(_mlstm_kernel, emit_ctx=emit_ctx),
        grid=(bsz, ML_HEADS),
        in_specs=in_specs,
        out_specs=out_specs,
        out_shape=out_shape,
        scratch_shapes=[
            pltpu.VMEM((t_tot, ML_DIM), BF16),
            pltpu.VMEM((2, n_tot, GATE_ROWS, CHUNK), F32),
            pltpu.VMEM((2, n_tot, CHUNK, 2 * ML_DIM), F32),
            pltpu.VMEM((2, n_tot, ML_DIM, 2 * ML_DIM), F32),
            pltpu.VMEM((2, n_tot, CHUNK, 128), F32),
            pltpu.VMEM((2, n_tot, CHUNK, 128), F32),
            pltpu.VMEM((2, n_tot, GATE_ROWS, CHUNK), F32),
            pltpu.VMEM((t_tot, ML_DIM), F32),
            pltpu.VMEM((ML_DIM, 2 * ML_DIM), F32),
            pltpu.VMEM((8, 128), F32),
        ],
        compiler_params=_cparams(2),
        name="mlstm_emit" if emit_ctx else "mlstm_last",
    )(*args)
    if emit_ctx:
        return res
    return None, res


def _outmlp_kernel(*refs, final, ffn_chunk):
    x_ref, att_ref, mem_ref, mod_ref, wo_ref, gain_ref, w1_ref, w2_ref = refs[:8]
    if final:
        gf_ref, o_ref = refs[8:]
    else:
        o_ref = refs[8]
    mod = mod_ref[...]
    y = _dot(att_ref[...], wo_ref[0:ATTN_WIDTH, :]) + _dot(mem_ref[...], wo_ref[ATTN_WIDTH:D_MODEL, :])
    x1 = x_ref[...] + mod[2:3] * y
    h = _rms_modulate(x1, gain_ref[...], mod[3:4], mod[4:5]).astype(BF16)
    def hidden(j):
        f = jnp.maximum(_dot(h, w1_ref[:, ffn_chunk * j:ffn_chunk * (j + 1)]), 0.0)
        return (f * f).astype(BF16)

    n_ffn = FFN_DIM // ffn_chunk
    acc = jnp.zeros(x1.shape, F32)
    f_next = hidden(0)
    for j in range(n_ffn):
        f_cur = f_next
        if j + 1 < n_ffn:
            f_next = hidden(j + 1)
        acc = acc + _dot(f_cur, w2_ref[ffn_chunk * j:ffn_chunk * (j + 1), :])
    x2 = x1 + mod[5:6] * acc
    if final:
        ms = jnp.mean(x2 * x2, axis=-1, keepdims=True)
        x2 = x2 * lax.rsqrt(ms + NORM_EPS) * gf_ref[...]
    o_ref[...] = x2


def _outmlp_call(x, att, mem, mod, mod_row, wo, gain, w1, w2, layer, gf, *, tm, ffn_chunk=1024):
    bsz, t, d = x.shape
    final = gf is not None
    if mod_row is None:
        mod_map = lambda b, i: (b, 0, 0)
    else:
        mod_map = lambda b, i: (mod_row, 0, 0)
    in_specs = [
        pl.BlockSpec((None, tm, d), lambda b, i: (b, i, 0)),
        pl.BlockSpec((None, tm, ATTN_WIDTH), lambda b, i: (b, i, 0)),
        pl.BlockSpec((None, tm, ML_WIDTH), lambda b, i: (b, i, 0)),
        pl.BlockSpec((None, N_MOD, d), mod_map),
        _const_spec((None, d, d), lambda b, i: (layer, 0, 0)),
        _const_spec((None, 1, d), lambda b, i: (layer, 0, 0)),
        _const_spec((None, d, FFN_DIM), lambda b, i: (layer, 0, 0)),
        _const_spec((None, FFN_DIM, d), lambda b, i: (layer, 0, 0)),
    ]
    args = [x, att, mem, mod, wo, gain, w1, w2]
    if final:
        in_specs.append(_const_spec((1, d), lambda b, i: (0, 0)))
        args.append(gf)
    return pl.pallas_call(
        functools.partial(_outmlp_kernel, final=final, ffn_chunk=ffn_chunk),
        grid=(bsz, t // tm),
        in_specs=in_specs,
        out_specs=pl.BlockSpec((None, tm, d), lambda b, i: (b, i, 0)),
        out_shape=jax.ShapeDtypeStruct((bsz, t, d), F32),
        compiler_params=_cparams(2),
        name="outproj_mlp_final" if final else "outproj_mlp",
    )(*args)


def _deinterleave(a, heads):
    lead = a.shape[:-1]
    return a.reshape(*lead, heads, HEAD_DIM // 2, 2).swapaxes(-1, -2).reshape(*lead, heads * HEAD_DIM)


def _relayout_projection(w_in):
    depth, d, _ = w_in.shape
    w = w_in.astype(BF16)
    a_q, a_k, a_v = 0, ATTN_WIDTH, ATTN_WIDTH + KV_WIDTH
    m_qk = ATTN_WIDTH + 2 * KV_WIDTH
    m_v = m_qk + 2 * ML_WIDTH
    m_o = m_v + ML_WIDTH
    gates = m_o + ML_WIDTH
    g = w[..., gates:gates + N_GATES].reshape(depth, d, 4, ML_HEADS).swapaxes(-1, -2)
    g = jnp.pad(g, ((0, 0), (0, 0), (0, 0), (0, GATE_ROWS - 4))).reshape(depth, d, ML_HEADS * GATE_ROWS)
    g = jnp.pad(g, ((0, 0), (0, 0), (0, 128 - ML_HEADS * GATE_ROWS)))
    return jnp.concatenate([
        w[..., m_qk:m_qk + 2 * ML_WIDTH],
        _deinterleave(w[..., a_q:a_q + ATTN_WIDTH], ATTN_HEADS),
        w[..., m_v:m_v + ML_WIDTH],
        w[..., m_o:m_o + ML_WIDTH],
        _deinterleave(w[..., a_k:a_k + KV_WIDTH], KV_HEADS),
        w[..., a_v:a_v + KV_WIDTH],
        g], axis=-1)


def _rope_tables(t):
    rows = t // GRID_W
    row_idx = jnp.repeat(jnp.arange(rows, dtype=F32), GRID_W)
    col_idx = jnp.tile(jnp.arange(GRID_W, dtype=F32), rows)
    inv_freq = jnp.power(ROPE_THETA, -jnp.arange(0, ROPE_AXIS_DIM, 2, dtype=F32) / ROPE_AXIS_DIM)
    ang = jnp.concatenate([row_idx[:, None] * inv_freq, col_idx[:, None] * inv_freq], axis=-1)
    cos, sin = jnp.cos(ang), jnp.sin(ang)
    cos128 = jnp.tile(cos, (1, 4))
    sin128 = jnp.tile(jnp.concatenate([-sin, sin], axis=-1), (1, 2))
    return cos128, sin128


def _pick_tile(t, pref):
    tm = min(pref, t)
    while t % tm:
        tm //= 2
    return tm


def kernel(x, c, ctx, c_ctx, w_ada, b_ada, norm_mix, norm_mlp, w_in, b_gates, conv_qk,
           q_norm, k_norm, mlstm_norm, w_out, w_mlp_in, w_mlp_out, norm_final):
    bsz, t, d = x.shape
    t_c = ctx.shape[1]
    depth = w_ada.shape[0]
    assert d == D_MODEL and t % CHUNK == 0 and t_c % CHUNK == 0 and t % GRID_W == 0

    rows = -(-(bsz + 1) // 8) * 8
    cvec = jnp.zeros((rows, d), F32).at[:bsz].set(c).at[bsz].set(c_ctx)
    mod = _ada_call(cvec, w_ada, b_ada).reshape(depth, rows, N_MOD, d)

    w_all = _relayout_projection(w_in)
    assert w_all.shape[-1] == W_COLS
    qg = jnp.tile(_deinterleave(q_norm, 1), (1, 2)).reshape(depth, 1, 128)
    kg = jnp.tile(_deinterleave(k_norm, 1), (1, 2)).reshape(depth, 1, 128)
    norm_mix = norm_mix.reshape(depth, 1, d)
    norm_mlp = norm_mlp.reshape(depth, 1, d)
    mlstm_norm = mlstm_norm.reshape(depth, 1, ML_WIDTH)
    bg = jnp.zeros((depth, ML_HEADS, GATE_ROWS), F32).at[:, :, 0:4].set(
        b_gates.reshape(depth, 4, ML_HEADS).transpose(0, 2, 1)).reshape(depth, ML_HEADS * GATE_ROWS, 1)
    wo = w_out.astype(BF16)
    w1 = w_mlp_in.astype(BF16)
    w2 = w_mlp_out.astype(BF16)
    cos128, sin128 = _rope_tables(t)
    gf = norm_final.reshape(1, d)

    tm_l = _pick_tile(t, 512)
    tm_in = _pick_tile(t, 1024)
    tm_c = _pick_tile(t_c, 512)
    tq_l = _pick_tile(t, 1024)
    tq_c = _pick_tile(t_c, 256)

    for layer in range(depth):
        emit_ctx = layer < depth - 1
        mod_l = mod[layer]
        qt_l, k_l, vt_l, mqk_l, mv_l, mo_l, g_l = _inproj_call(
            x, mod_l, None, norm_mix, w_all, layer, qg, kg, bg, cos128, sin128, use_rope=True, tm=tm_in)
        qt_c, k_c, vt_c, mqk_c, mv_c, mo_c, g_c = _inproj_call(
            ctx, mod_l, bsz, norm_mix, w_all, layer, qg, kg, bg, cos128, sin128, use_rope=False, tm=tm_c)
        att_l = _attn_call(qt_l, [k_c, k_l], [vt_c, vt_l], tq=tq_l)
        mem_c, mem_l = _mlstm_call((mqk_c, mv_c, mo_c, g_c), (mqk_l, mv_l, mo_l, g_l),
                                   conv_qk, mlstm_norm, layer, emit_ctx=emit_ctx)
        x = _outmlp_call(x, att_l, mem_l, mod_l, None, wo, norm_mlp, w1, w2, layer,
                         None if emit_ctx else gf, tm=tm_l)
        if emit_ctx:
            att_c = _attn_call(qt_c, [k_c], [vt_c], tq=tq_c)
            ctx = _outmlp_call(ctx, att_c, mem_c, mod_l, bsz, wo, norm_mlp, w1, w2, layer, None, tm=tm_c)
    return x
```

```python
import functools

import jax
import jax.numpy as jnp
from jax import lax
from jax.experimental import pallas as pl
from jax.experimental.pallas import tpu as pltpu

D_MODEL = 1024
N_MOD = 6
GRID_W = 64
ATTN_WIDTH = 512
ATTN_HEADS = 8
HEAD_DIM = 64
KV_HEADS = 2
ATTN_GROUP = ATTN_HEADS // KV_HEADS
KV_WIDTH = KV_HEADS * HEAD_DIM
ROPE_THETA = 10000.0
ROPE_AXIS_DIM = HEAD_DIM // 2
ML_WIDTH = 512
ML_HEADS = 4
ML_DIM = 128
CHUNK = 128
CONV_WIDTH = 3
N_GATES = 4 * ML_HEADS
GATE_ROWS = 8
FFN_DIM = 4 * D_MODEL
NORM_EPS = 1e-6
LOG2_E = 1.4426950408889634

C_MQK = 0
C_Q = C_MQK + 2 * ML_WIDTH
C_MV = C_Q + ATTN_WIDTH
C_MO = C_MV + ML_WIDTH
C_K = C_MO + ML_WIDTH
C_V = C_K + KV_WIDTH
C_G = C_V + KV_WIDTH
W_COLS = C_G + 128

VMEM_LIMIT_BYTES = 56 * 1024 * 1024

BF16 = jnp.bfloat16
F32 = jnp.float32


def _cparams(n_grid):
    return pltpu.CompilerParams(
        dimension_semantics=("arbitrary",) * n_grid,
        vmem_limit_bytes=VMEM_LIMIT_BYTES)


def _const_spec(shape, index_map):
    return pl.BlockSpec(shape, index_map, pipeline_mode=pl.Buffered(1))


def _dot(a, b):
    return jnp.dot(a, b, preferred_element_type=F32)


def _sigmoid(x):
    return 1.0 / (1.0 + jnp.exp(-x))


def _log_sigmoid(x):
    return jnp.minimum(x, 0.0) - jnp.log(1.0 + jnp.exp(-jnp.abs(x)))


def _group_ones(n, group):
    r = lax.broadcasted_iota(jnp.int32, (n, n), 0) // group
    c = lax.broadcasted_iota(jnp.int32, (n, n), 1) // group
    return jnp.where(r == c, 1.0, 0.0).astype(BF16)


def _split_dot(a, ones_mat):
    hi = a.astype(BF16)
    lo = (a - hi.astype(F32)).astype(BF16)
    return _dot(hi, ones_mat) + _dot(lo, ones_mat)


def _ada_kernel(c_ref, w_ref, b_ref, o_ref):
    c = c_ref[...]
    sc = (c * _sigmoid(c)).astype(BF16)
    o_ref[0] = _dot(sc, w_ref[0].astype(BF16)) + b_ref[0]


def _ada_call(cvec, w_ada, b_ada):
    depth, d, n = w_ada.shape
    rows = cvec.shape[0]
    tn = n // 4
    return pl.pallas_call(
        _ada_kernel,
        grid=(depth, n // tn),
        in_specs=[
            pl.BlockSpec((rows, d), lambda l, j: (0, 0)),
            pl.BlockSpec((1, d, tn), lambda l, j: (l, 0, j)),
            pl.BlockSpec((1, 1, tn), lambda l, j: (l, 0, j)),
        ],
        out_specs=pl.BlockSpec((1, rows, tn), lambda l, j: (l, 0, j)),
        out_shape=jax.ShapeDtypeStruct((depth, rows, n), F32),
        compiler_params=_cparams(2),
        name="ada_mod",
    )(cvec, w_ada, b_ada.reshape(depth, 1, n))


def _rms_modulate(x, gain, shift, scale):
    ms = jnp.mean(x * x, axis=-1, keepdims=True)
    y = x * lax.rsqrt(ms + NORM_EPS) * gain
    return y * (1.0 + scale) + shift


def _swap32(x):
    lane = lax.broadcasted_iota(jnp.int32, x.shape, 1)
    up = pltpu.roll(x, 96, 1)
    down = pltpu.roll(x, 32, 1)
    return jnp.where((lane % HEAD_DIM) < ROPE_AXIS_DIM, up, down)


def _head_sumsq(t):
    return _dot((t * t).astype(BF16), _group_ones(t.shape[1], HEAD_DIM))


def _head_norm_rope(t, ss, gain, cos, sin, use_rope):
    tn = t * lax.rsqrt(ss * (1.0 / HEAD_DIM) + NORM_EPS) * gain
    if use_rope:
        tn = tn * cos + _swap32(tn) * sin
    return tn


def _inproj_kernel(x_ref, mod_ref, gain_ref, w_ref, qg_ref, kg_ref, bg_ref, cos_ref, sin_ref,
                   qt_out, k_out, vt_out, mqk_out, mv_out, mo_out, g_out, *, use_rope):
    x = x_ref[...]
    mod = mod_ref[...]
    h = _rms_modulate(x, gain_ref[...], mod[0:1], mod[1:2]).astype(BF16)
    tm = x.shape[0]
    cos = cos_ref[...] if use_rope else None
    sin = sin_ref[...] if use_rope else None

    q = _dot(h, w_ref[:, C_Q:C_Q + ATTN_WIDTH])
    kvg = _dot(h, w_ref[:, C_K:W_COLS])
    k, v, g = kvg[:, 0:KV_WIDTH], kvg[:, KV_WIDTH:2 * KV_WIDTH], kvg[:, 2 * KV_WIDTH:]
    mqk_out[...] = _dot(h, w_ref[:, C_MQK:C_MQK + 2 * ML_WIDTH])

    qscale = HEAD_DIM ** -0.5 * LOG2_E
    q_ss = [_head_sumsq(q[:, 256 * j:256 * (j + 1)]) for j in range(ATTN_WIDTH // 256)]
    k_ss = _head_sumsq(k)
    q_blocks = [_head_norm_rope(q[:, 128 * j:128 * (j + 1)], q_ss[j // 2][:, 128 * (j % 2):128 * (j % 2 + 1)],
                                qg_ref[...], cos, sin, use_rope)
                for j in range(ATTN_WIDTH // 128)]
    k_normed = _head_norm_rope(k, k_ss, kg_ref[...], cos, sin, use_rope)

    mv_out[...] = _dot(h, w_ref[:, C_MV:C_MV + ML_WIDTH]).astype(BF16)
    mo_out[...] = _sigmoid(_dot(h, w_ref[:, C_MO:C_MO + ML_WIDTH])).astype(BF16)
    vt = v.T
    pad_row = lax.broadcasted_iota(jnp.int32, (HEAD_DIM, tm), 0)
    pad = jnp.where(pad_row == 0, 1.0, 0.0)
    for kvh in range(KV_HEADS):
        vt_out[128 * kvh:128 * kvh + HEAD_DIM, :] = vt[HEAD_DIM * kvh:HEAD_DIM * (kvh + 1)].astype(BF16)
        vt_out[128 * kvh + HEAD_DIM:128 * (kvh + 1), :] = pad.astype(BF16)

    for j, blk in enumerate(q_blocks):
        qt_out[128 * j:128 * (j + 1), :] = (blk * qscale).T.astype(BF16)
    k_out[...] = k_normed.astype(BF16)

    gt = g.T[0:ML_HEADS * GATE_ROWS, :] + bg_ref[...]
    row = lax.broadcasted_iota(jnp.int32, gt.shape, 0) % GATE_ROWS
    gt = jnp.where((row == 1) | (row == 3), _log_sigmoid(gt), gt)
    for j in range(tm // CHUNK):
        g_out[j] = gt[:, CHUNK * j:CHUNK * (j + 1)]


def _inproj_call(x, mod, mod_row, gain, w_all, layer, qg, kg, bg, cos, sin, *, use_rope, tm):
    bsz, t, d = x.shape
    nt = t // tm
    nc = t // CHUNK
    cpt = tm // CHUNK
    if mod_row is None:
        mod_map = lambda b, i: (b, 0, 0)
    else:
        mod_map = lambda b, i: (mod_row, 0, 0)
    tok = lambda w: pl.BlockSpec((None, tm, w), lambda b, i: (b, i, 0))
    out_shapes = (
        jax.ShapeDtypeStruct((bsz, ATTN_WIDTH, t), BF16),
        jax.ShapeDtypeStruct((bsz, t, KV_WIDTH), BF16),
        jax.ShapeDtypeStruct((bsz, KV_HEADS * 128, t), BF16),
        jax.ShapeDtypeStruct((bsz, t, 2 * ML_WIDTH), F32),
        jax.ShapeDtypeStruct((bsz, t, ML_WIDTH), BF16),
        jax.ShapeDtypeStruct((bsz, t, ML_WIDTH), BF16),
        jax.ShapeDtypeStruct((bsz, nc, ML_HEADS * GATE_ROWS, CHUNK), F32),
    )
    out_specs = (
        pl.BlockSpec((None, ATTN_WIDTH, tm), lambda b, i: (b, 0, i)),
        tok(KV_WIDTH),
        pl.BlockSpec((None, KV_HEADS * 128, tm), lambda b, i: (b, 0, i)),
        tok(2 * ML_WIDTH),
        tok(ML_WIDTH),
        tok(ML_WIDTH),
        pl.BlockSpec((None, cpt, ML_HEADS * GATE_ROWS, CHUNK), lambda b, i: (b, i, 0, 0)),
    )
    in_specs = [
        tok(d),
        pl.BlockSpec((None, N_MOD, d), mod_map),
        _const_spec((None, 1, d), lambda b, i: (layer, 0, 0)),
        _const_spec((None, d, W_COLS), lambda b, i: (layer, 0, 0)),
        _const_spec((None, 1, 128), lambda b, i: (layer, 0, 0)),
        _const_spec((None, 1, 128), lambda b, i: (layer, 0, 0)),
        _const_spec((None, ML_HEADS * GATE_ROWS, 1), lambda b, i: (layer, 0, 0)),
        pl.BlockSpec((tm, 128), lambda b, i: (i, 0)),
        pl.BlockSpec((tm, 128), lambda b, i: (i, 0)),
    ]
    return pl.pallas_call(
        functools.partial(_inproj_kernel, use_rope=use_rope),
        grid=(bsz, nt),
        in_specs=in_specs,
        out_specs=out_specs,
        out_shape=out_shapes,
        compiler_params=_cparams(2),
        name="inproj_rope" if use_rope else "inproj_ctx",
    )(x, mod, gain, w_all, qg, kg, bg, cos, sin)


def _attn_kernel(*refs, n_seg, q_block):
    qt_ref = refs[0]
    k_refs = refs[1:1 + n_seg]
    vt_refs = refs[1 + n_seg:1 + 2 * n_seg]
    o_ref = refs[1 + 2 * n_seg]
    k_all, vt_all, st_ref = refs[2 + 2 * n_seg:]
    s_tot = k_all.shape[0]
    tq = qt_ref.shape[1]
    cols = ATTN_GROUP * q_block

    @pl.when(pl.program_id(1) == 0)
    def _():
        base = 0
        for seg in range(n_seg):
            s_len = k_refs[seg].shape[0]
            k_all[base:base + s_len, :] = k_refs[seg][...]
            vt_all[:, base:base + s_len] = vt_refs[seg][...]
            base += s_len

    groups = [(qb, kvh) for qb in range(tq // q_block) for kvh in range(KV_HEADS)]
    zeros = jnp.zeros((HEAD_DIM, q_block), BF16)

    def padded_queries(qb, kvh):
        pieces = []
        for g in range(ATTN_GROUP):
            row0 = HEAD_DIM * (ATTN_GROUP * kvh + g)
            qg = qt_ref[row0:row0 + HEAD_DIM, q_block * qb:q_block * (qb + 1)]
            pieces.append(jnp.concatenate([qg, zeros] if kvh == 0 else [zeros, qg], axis=0))
        return jnp.concatenate(pieces, axis=1)

    def finalize(qb, kvh, acc):
        ot = acc[0:HEAD_DIM] * (1.0 / acc[HEAD_DIM:HEAD_DIM + 1])
        for pair in range(ATTN_GROUP // 2):
            two = jnp.concatenate([ot[:, q_block * (2 * pair):q_block * (2 * pair + 1)],
                                   ot[:, q_block * (2 * pair + 1):q_block * (2 * pair + 2)]], axis=0)
            lane0 = HEAD_DIM * (ATTN_GROUP * kvh + 2 * pair)
            o_ref[q_block * qb:q_block * (qb + 1), lane0:lane0 + 2 * HEAD_DIM] = two.T.astype(BF16)

    m_prev = None
    for j in range(len(groups) + 1):
        qz = padded_queries(*groups[j]) if j < len(groups) else None
        m_next = None
        if qz is not None:
            st = _dot(k_all[...], qz)
            st_ref[j % 2] = st
            m_next = jnp.max(jnp.max(st.reshape(s_tot // 8, 8, cols), axis=0), axis=0, keepdims=True)
        if j >= 1:
            qb, kvh = groups[j - 1]
            p = jnp.exp2(st_ref[(j - 1) % 2] - m_prev).astype(BF16)
            finalize(qb, kvh, _dot(vt_all[128 * kvh:128 * (kvh + 1), :], p))
        m_prev = m_next


def _attn_call(qt, ks, vts, *, tq, q_block=128):
    bsz, _, t = qt.shape
    n_seg = len(ks)
    s_tot = sum(k.shape[1] for k in ks)
    q_block = min(q_block, tq)
    cols = ATTN_GROUP * q_block
    in_specs = [pl.BlockSpec((None, ATTN_WIDTH, tq), lambda b, i: (b, 0, i))]
    for k in ks:
        in_specs.append(pl.BlockSpec((None, k.shape[1], KV_WIDTH), lambda b, i: (b, 0, 0)))
    for vt in vts:
        in_specs.append(pl.BlockSpec((None, KV_HEADS * 128, vt.shape[2]), lambda b, i: (b, 0, 0)))
    return pl.pallas_call(
        functools.partial(_attn_kernel, n_seg=n_seg, q_block=q_block),
        grid=(bsz, t // tq),
        in_specs=in_specs,
        out_specs=pl.BlockSpec((None, tq, ATTN_WIDTH), lambda b, i: (b, i, 0)),
        out_shape=jax.ShapeDtypeStruct((bsz, t, ATTN_WIDTH), BF16),
        scratch_shapes=[
            pltpu.VMEM((s_tot, KV_WIDTH), BF16),
            pltpu.VMEM((KV_HEADS * 128, s_tot), BF16),
            pltpu.VMEM((2, s_tot, cols), F32),
        ],
        compiler_params=_cparams(2),
        name="attention_%dseg" % n_seg,
    )(qt, *ks, *vts)


def _mlstm_kernel(*refs, emit_ctx):
    (mq_c, mk_c, mv_c, mo_c, g_c, mq_l, mk_l, mv_l, mo_l, g_l, cw_q, cw_k, gain_ref) = refs[:13]
    if emit_ctx:
        out_c, out_l = refs[13:15]
        scratch = refs[15:]
    else:
        out_c = None
        out_l = refs[13]
        scratch = refs[14:]
    q_s, cum_s, x_s, u_s, pm_s, nb_s, sc_s, hf_s, ct_s, m_s = scratch

    L = CHUNK
    segs = []
    chunk_base = 0
    for (mq, mk, mv, mo, g, out, emit) in ((mq_c, mk_c, mv_c, mo_c, g_c, out_c, emit_ctx),
                                           (mq_l, mk_l, mv_l, mo_l, g_l, out_l, True)):
        n = mq.shape[0] // L
        segs.append(dict(mq=mq, mk=mk, mv=mv, mo=mo, g=g, out=out, emit=emit, n=n, cb=chunk_base))
        chunk_base += n

    ri = lax.broadcasted_iota(jnp.int32, (L, L), 0)
    ci = lax.broadcasted_iota(jnp.int32, (L, L), 1)
    ones_le = jnp.where(ri <= ci, 1.0, 0.0).astype(BF16)
    ones_ge = jnp.where(ri >= ci, 1.0, 0.0).astype(BF16)
    masks = (ci <= ri, ci >= ri)
    row_i = lax.broadcasted_iota(jnp.int32, (L, ML_DIM), 0)
    ones_blk = jnp.ones((L, 128), BF16)
    kscale = ML_DIM ** -0.5

    def unroll_of(n, want):
        while n % want:
            want //= 2
        return want

    def conv_silu(src, cw, c, n):
        t0 = pl.multiple_of(c * L, L)
        xc = src[pl.ds(t0, L), :]
        prev = src[pl.ds(jnp.maximum(t0 - 1, 0), 1), :] * jnp.where(c > 0, 1.0, 0.0)
        nxt = src[pl.ds(jnp.minimum(t0 + L, n * L - 1), 1), :] * jnp.where(c < n - 1, 1.0, 0.0)
        xm = jnp.where(row_i == 0, prev, pltpu.roll(xc, 1, 0))
        xp = jnp.where(row_i == L - 1, nxt, pltpu.roll(xc, L - 1, 0))
        y = xm * cw[0:1, :] + xc * cw[1:2, :] + xp * cw[2:3, :]
        return y * _sigmoid(y)

    for sg in segs:
        n, cb = sg["n"], sg["cb"]
        g2 = sg["g"][...].reshape(n * GATE_ROWS, L) * LOG2_E
        cum_s[0, cb:cb + n] = _split_dot(g2, ones_le).reshape(n, GATE_ROWS, L)
        cum_s[1, cb:cb + n] = _split_dot(g2, ones_ge).reshape(n, GATE_ROWS, L)

    def gate_rows(d, sg, c):
        gr = sg["g"][c] * LOG2_E
        cum = cum_s[d, sg["cb"] + c]
        if d == 0:
            return gr[0:1], gr[1:2], cum[1:2], cum[1:2, L - 1:L]
        return gr[2:3], gr[3:4], cum[3:4], cum[3:4, 0:1]

    def state_free_pass():
        for sg in segs:
            group = unroll_of(sg["n"], 8)

            def body(i, carry, sg=sg, group=group):
                pending = []
                for j in range(group):
                    c = i * group + j
                    cg = sg["cb"] + c
                    t0 = pl.multiple_of(c * L, L)
                    t0g = pl.multiple_of(cg * L, L)
                    qc = conv_silu(sg["mq"], cw_q, c, sg["n"]).astype(BF16)
                    ktf = (conv_silu(sg["mk"], cw_k, c, sg["n"]) * kscale).T
                    ktc = ktf.astype(BF16)
                    q_s[pl.ds(t0g, L), :] = qc
                    v_aug = jnp.concatenate([sg["mv"][pl.ds(t0, L), :], ones_blk], axis=1)
                    s0 = _dot(qc, ktc) if sg["emit"] else None
                    kws, es = [], []
                    for d in range(2):
                        ig, lf, brow, total = gate_rows(d, sg, c)
                        a = ig - brow
                        amax = jnp.max(a, axis=-1, keepdims=True)
                        kws.append((ktf * jnp.exp2(a - amax)).astype(BF16))
                        sc_s[d, cg] = jnp.concatenate(
                            [jnp.broadcast_to(amax, (1, L)), jnp.broadcast_to(total, (1, L)),
                             jnp.zeros((GATE_ROWS - 2, L), F32)], axis=0)
                        if sg["emit"]:
                            a_vis = jnp.where(masks[d], a, -jnp.inf)
                            pm = jnp.broadcast_to(jnp.max(a_vis, axis=-1, keepdims=True), (L, L))
                            pm_s[d, cg] = pm
                            bcol = jnp.sum(jnp.where(masks[d], lf, 0.0), axis=-1, keepdims=True)
                            nb_s[d, cg] = -jnp.broadcast_to(bcol, (L, L)) - pm
                            es.append(jnp.exp2(a_vis - pm))
                    pending.append((cg, s0, kws, es, v_aug))
                for cg, s0, kws, es, v_aug in pending:
                    lhs = jnp.concatenate(kws + [(s0 * e).astype(BF16) for e in es], axis=0)
                    res = _dot(lhs, v_aug)
                    for d in range(2):
                        u_s[d, cg] = res[L * d:L * (d + 1)]
                        if es:
                            x_s[d, cg] = res[L * (2 + d):L * (3 + d)]
                return carry
            lax.fori_loop(0, sg["n"] // group, body, 0)

    def sequential_pass(d):
        ct_s[...] = jnp.zeros(ct_s.shape, F32)
        m_s[...] = jnp.zeros(m_s.shape, F32)
        for sg in segs:
            n = sg["n"]
            group = unroll_of(n, 8)

            def body(i, carry, sg=sg, n=n, group=group):
                m_prev = m_s[0:1, 0:1]
                ct = ct_s[...]
                pending = []
                for j in range(group):
                    c = i * group + j if d == 0 else n - 1 - (i * group + j)
                    cg = sg["cb"] + c
                    t0 = pl.multiple_of(c * L, L)
                    t0g = pl.multiple_of(cg * L, L)
                    sc = sc_s[d, cg]
                    amax, total = sc[0:1, 0:1], sc[1:2, 0:1]
                    m_last = jnp.maximum(m_prev, amax)
                    if sg["emit"]:
                        inter = _dot(q_s[pl.ds(t0g, L), :], ct.astype(BF16))
                        pending.append((cg, t0, t0g, inter, m_prev))
                    ct = jnp.exp2(m_prev - m_last) * ct + jnp.exp2(amax - m_last) * u_s[d, cg]
                    m_prev = total + m_last
                ct_s[...] = ct
                m_s[...] = jnp.broadcast_to(m_prev, m_s.shape)
                for cg, t0, t0g, inter, m_in in pending:
                    pm = pm_s[d, cg]
                    mx = jnp.maximum(pm, m_in)
                    dpm = pm - mx
                    r = jnp.exp2(dpm)
                    wi = jnp.exp2(m_in - mx)
                    x = x_s[d, cg]
                    num = r * x[:, 0:ML_DIM] + wi * inter[:, 0:ML_DIM]
                    den = r * x[:, ML_DIM:] + wi * inter[:, ML_DIM:]
                    h = num / jnp.maximum(jnp.abs(den), jnp.exp2(nb_s[d, cg] + dpm))
                    if d == 0:
                        hf_s[pl.ds(t0g, L), :] = h
                    else:
                        hsum = hf_s[pl.ds(t0g, L), :] + h
                        ms = jnp.mean(hsum * hsum, axis=-1, keepdims=True)
                        hn = hsum * lax.rsqrt(ms + NORM_EPS) * gain_ref[...]
                        sg["out"][pl.ds(t0, L), :] = (sg["mo"][pl.ds(t0, L), :].astype(F32) * hn).astype(BF16)
                return carry

            lax.fori_loop(0, n // group, body, 0)

    state_free_pass()
    for d in range(2):
        sequential_pass(d)


def _mlstm_call(ctx_p, lat_p, conv_w, ml_gain, layer, *, emit_ctx):
    bsz = lat_p[0].shape[0]
    in_specs = []
    args = []
    t_tot = 0
    for (mqk, mv, mo, g) in (ctx_p, lat_p):
        t = mqk.shape[1]
        n = t // CHUNK
        t_tot += t
        in_specs += [
            pl.BlockSpec((None, t, ML_DIM), lambda b, h: (b, 0, h)),
            pl.BlockSpec((None, t, ML_DIM), lambda b, h: (b, 0, ML_HEADS + h)),
            pl.BlockSpec((None, t, ML_DIM), lambda b, h: (b, 0, h)),
            pl.BlockSpec((None, t, ML_DIM), lambda b, h: (b, 0, h)),
            pl.BlockSpec((None, n, GATE_ROWS, CHUNK), lambda b, h: (b, 0, h, 0)),
        ]
        args += [mqk, mqk, mv, mo, g]
    in_specs += [
        pl.BlockSpec((None, CONV_WIDTH, ML_DIM), lambda b, h: (layer, 0, h)),
        pl.BlockSpec((None, CONV_WIDTH, ML_DIM), lambda b, h: (layer, 0, ML_HEADS + h)),
        pl.BlockSpec((None, 1, ML_DIM), lambda b, h: (layer, 0, h)),
    ]
    args += [conv_w, conv_w, ml_gain]
    t_c, t_l = ctx_p[0].shape[1], lat_p[0].shape[1]
    out_l_shape = jax.ShapeDtypeStruct((bsz, t_l, ML_WIDTH), BF16)
    out_l_spec = pl.BlockSpec((None, t_l, ML_DIM), lambda b, h: (b, 0, h))
    if emit_ctx:
        out_shape = (jax.ShapeDtypeStruct((bsz, t_c, ML_WIDTH), BF16), out_l_shape)
        out_specs = (pl.BlockSpec((None, t_c, ML_DIM), lambda b, h: (b, 0, h)), out_l_spec)
    else:
        out_shape = out_l_shape
        out_specs = out_l_spec
    n_tot = t_tot // CHUNK
    res = pl.pallas_call(
        functools.partial(_mlstm_kernel, emit_ctx=emit_ctx),
        grid=(bsz, ML_HEADS),
        in_specs=in_specs,
        out_specs=out_specs,
        out_shape=out_shape,
        scratch_shapes=[
            pltpu.VMEM((t_tot, ML_DIM), BF16),
            pltpu.VMEM((2, n_tot, GATE_ROWS, CHUNK), F32),
            pltpu.VMEM((2, n_tot, CHUNK, 2 * ML_DIM), F32),
            pltpu.VMEM((2, n_tot, ML_DIM, 2 * ML_DIM), F32),
            pltpu.VMEM((2, n_tot, CHUNK, 128), F32),
            pltpu.VMEM((2, n_tot, CHUNK, 128), F32),
            pltpu.VMEM((2, n_tot, GATE_ROWS, CHUNK), F32),
            pltpu.VMEM((t_tot, ML_DIM), F32),
            pltpu.VMEM((ML_DIM, 2 * ML_DIM), F32),
            pltpu.VMEM((8, 128), F32),
        ],
        compiler_params=_cparams(2),
        name="mlstm_emit" if emit_ctx else "mlstm_last",
    )(*args)
    if emit_ctx:
        return res
    return None, res


def _outmlp_kernel(*refs, final, ffn_chunk):
    x_ref, att_ref, mem_ref, mod_ref, wo_ref, gain_ref, w1_ref, w2_ref = refs[:8]
    if final:
        gf_ref, o_ref = refs[8:]
    else:
        o_ref = refs[8]
    mod = mod_ref[...]
    y = _dot(att_ref[...], wo_ref[0:ATTN_WIDTH, :]) + _dot(mem_ref[...], wo_ref[ATTN_WIDTH:D_MODEL, :])
    x1 = x_ref[...] + mod[2:3] * y
    h = _rms_modulate(x1, gain_ref[...], mod[3:4], mod[4:5]).astype(BF16)
    def hidden(j):
        f = jnp.maximum(_dot(h, w1_ref[:, ffn_chunk * j:ffn_chunk * (j + 1)]), 0.0)
        return (f * f).astype(BF16)

    n_ffn = FFN_DIM // ffn_chunk
    acc = jnp.zeros(x1.shape, F32)
    f_next = hidden(0)
    for j in range(n_ffn):
        f_cur = f_next
        if j + 1 < n_ffn:
            f_next = hidden(j + 1)
        acc = acc + _dot(f_cur, w2_ref[ffn_chunk * j:ffn_chunk * (j + 1), :])
    x2 = x1 + mod[5:6] * acc
    if final:
        ms = jnp.mean(x2 * x2, axis=-1, keepdims=True)
        x2 = x2 * lax.rsqrt(ms + NORM_EPS) * gf_ref[...]
    o_ref[...] = x2


def _outmlp_call(x, att, mem, mod, mod_row, wo, gain, w1, w2, layer, gf, *, tm, ffn_chunk=1024):
    bsz, t, d = x.shape
    final = gf is not None
    if mod_row is None:
        mod_map = lambda b, i: (b, 0, 0)
    else:
        mod_map = lambda b, i: (mod_row, 0, 0)
    in_specs = [
        pl.BlockSpec((None, tm, d), lambda b, i: (b, i, 0)),
        pl.BlockSpec((None, tm, ATTN_WIDTH), lambda b, i: (b, i, 0)),
        pl.BlockSpec((None, tm, ML_WIDTH), lambda b, i: (b, i, 0)),
        pl.BlockSpec((None, N_MOD, d), mod_map),
        _const_spec((None, d, d), lambda b, i: (layer, 0, 0)),
        _const_spec((None, 1, d), lambda b, i: (layer, 0, 0)),
        _const_spec((None, d, FFN_DIM), lambda b, i: (layer, 0, 0)),
        _const_spec((None, FFN_DIM, d), lambda b, i: (layer, 0, 0)),
    ]
    args = [x, att, mem, mod, wo, gain, w1, w2]
    if final:
        in_specs.append(_const_spec((1, d), lambda b, i: (0, 0)))
        args.append(gf)
    return pl.pallas_call(
        functools.partial(_outmlp_kernel, final=final, ffn_chunk=ffn_chunk),
        grid=(bsz, t // tm),
        in_specs=in_specs,
        out_specs=pl.BlockSpec((None, tm, d), lambda b, i: (b, i, 0)),
        out_shape=jax.ShapeDtypeStruct((bsz, t, d), F32),
        compiler_params=_cparams(2),
        name="outproj_mlp_final" if final else "outproj_mlp",
    )(*args)


def _deinterleave(a, heads):
    lead = a.shape[:-1]
    return a.reshape(*lead, heads, HEAD_DIM // 2, 2).swapaxes(-1, -2).reshape(*lead, heads * HEAD_DIM)


def _relayout_projection(w_in):
    depth, d, _ = w_in.shape
    w = w_in.astype(BF16)
    a_q, a_k, a_v = 0, ATTN_WIDTH, ATTN_WIDTH + KV_WIDTH
    m_qk = ATTN_WIDTH + 2 * KV_WIDTH
    m_v = m_qk + 2 * ML_WIDTH
    m_o = m_v + ML_WIDTH
    gates = m_o + ML_WIDTH
    g = w[..., gates:gates + N_GATES].reshape(depth, d, 4, ML_HEADS).swapaxes(-1, -2)
    g = jnp.pad(g, ((0, 0), (0, 0), (0, 0), (0, GATE_ROWS - 4))).reshape(depth, d, ML_HEADS * GATE_ROWS)
    g = jnp.pad(g, ((0, 0), (0, 0), (0, 128 - ML_HEADS * GATE_ROWS)))
    return jnp.concatenate([
        w[..., m_qk:m_qk + 2 * ML_WIDTH],
        _deinterleave(w[..., a_q:a_q + ATTN_WIDTH], ATTN_HEADS),
        w[..., m_v:m_v + ML_WIDTH],
        w[..., m_o:m_o + ML_WIDTH],
        _deinterleave(w[..., a_k:a_k + KV_WIDTH], KV_HEADS),
        w[..., a_v:a_v + KV_WIDTH],
        g], axis=-1)


def _rope_tables(t):
    rows = t // GRID_W
    row_idx = jnp.repeat(jnp.arange(rows, dtype=F32), GRID_W)
    col_idx = jnp.tile(jnp.arange(GRID_W, dtype=F32), rows)
    inv_freq = jnp.power(ROPE_THETA, -jnp.arange(0, ROPE_AXIS_DIM, 2, dtype=F32) / ROPE_AXIS_DIM)
    ang = jnp.concatenate([row_idx[:, None] * inv_freq, col_idx[:, None] * inv_freq], axis=-1)
    cos, sin = jnp.cos(ang), jnp.sin(ang)
    cos128 = jnp.tile(cos, (1, 4))
    sin128 = jnp.tile(jnp.concatenate([-sin, sin], axis=-1), (1, 2))
    return cos128, sin128


def _pick_tile(t, pref):
    tm = min(pref, t)
    while t % tm:
        tm //= 2
    return tm


def kernel(x, c, ctx, c_ctx, w_ada, b_ada, norm_mix, norm_mlp, w_in, b_gates, conv_qk,
           q_norm, k_norm, mlstm_norm, w_out, w_mlp_in, w_mlp_out, norm_final):
    bsz, t, d = x.shape
    t_c = ctx.shape[1]
    depth = w_ada.shape[0]
    assert d == D_MODEL and t % CHUNK == 0 and t_c % CHUNK == 0 and t % GRID_W == 0

    rows = -(-(bsz + 1) // 8) * 8
    cvec = jnp.zeros((rows, d), F32).at[:bsz].set(c).at[bsz].set(c_ctx)
    mod = _ada_call(cvec, w_ada, b_ada).reshape(depth, rows, N_MOD, d)

    w_all = _relayout_projection(w_in)
    assert w_all.shape[-1] == W_COLS
    qg = jnp.tile(_deinterleave(q_norm, 1), (1, 2)).reshape(depth, 1, 128)
    kg = jnp.tile(_deinterleave(k_norm, 1), (1, 2)).reshape(depth, 1, 128)
    norm_mix = norm_mix.reshape(depth, 1, d)
    norm_mlp = norm_mlp.reshape(depth, 1, d)
    mlstm_norm = mlstm_norm.reshape(depth, 1, ML_WIDTH)
    bg = jnp.zeros((depth, ML_HEADS, GATE_ROWS), F32).at[:, :, 0:4].set(
        b_gates.reshape(depth, 4, ML_HEADS).transpose(0, 2, 1)).reshape(depth, ML_HEADS * GATE_ROWS, 1)
    wo = w_out.astype(BF16)
    w1 = w_mlp_in.astype(BF16)
    w2 = w_mlp_out.astype(BF16)
    cos128, sin128 = _rope_tables(t)
    gf = norm_final.reshape(1, d)

    tm_l = _pick_tile(t, 512)
    tm_in = _pick_tile(t, 1024)
    tm_c = _pick_tile(t_c, 512)
    tq_l = _pick_tile(t, 1024)
    tq_c = _pick_tile(t_c, 256)

    for layer in range(depth):
        emit_ctx = layer < depth - 1
        mod_l = mod[layer]
        qt_l, k_l, vt_l, mqk_l, mv_l, mo_l, g_l = _inproj_call(
            x, mod_l, None, norm_mix, w_all, layer, qg, kg, bg, cos128, sin128, use_rope=True, tm=tm_in)
        qt_c, k_c, vt_c, mqk_c, mv_c, mo_c, g_c = _inproj_call(
            ctx, mod_l, bsz, norm_mix, w_all, layer, qg, kg, bg, cos128, sin128, use_rope=False, tm=tm_c)
        att_l = _attn_call(qt_l, [k_c, k_l], [vt_c, vt_l], tq=tq_l)
        mem_c, mem_l = _mlstm_call((mqk_c, mv_c, mo_c, g_c), (mqk_l, mv_l, mo_l, g_l),
                                   conv_qk, mlstm_norm, layer, emit_ctx=emit_ctx)
        x = _outmlp_call(x, att_l, mem_l, mod_l, None, wo, norm_mlp, w1, w2, layer,
                         None if emit_ctx else gf, tm=tm_l)
        if emit_ctx:
            att_c = _attn_call(qt_c, [k_c], [vt_c], tq=tq_c)
            ctx = _outmlp_call(ctx, att_c, mem_c, mod_l, bsz, wo, norm_mlp, w1, w2, layer, None, tm=tm_c)
    return x
```
